```python
import math
import jax, jax.numpy as jnp
from jax import lax
import numpy as np

D_MODEL = 1024
BATCH = 8
SEQ = 2048
DEPTH = 4

N_MIXERS = 4
EXPAND = 2
D_INNER = EXPAND * D_MODEL
NORM_EPS = 1e-6
ROPE_THETA = 10000.0
Q_BLOCK = 128

A_HEADS = 16
A_HEAD_DIM = D_INNER // A_HEADS
A_IDX_HEADS = 16
A_IDX_DIM = 64
A_TOPK_MAX = 256
A_SPLITS = (A_HEADS * A_HEAD_DIM, A_HEAD_DIM, A_HEAD_DIM, A_IDX_HEADS * A_IDX_DIM, A_IDX_DIM, A_IDX_HEADS, D_INNER)

B_HEADS = 16
B_HEAD_DIM = D_INNER // B_HEADS
B_CONFIGS = ((128, 1), (512, 4), (2048, 16))

C_HEADS = 8
C_QK_DIM = D_MODEL // C_HEADS
C_V_DIM = D_INNER // C_HEADS
C_CHUNK = 128
C_SPLITS = (C_HEADS * C_QK_DIM, C_HEADS * C_QK_DIM, D_INNER, D_INNER)

D_HEADS = 16
D_K_DIM = D_INNER // D_HEADS
D_V_DIM = D_INNER // D_HEADS
D_CHUNK = 64

kernel_name = "hybrid_dsa_dilated_retnet_hgrn2_trunk"


def _n_layers_of(kind):
    return len(range(kind, DEPTH, N_MIXERS))


def rmsnorm(x, g):
    x32 = x.astype(jnp.float32)
    y = x32 * lax.rsqrt(jnp.mean(x32 * x32, axis=-1, keepdims=True) + NORM_EPS)
    return (y * g.astype(jnp.float32)).astype(x.dtype)


def head_rmsnorm(x):
    x32 = x.astype(jnp.float32)
    return x32 * lax.rsqrt(jnp.mean(x32 * x32, axis=-1, keepdims=True) + NORM_EPS)


def rotary(x, pos):
    dh = x.shape[-1]
    freqs = ROPE_THETA ** (-jnp.arange(0, dh, 2, dtype=jnp.float32) / dh)
    ang = pos.astype(jnp.float32)[:, None] * freqs[None, :]
    cos = jnp.cos(ang)[:, None, :].astype(x.dtype)
    sin = jnp.sin(ang)[:, None, :].astype(x.dtype)
    x1, x2 = jnp.split(x, 2, axis=-1)
    return jnp.concatenate([x1 * cos - x2 * sin, x1 * sin + x2 * cos], axis=-1)


def _split(t, sizes):
    return jnp.split(t, np.cumsum(sizes)[:-1].tolist(), axis=-1)


def dsa_mixer(h, w_in, w_out):
    B_, S, _ = h.shape
    pos = jnp.arange(S)
    q, k, v, iq, ik, iw, z = _split(h @ w_in, A_SPLITS)
    q = rotary(q.reshape(B_, S, A_HEADS, A_HEAD_DIM), pos) * (A_HEAD_DIM ** -0.5)
    k = rotary(k.reshape(B_, S, 1, A_HEAD_DIM), pos)[:, :, 0]
    iq = rotary(iq.reshape(B_, S, A_IDX_HEADS, A_IDX_DIM), pos)
    ik = rotary(ik.reshape(B_, S, 1, A_IDX_DIM), pos)[:, :, 0]
    iw = iw * (A_IDX_HEADS ** -0.5 * A_IDX_DIM ** -0.5)
    top_k = min(A_TOPK_MAX, S // 4)
    nb = S // Q_BLOCK

    def blockify(t):
        return t.reshape(B_, nb, Q_BLOCK, *t.shape[2:]).swapaxes(0, 1)

    def one_block(args):
        qb, iqb, iwb, tb = args
        rel = jax.nn.relu(jnp.einsum('bqhd,bsd->bqsh', iqb, ik))
        score = jnp.einsum('bqsh,bqh->bqs', rel, iwb).astype(jnp.float32)
        causal = pos[None, None, :] <= tb[None, :, None]
        score = jnp.where(causal, score, -jnp.inf)
        _, sel = lax.top_k(score, top_k)
        valid = sel <= tb[None, :, None]
        k_sel = jax.vmap(lambda kk, ii: kk[ii])(k, sel)
        v_sel = jax.vmap(lambda vv, ii: vv[ii])(v, sel)
        s = jnp.einsum('bqhd,bqkd->bhqk', qb, k_sel).astype(jnp.float32)
        s = jnp.where(valid[:, None], s, -jnp.inf)
        p = jax.nn.softmax(s, axis=-1).astype(v.dtype)
        return jnp.einsum('bhqk,bqkd->bqhd', p, v_sel)

    tpos = pos.reshape(nb, Q_BLOCK)
    o = lax.map(one_block, (blockify(q), blockify(iq), blockify(iw), tpos))
    o = o.swapaxes(0, 1).reshape(B_, S, D_INNER)
    return (o * jax.nn.silu(z)) @ w_out


def _dilated_band_attention(q, k, v, window, dilation):
    B_, S, H, dh = q.shape
    L = S // dilation
    span = window // dilation
    blk = Q_BLOCK
    nb = -(-L // blk)
    Lp = nb * blk

    def residues(t):
        t = t.reshape(B_, L, dilation, H, dh).transpose(0, 2, 1, 3, 4)
        return jnp.pad(t, ((0, 0), (0, 0), (0, Lp - L), (0, 0), (0, 0)))

    def band(t):
        tp = jnp.pad(t, ((0, 0), (0, 0), (blk, 0), (0, 0), (0, 0))).reshape(B_, dilation, nb + 1, blk, H, dh)
        return jnp.concatenate([tp[:, :, :-1], tp[:, :, 1:]], axis=3)

    qb = residues(q).reshape(B_, dilation, nb, blk, H, dh)
    kb, vb = band(residues(k)), band(residues(v))
    s = jnp.einsum('brnqhd,brnkhd->brnqhk', qb, kb).astype(jnp.float32)
    qi = jnp.arange(blk)[:, None]
    ki = jnp.arange(2 * blk)[None, :]
    dist = blk + qi - ki
    key_m = jnp.arange(nb)[:, None, None] * blk - blk + ki[None]
    valid = (dist >= 0)[None] & (dist <= span)[None] & (key_m >= 0)
    s = jnp.where(valid[:, :, None, :], s, -jnp.inf)
    m = jnp.max(s, axis=-1, keepdims=True)
    p = jnp.exp(s - m)
    den = jnp.sum(p, axis=-1, keepdims=True)
    o = jnp.einsum('brnqhk,brnkhd->brnqhd', (p / den).astype(v.dtype), vb)
    lse = (m + jnp.log(den))[..., 0]
    o = o.reshape(B_, dilation, Lp, H, dh)[:, :, :L].transpose(0, 2, 1, 3, 4).reshape(B_, S, H, dh)
    lse = lse.reshape(B_, dilation, Lp, H)[:, :, :L].transpose(0, 2, 1, 3).reshape(B_, S, H)
    return o, lse


def dilated_mixer(h, w_in, w_out):
    B_, S, _ = h.shape
    pos = jnp.arange(S)
    q, k, v, z = jnp.split(h @ w_in, 4, axis=-1)
    q = rotary(q.reshape(B_, S, B_HEADS, B_HEAD_DIM), pos) * (B_HEAD_DIM ** -0.5)
    k = rotary(k.reshape(B_, S, B_HEADS, B_HEAD_DIM), pos)
    v = v.reshape(B_, S, B_HEADS, B_HEAD_DIM)
    outs, lses = [], []
    for window, dilation in B_CONFIGS:
        o_g, lse_g = _dilated_band_attention(q, k, v, window, dilation)
        outs.append(o_g.astype(jnp.float32))
        lses.append(lse_g)
    alpha = jax.nn.softmax(jnp.stack(lses), axis=0)
    o = jnp.einsum('gbsh,gbshd->bshd', alpha, jnp.stack(outs)).astype(h.dtype).reshape(B_, S, D_INNER)
    return (o * jax.nn.silu(z)) @ w_out


def retention_mixer(h, w_in, w_out):
    B_, S, _ = h.shape
    pos = jnp.arange(S)
    q, k, v, z = _split(h @ w_in, C_SPLITS)
    q = rotary(q.reshape(B_, S, C_HEADS, C_QK_DIM), pos).astype(jnp.float32)
    k = (rotary(k.reshape(B_, S, C_HEADS, C_QK_DIM), pos) * (C_QK_DIM ** -0.5)).astype(jnp.float32)
    v = v.reshape(B_, S, C_HEADS, C_V_DIM).astype(jnp.float32)
    log_gamma = jnp.log1p(-jnp.exp2(-5.0 - jnp.arange(C_HEADS, dtype=jnp.float32)))
    n = S // C_CHUNK
    qc = q.reshape(B_, n, C_CHUNK, C_HEADS, C_QK_DIM)
    kc = k.reshape(B_, n, C_CHUNK, C_HEADS, C_QK_DIM)
    vc = v.reshape(B_, n, C_CHUNK, C_HEADS, C_V_DIM)
    idx = jnp.arange(C_CHUNK, dtype=jnp.float32)
    rel = idx[:, None] - idx[None, :]
    decay = jnp.where(rel[None] >= 0, jnp.exp(jnp.maximum(rel, 0.0)[None] * log_gamma[:, None, None]), 0.0)
    scores = jnp.einsum('bnihd,bnjhd->bnhij', qc, kc) * decay
    o_intra = jnp.einsum('bnhij,bnjhe->bnihe', scores, vc)
    k_decay = jnp.exp((C_CHUNK - 1 - idx)[None, :] * log_gamma[:, None])
    kv = jnp.einsum('bnjhd,hj,bnjhe->bnhde', kc, k_decay, vc)
    chunk_decay = jnp.exp(C_CHUNK * log_gamma)[:, None, None]

    def step(state, kv_n):
        return state * chunk_decay + kv_n, state

    _, prev = lax.scan(step, jnp.zeros((B_, C_HEADS, C_QK_DIM, C_V_DIM), jnp.float32), kv.swapaxes(0, 1))
    prev = prev.swapaxes(0, 1)
    q_decay = jnp.exp((idx + 1.0)[None, :] * log_gamma[:, None])
    o_inter = jnp.einsum('bnihd,hi,bnhde->bnihe', qc, q_decay, prev)
    o = head_rmsnorm((o_intra + o_inter).reshape(B_, S, C_HEADS, C_V_DIM))
    o = o.reshape(B_, S, D_INNER).astype(h.dtype)
    return (o * jax.nn.silu(z)) @ w_out


def hgrn2_mixer(h, w_in, w_out, lower_bound):
    B_, S, _ = h.shape
    q, f, i, z = jnp.split(h @ w_in, 4, axis=-1)
    q = jax.nn.silu(q.astype(jnp.float32)) * (D_K_DIM ** -0.5)
    g = lower_bound + (1.0 - lower_bound) * jax.nn.sigmoid(f.astype(jnp.float32))
    log_f = jnp.log(g)
    k = 1.0 - g
    n = S // D_CHUNK

    def chunks(t, d):
        return t.reshape(B_, n, D_CHUNK, D_HEADS, d).swapaxes(0, 1)

    qs, ks, ls = chunks(q, D_K_DIM), chunks(k, D_K_DIM), chunks(log_f, D_K_DIM)
    vs = chunks(i.astype(jnp.float32), D_V_DIM)
    tri = jnp.arange(D_CHUNK)[:, None] >= jnp.arange(D_CHUNK)[None, :]

    def step(state, inp):
        qn, kn, vn, ln = inp
        b = jnp.cumsum(ln, axis=1)
        diff = jnp.where(tri[None, :, :, None, None], b[:, :, None] - b[:, None, :], -jnp.inf)
        a = jnp.einsum('bthk,bshk,btshk->bhts', qn, kn, jnp.exp(diff))
        o = jnp.einsum('bhts,bshv->bthv', a, vn) + jnp.einsum('bthk,bhkv->bthv', qn * jnp.exp(b), state)
        b_last = b[:, -1]
        new_state = state * jnp.exp(b_last)[..., None] + jnp.einsum('bshk,bshv->bhkv', kn * jnp.exp(b_last[:, None] - b), vn)
        return new_state, o

    _, o = lax.scan(step, jnp.zeros((B_, D_HEADS, D_K_DIM, D_V_DIM), jnp.float32), (qs, ks, vs, ls))
    o = head_rmsnorm(o.swapaxes(0, 1).reshape(B_, S, D_HEADS, D_V_DIM))
    o = o.reshape(B_, S, D_INNER).astype(h.dtype)
    return (o * jax.nn.silu(z)) @ w_out


def setup_inputs(seed: int = 0) -> dict:
    key = jax.random.key(seed)
    ks = jax.random.split(key, 13)
    a_cols = sum(A_SPLITS)
    c_cols = sum(C_SPLITS)
    na, nb_, nc, nd = (_n_layers_of(m) for m in range(N_MIXERS))

    def w(k, shape, fan_in):
        return jax.random.normal(k, shape, jnp.float32) * (fan_in ** -0.5)

    return {
        "x": jax.random.normal(ks[0], (BATCH, SEQ, D_MODEL), jnp.float32),
        "norm_g": 1.0 + 0.02 * jax.random.normal(ks[1], (DEPTH, D_MODEL), jnp.float32),
        "final_g": 1.0 + 0.02 * jax.random.normal(ks[2], (D_MODEL,), jnp.float32),
        "a_w_in": w(ks[3], (na, D_MODEL, a_cols), D_MODEL),
        "a_w_out": w(ks[4], (na, D_INNER, D_MODEL), D_INNER),
        "b_w_in": w(ks[5], (nb_, D_MODEL, 4 * D_INNER), D_MODEL),
        "b_w_out": w(ks[6], (nb_, D_INNER, D_MODEL), D_INNER),
        "c_w_in": w(ks[7], (nc, D_MODEL, c_cols), D_MODEL),
        "c_w_out": w(ks[8], (nc, D_INNER, D_MODEL), D_INNER),
        "d_w_in": w(ks[9], (nd, D_MODEL, 4 * D_INNER), D_MODEL),
        "d_w_out": w(ks[10], (nd, D_INNER, D_MODEL), D_INNER),
        "hgrn_lb_logits": 0.5 * jax.random.normal(ks[11], (DEPTH, D_INNER), jnp.float32),
    }


def reference(x, norm_g, final_g, a_w_in, a_w_out, b_w_in, b_w_out, c_w_in, c_w_out, d_w_in, d_w_out, hgrn_lb_logits):
    lb_cum = jnp.cumsum(jax.nn.softmax(hgrn_lb_logits.astype(jnp.float32), axis=0), axis=0)
    lower_bounds = lb_cum - lb_cum[0]
    h = x
    for layer in range(DEPTH):
        kind = layer % N_MIXERS
        slot = layer // N_MIXERS
        u = rmsnorm(h, norm_g[layer])
        if kind == 0:
            y = dsa_mixer(u, a_w_in[slot], a_w_out[slot])
        elif kind == 1:
            y = dilated_mixer(u, b_w_in[slot], b_w_out[slot])
        elif kind == 2:
            y = retention_mixer(u, c_w_in[slot], c_w_out[slot])
        else:
            y = hgrn2_mixer(u, d_w_in[slot], d_w_out[slot], lower_bounds[layer])
        h = h + y
    return rmsnorm(h, final_g)
```

```python
import functools

import jax
import jax.numpy as jnp
import numpy as np
from jax import lax
from jax.experimental import pallas as pl
from jax.experimental.pallas import tpu as pltpu

F32 = jnp.float32
BF16 = jnp.bfloat16

D_MODEL = 1024
D_INNER = 2048
N_MIXERS = 4
NORM_EPS = 1e-6
ROPE_THETA = 10000.0
LANES = 128
NEG_BIG = -1e30

A_HEADS, A_HEAD_DIM = 16, 128
A_IDX_HEADS, A_IDX_DIM = 16, 64
A_TOPK_MAX = 256
B_HEADS, B_HEAD_DIM = 16, 128
B_CONFIGS = ((128, 1), (512, 4), (2048, 16))
C_HEADS, C_QK_DIM, C_V_DIM, C_CHUNK = 8, 128, 256, 128
D_HEADS, D_K_DIM, D_V_DIM = 16, 128, 128
D_CHUNK = 64
D_PAD = 8

VMEM_LIMIT = 56 * 1024 * 1024


def _cparams(sem):
    return pltpu.CompilerParams(dimension_semantics=sem, vmem_limit_bytes=VMEM_LIMIT)


def _dot(a, b):
    return jnp.dot(a, b, preferred_element_type=F32)


def _dot_nt(a, b):
    return lax.dot_general(a, b, (((1,), (1,)), ((), ())), preferred_element_type=F32)


def _dot_tn(a, b):
    return lax.dot_general(a, b, (((0,), (0,)), ((), ())), preferred_element_type=F32)


def _rope_tables(seq, dh):
    freqs = ROPE_THETA ** (-jnp.arange(0, dh, 2, dtype=F32) / dh)
    ang = jnp.arange(seq, dtype=F32)[:, None] * freqs[None, :]
    cos, sin = jnp.cos(ang), jnp.sin(ang)
    cosf = jnp.concatenate([cos, cos], axis=-1)
    sinf = jnp.concatenate([-sin, sin], axis=-1)
    rep = LANES // dh
    return jnp.tile(cosf, (1, rep)), jnp.tile(sinf, (1, rep))


def _epilogue(kind, scale, x, tabs, aux):
    c128, s128, c64, s64 = tabs
    if kind == "rot128":
        y = x * c128 + pltpu.roll(x, 64, 1) * s128
    elif kind == "rot64":
        lane = lax.broadcasted_iota(jnp.int32, x.shape, 1)
        partner = jnp.where((lane % 64) < 32, pltpu.roll(x, 96, 1), pltpu.roll(x, 32, 1))
        y = x * c64 + partner * s64
    elif kind == "silu":
        y = x * jax.nn.sigmoid(x)
    elif kind == "hgrn_g":
        y = aux + (1.0 - aux) * jax.nn.sigmoid(x)
    else:
        y = x
    if scale != 1.0:
        y = y * scale
    return y


def _proj_body(h_ref, g_ref, w_ref, c128_ref, s128_ref, c64_ref, s64_ref, aux_ref, o_ref, u_ref,
               *, patterns, tile_pat):
    j = pl.program_id(1)

    @pl.when(j == 0)
    def _norm():
        x = h_ref[...]
        ms = jnp.mean(x * x, axis=-1, keepdims=True)
        u_ref[...] = (x * lax.rsqrt(ms + NORM_EPS) * g_ref[...]).astype(BF16)

    acc = _dot(u_ref[...], w_ref[...])

    def epi(pat):
        kinds = {k for k, _ in pat}
        tabs = (c128_ref[...] if "rot128" in kinds else None, s128_ref[...] if "rot128" in kinds else None,
                c64_ref[...] if "rot64" in kinds else None, s64_ref[...] if "rot64" in kinds else None)
        for c, (kind, scale) in enumerate(pat):
            sl = slice(c * LANES, (c + 1) * LANES)
            aux = aux_ref[:, sl] if kind == "hgrn_g" else None
            o_ref[:, sl] = _epilogue(kind, scale, acc[:, sl], tabs, aux).astype(o_ref.dtype)

    if len(patterns) == 1:
        epi(patterns[0])
    else:
        for pid, pat in enumerate(patterns):
            tiles = [t for t, p in enumerate(tile_pat) if p == pid]
            cond = functools.reduce(jnp.logical_or, [j == t for t in tiles])
            pl.when(cond)(functools.partial(epi, pat))


def _proj(h, g, w, chunk_kinds, tn, out_dtype, tabs, aux, seq):
    n, d = h.shape
    c = w.shape[1]
    tm = min(1024, seq)
    per_tile = tn // LANES
    tile_kinds = [tuple(chunk_kinds[t * per_tile:(t + 1) * per_tile]) for t in range(c // tn)]
    patterns = tuple(dict.fromkeys(tile_kinds))
    tile_pat = tuple(patterns.index(t) for t in tile_kinds)
    pos_blocks = seq // tm
    tab_spec = pl.BlockSpec((tm, LANES), lambda i, j: (i % pos_blocks, 0))
    return pl.pallas_call(
        functools.partial(_proj_body, patterns=patterns, tile_pat=tile_pat),
        grid=(n // tm, c // tn),
        in_specs=[
            pl.BlockSpec((tm, d), lambda i, j: (i, 0)),
            pl.BlockSpec((1, d), lambda i, j: (0, 0)),
            pl.BlockSpec((d, tn), lambda i, j: (0, j)),
            tab_spec, tab_spec, tab_spec, tab_spec,
            pl.BlockSpec((1, tn), lambda i, j: (0, j)),
        ],
        out_specs=pl.BlockSpec((tm, tn), lambda i, j: (i, j)),
        out_shape=jax.ShapeDtypeStruct((n, c), out_dtype),
        scratch_shapes=[pltpu.VMEM((tm, d), BF16)],
        compiler_params=_cparams(("parallel", "arbitrary")),
        name="norm_in_proj",
    )(h, g.reshape(1, d), w, *tabs, aux)


def _outproj_body(og_ref, w_ref, h_ref, gf_ref, o_ref, *, final):
    y = h_ref[...] + _dot(og_ref[...], w_ref[...])
    if final:
        ms = jnp.mean(y * y, axis=-1, keepdims=True)
        y = y * lax.rsqrt(ms + NORM_EPS) * gf_ref[...]
    o_ref[...] = y


def _outproj(og, w, h, gf, final):
    n, di = og.shape
    d = w.shape[1]
    tm = min(512, n)
    return pl.pallas_call(
        functools.partial(_outproj_body, final=final),
        grid=(n // tm,),
        in_specs=[
            pl.BlockSpec((tm, di), lambda i: (i, 0)),
            pl.BlockSpec((di, d), lambda i: (0, 0)),
            pl.BlockSpec((tm, d), lambda i: (i, 0)),
            pl.BlockSpec((1, d), lambda i: (0, 0)),
        ],
        out_specs=pl.BlockSpec((tm, d), lambda i: (i, 0)),
        out_shape=jax.ShapeDtypeStruct((n, d), F32),
        compiler_params=_cparams(("parallel",)),
        name="out_proj_residual",
    )(og, w, h, gf.reshape(1, d))


def _float_order_key(x):
    bits = pltpu.bitcast(x, jnp.int32)
    return jnp.where(bits < 0, bits ^ jnp.int32(0x7FFFFFFF), bits)


def _dsa_index_body(iq_ref, ik_ref, iw_ref, bias_ref, key_ref, *, topk, tq, seq):
    i = pl.program_id(1)
    int_min = jnp.int32(-2 ** 31)
    ik2 = ik_ref[...]
    lane = lax.broadcasted_iota(jnp.int32, ik2.shape, 1)
    ik_lo = jnp.where(lane < A_IDX_DIM, ik2, jnp.zeros_like(ik2))
    ik_hi = jnp.where(lane >= A_IDX_DIM, ik2, jnp.zeros_like(ik2))
    iw = iw_ref[...].astype(F32)
    score = jnp.zeros((tq, seq), F32)
    for p in range(A_IDX_HEADS // 2):
        iq2 = iq_ref[:, p * LANES:(p + 1) * LANES]
        score += jnp.maximum(_dot_nt(iq2, ik_lo), 0.0) * iw[:, 2 * p:2 * p + 1]
        score += jnp.maximum(_dot_nt(iq2, ik_hi), 0.0) * iw[:, 2 * p + 1:2 * p + 2]
    t_pos = i * tq + lax.broadcasted_iota(jnp.int32, (tq, seq), 0)
    s_pos = lax.broadcasted_iota(jnp.int32, (tq, seq), 1)
    causal = s_pos <= t_pos
    key_ref[...] = jnp.where(causal, _float_order_key(score), int_min)

    def bisect(it, prefix):
        cand = prefix | (jnp.int32(1) << (31 - it))
        cnt = jnp.sum((key_ref[...] >= (cand ^ int_min)).astype(F32), axis=1, keepdims=True)
        return jnp.where(cnt >= topk, cand, prefix)

    prefix = lax.fori_loop(0, 32, bisect, jnp.zeros((tq, 1), jnp.int32))
    sel = (key_ref[...] >= (prefix ^ int_min)) & causal
    bias_ref[...] = jnp.where(sel, 0.0, NEG_BIG).astype(bias_ref.dtype)


def _dsa_index(pa, batch, seq, topk):
    n = pa.shape[0]
    tq = min(256, seq)
    nq = seq // tq
    return pl.pallas_call(
        functools.partial(_dsa_index_body, topk=topk, tq=tq, seq=seq),
        grid=(batch, nq),
        in_specs=[
            pl.BlockSpec((tq, 1024), lambda b, i: (b * nq + i, 4)),
            pl.BlockSpec((seq, LANES), lambda b, i: (b, 42)),
            pl.BlockSpec((tq, LANES), lambda b, i: (b * nq + i, 43)),
        ],
        out_specs=pl.BlockSpec((tq, seq), lambda b, i: (b * nq + i, 0)),
        out_shape=jax.ShapeDtypeStruct((n, seq), BF16),
        scratch_shapes=[pltpu.VMEM((tq, seq), jnp.int32)],
        compiler_params=_cparams(("parallel", "parallel")),
        name="dsa_indexer_topk_mask",
    )(pa, pa, pa)


def _dsa_attn_body(q_ref, z_ref, k_ref, v_ref, bias_ref, o_ref, qc_ref, m_ref, l_ref, acc_ref, *, tq, kb):
    i = pl.program_id(1)
    j = pl.program_id(2)
    nh = A_HEADS

    @pl.when(j == 0)
    def _init():
        for h in range(nh):
            qc_ref[h * tq:(h + 1) * tq, :] = q_ref[:, h * LANES:(h + 1) * LANES]
        m_ref[...] = jnp.full(m_ref.shape, NEG_BIG, F32)
        l_ref[...] = jnp.zeros(l_ref.shape, F32)
        acc_ref[...] = jnp.zeros(acc_ref.shape, F32)

    @pl.when(j * kb <= i * tq + tq - 1)
    def _step():
        s = _dot_nt(qc_ref[...], k_ref[...])
        bias = bias_ref[...].astype(F32)
        s = (s.reshape(nh, tq, kb) + bias[None]).reshape(nh * tq, kb)
        m_old = m_ref[...]
        m_new = jnp.maximum(m_old, jnp.max(s, axis=1, keepdims=True))
        alpha = jnp.exp(m_old - m_new)
        p = jnp.exp(s - m_new)
        l_ref[...] = alpha * l_ref[...] + jnp.sum(p, axis=1, keepdims=True)
        acc_ref[...] = alpha * acc_ref[...] + _dot(p.astype(BF16), v_ref[...])
        m_ref[...] = m_new

    @pl.when(j == pl.num_programs(2) - 1)
    def _fin():
        o = acc_ref[...] / l_ref[...]
        for h in range(nh):
            sl = slice(h * LANES, (h + 1) * LANES)
            o_ref[:, sl] = (o[h * tq:(h + 1) * tq, :] * z_ref[:, sl].astype(F32)).astype(o_ref.dtype)


def _dsa_attn(pa, bias, batch, seq):
    n = pa.shape[0]
    tq = 128
    kb = min(512, seq)
    nq, nk = seq // tq, seq // kb

    def kj(i, j):
        return jnp.minimum(j, (i * tq + tq - 1) // kb)

    return pl.pallas_call(
        functools.partial(_dsa_attn_body, tq=tq, kb=kb),
        grid=(batch, nq, nk),
        in_specs=[
            pl.BlockSpec((tq, D_INNER), lambda b, i, j: (b * nq + i, 0)),
            pl.BlockSpec((tq, D_INNER), lambda b, i, j: (b * nq + i, 1)),
            pl.BlockSpec((kb, LANES), lambda b, i, j: (b * nk + kj(i, j), 40)),
            pl.BlockSpec((kb, LANES), lambda b, i, j: (b * nk + kj(i, j), 41)),
            pl.BlockSpec((tq, kb), lambda b, i, j: (b * nq + i, kj(i, j))),
        ],
        out_specs=pl.BlockSpec((tq, D_INNER), lambda b, i, j: (b * nq + i, 0)),
        out_shape=jax.ShapeDtypeStruct((n, D_INNER), BF16),
        scratch_shapes=[
            pltpu.VMEM((A_HEADS * tq, LANES), BF16),
            pltpu.VMEM((A_HEADS * tq, 1), F32),
            pltpu.VMEM((A_HEADS * tq, 1), F32),
            pltpu.VMEM((A_HEADS * tq, LANES), F32),
        ],
        compiler_params=_cparams(("parallel", "parallel", "arbitrary")),
        name="dsa_masked_attention",
    )(pa, pa, pa, pa, bias)


def _mixer_a(h, g, w_in, w_out, batch, seq, tabs, zero_aux):
    hd = A_HEADS * A_HEAD_DIM
    hi = A_IDX_HEADS * A_IDX_DIM
    o0 = np.cumsum([0, hd, A_HEAD_DIM, A_HEAD_DIM, hi, A_IDX_DIM, A_IDX_HEADS, D_INNER])
    wq, wk, wv, wiq, wik, wiw, wz = (w_in[:, o0[t]:o0[t + 1]] for t in range(7))
    pad = jnp.zeros((D_MODEL, LANES - A_IDX_HEADS), F32)
    w = jnp.concatenate([wq, wz, wiq, wk, wv, wik, wik, wiw, pad], axis=1).astype(BF16)
    kinds = ([("rot128", A_HEAD_DIM ** -0.5)] * 16 + [("silu", 1.0)] * 16 + [("rot64", 1.0)] * 8
             + [("rot128", 1.0), ("none", 1.0), ("rot64", 1.0), ("none", A_IDX_HEADS ** -0.5 * A_IDX_DIM ** -0.5)])
    pa = _proj(h, g, w, kinds, 1408, BF16, tabs, jnp.zeros((1, w.shape[1]), F32), seq)
    bias = _dsa_index(pa, batch, seq, min(A_TOPK_MAX, seq // 4))
    return _dsa_attn(pa, bias, batch, seq)


def _dilated_log_multiplicity(seq, blk):
    nd = seq // blk
    d = (jnp.arange(nd)[:, None, None] * blk + jnp.arange(blk)[None, :, None] - jnp.arange(blk)[None, None, :])
    mult = jnp.zeros(d.shape, F32)
    for window, dil in B_CONFIGS:
        mult += ((d >= 0) & (d % dil == 0) & (d <= window)).astype(F32)
    return jnp.where(mult > 0, jnp.log(jnp.maximum(mult, 1.0)), NEG_BIG)


def _dilated_body(q_ref, k_ref, v_ref, z_ref, tab_ref, o_ref, m_ref, l_ref, acc_ref):
    i = pl.program_id(2)
    j = pl.program_id(3)

    @pl.when(j == 0)
    def _init():
        m_ref[...] = jnp.full(m_ref.shape, NEG_BIG, F32)
        l_ref[...] = jnp.zeros(l_ref.shape, F32)
        acc_ref[...] = jnp.zeros(acc_ref.shape, F32)

    @pl.when(j <= i)
    def _step():
        s = _dot_nt(q_ref[...], k_ref[...]) + tab_ref[i - j]
        m_old = m_ref[...]
        m_new = jnp.maximum(m_old, jnp.max(s, axis=1, keepdims=True))
        alpha = jnp.exp(m_old - m_new)
        p = jnp.exp(s - m_new)
        l_ref[...] = alpha * l_ref[...] + jnp.sum(p, axis=1, keepdims=True)
        acc_ref[...] = alpha * acc_ref[...] + _dot(p.astype(BF16), v_ref[...])
        m_ref[...] = m_new

    @pl.when(j == pl.num_programs(3) - 1)
    def _fin():
        o_ref[...] = (acc_ref[...] / l_ref[...] * z_ref[...].astype(F32)).astype(o_ref.dtype)


def _mixer_b(h, g, w_in, w_out, batch, seq, tabs, zero_aux):
    kinds = ([("rot128", B_HEAD_DIM ** -0.5)] * 16 + [("rot128", 1.0)] * 16 + [("none", 1.0)] * 16
             + [("silu", 1.0)] * 16)
    pb = _proj(h, g, w_in.astype(BF16), kinds, 1024, BF16, tabs, jnp.zeros((1, 4 * D_INNER), F32), seq)
    n = pb.shape[0]
    blk = min(512, seq)
    nb = seq // blk
    tab = _dilated_log_multiplicity(seq, blk)
    nh = B_HEADS
    return pl.pallas_call(
        _dilated_body,
        grid=(batch, nh, nb, nb),
        in_specs=[
            pl.BlockSpec((blk, LANES), lambda b, h, i, j: (b * nb + i, h)),
            pl.BlockSpec((blk, LANES), lambda b, h, i, j: (b * nb + jnp.minimum(i, j), nh + h)),
            pl.BlockSpec((blk, LANES), lambda b, h, i, j: (b * nb + jnp.minimum(i, j), 2 * nh + h)),
            pl.BlockSpec((blk, LANES), lambda b, h, i, j: (b * nb + i, 3 * nh + h)),
            pl.BlockSpec((nb, blk, blk), lambda b, h, i, j: (0, 0, 0)),
        ],
        out_specs=pl.BlockSpec((blk, LANES), lambda b, h, i, j: (b * nb + i, h)),
        out_shape=jax.ShapeDtypeStruct((n, D_INNER), BF16),
        scratch_shapes=[pltpu.VMEM((blk, 1), F32), pltpu.VMEM((blk, 1), F32), pltpu.VMEM((blk, LANES), F32)],
        compiler_params=_cparams(("parallel", "parallel", "parallel", "arbitrary")),
        name="dilated_attention",
    )(pb, pb, pb, pb, tab)


def _retention_body(lg_ref, q_ref, k_ref, v_ref, z_ref, o_ref, st_ref, *, seq):
    c = C_CHUNK
    lg = lg_ref[pl.program_id(1)]
    ri = lax.broadcasted_iota(jnp.int32, (c, c), 0)
    ci = lax.broadcasted_iota(jnp.int32, (c, c), 1)
    rel = (ri - ci).astype(F32)
    decay = jnp.where(rel >= 0, jnp.exp(jnp.maximum(rel, 0.0) * lg), 0.0)
    idx = lax.broadcasted_iota(jnp.int32, (c, 1), 0).astype(F32)
    k_decay = jnp.exp((c - 1 - idx) * lg)
    q_decay = jnp.exp((idx + 1.0) * lg)
    chunk_decay = jnp.exp(jnp.full((1, C_QK_DIM), c, F32) * lg)
    st_ref[...] = jnp.zeros(st_ref.shape, F32)

    def chunk(nidx, carry):
        r0 = pl.multiple_of(nidx * c, c)
        q = q_ref[pl.ds(r0, c), :]
        k = k_ref[pl.ds(r0, c), :]
        v = v_ref[pl.ds(r0, c), :]
        scores = _dot_nt(q, k) * decay
        o = _dot(scores.astype(BF16), v)
        st = st_ref[...]
        o += _dot_nt((q.astype(F32) * q_decay).astype(BF16), st.astype(BF16))
        st_ref[...] = st * chunk_decay + _dot_tn(v, (k.astype(F32) * k_decay).astype(BF16))
        o = o * lax.rsqrt(jnp.mean(o * o, axis=-1, keepdims=True) + NORM_EPS)
        o_ref[pl.ds(r0, c), :] = (o * z_ref[pl.ds(r0, c), :].astype(F32)).astype(o_ref.dtype)
        return carry

    lax.fori_loop(0, seq // c, chunk, 0)


def _mixer_c(h, g, w_in, w_out, batch, seq, tabs, zero_aux):
    kinds = ([("rot128", 1.0)] * 8 + [("rot128", C_QK_DIM ** -0.5)] * 8 + [("none", 1.0)] * 16
             + [("silu", 1.0)] * 16)
    pc = _proj(h, g, w_in.astype(BF16), kinds, 1024, BF16, tabs, jnp.zeros((1, w_in.shape[1]), F32), seq)
    n = pc.shape[0]
    nh = C_HEADS
    log_gamma = jnp.log1p(-jnp.exp2(-5.0 - jnp.arange(nh, dtype=F32)))
    return pl.pallas_call(
        functools.partial(_retention_body, seq=seq),
        grid=(batch, nh),
        in_specs=[
            pl.BlockSpec(memory_space=pltpu.SMEM),
            pl.BlockSpec((seq, C_QK_DIM), lambda b, h: (b, h)),
            pl.BlockSpec((seq, C_QK_DIM), lambda b, h: (b, nh + h)),
            pl.BlockSpec((seq, C_V_DIM), lambda b, h: (b, nh + h)),
            pl.BlockSpec((seq, C_V_DIM), lambda b, h: (b, 2 * nh + h)),
        ],
        out_specs=pl.BlockSpec((seq, C_V_DIM), lambda b, h: (b, h)),
        out_shape=jax.ShapeDtypeStruct((n, D_INNER), BF16),
        scratch_shapes=[pltpu.VMEM((C_V_DIM, C_QK_DIM), F32)],
        compiler_params=_cparams(("parallel", "parallel")),
        name="retention",
    )(log_gamma, pc, pc, pc, pc)


def _hgrn_body(q_ref, g_ref, v_ref, z_ref, o_ref, st_ref, kpad_ref, bpad_ref, vpad_ref, *, seq):
    c = D_CHUNK
    nblk = c // 8
    ri = lax.broadcasted_iota(jnp.int32, (c, c), 0)
    ci = lax.broadcasted_iota(jnp.int32, (c, c), 1)
    tri = (ri >= ci).astype(BF16)
    sub = lax.broadcasted_iota(jnp.int32, (8, LANES), 0)
    st_ref[...] = jnp.zeros(st_ref.shape, F32)
    zpad = jnp.zeros((D_PAD, LANES), F32)
    kpad_ref[0:D_PAD, :] = zpad
    bpad_ref[0:D_PAD, :] = zpad
    vpad_ref[0:D_PAD, :] = zpad

    def chunk(nidx, carry):
        r0 = pl.multiple_of(nidx * c, c)
        gate = g_ref[pl.ds(r0, c), :]
        lf = jnp.log(gate)
        hi = lf.astype(BF16)
        r1 = lf - hi.astype(F32)
        mid = r1.astype(BF16)
        lo = (r1 - mid.astype(F32)).astype(BF16)
        b = _dot(tri, hi) + _dot(tri, mid) + _dot(tri, lo)
        q = q_ref[pl.ds(r0, c), :].astype(F32)
        v = v_ref[pl.ds(r0, c), :]
        st = st_ref[...]
        o_inter = _dot_nt((q * jnp.exp(b)).astype(BF16), st.astype(BF16))
        kpad_ref[D_PAD:D_PAD + c, :] = 1.0 - gate
        bpad_ref[D_PAD:D_PAD + c, :] = b
        vpad_ref[D_PAD:D_PAD + c, :] = v.astype(F32)
        o_blocks = [o_inter[8 * tb:8 * tb + 8, :] for tb in range(nblk)]
        for d in range(c):
            for tb in range(d // 8, nblk):
                src = pl.ds(D_PAD + 8 * tb - d, 8)
                x = q[8 * tb:8 * tb + 8, :] * kpad_ref[src, :] * jnp.exp(b[8 * tb:8 * tb + 8, :] - bpad_ref[src, :])
                if tb == d // 8 and d % 8:
                    x = jnp.where(sub >= d % 8, x, 0.0)
                a = jnp.sum(x, axis=1, keepdims=True)
                o_blocks[tb] = o_blocks[tb] + a * vpad_ref[src, :]
        o = jnp.concatenate(o_blocks, axis=0)
        b_last = b[c - 1:c, :]
        k_dec = ((1.0 - gate) * jnp.exp(b_last - b)).astype(BF16)
        st_ref[...] = st * jnp.exp(b_last) + _dot_tn(v, k_dec)
        o = o * lax.rsqrt(jnp.mean(o * o, axis=-1, keepdims=True) + NORM_EPS)
        o_ref[pl.ds(r0, c), :] = (o * z_ref[pl.ds(r0, c), :].astype(F32)).astype(o_ref.dtype)
        return carry

    lax.fori_loop(0, seq // c, chunk, 0)


def _mixer_d(h, g, w_in, w_out, lower_bound, batch, seq, tabs, zero_aux):
    wq, wf, wi, wz = (w_in[:, t * D_INNER:(t + 1) * D_INNER] for t in range(4))
    w1 = jnp.concatenate([wq, wi, wz], axis=1).astype(BF16)
    kinds = [("silu", D_K_DIM ** -0.5)] * 16 + [("none", 1.0)] * 16 + [("silu", 1.0)] * 16
    pd = _proj(h, g, w1, kinds, 1024, BF16, tabs, jnp.zeros((1, 3 * D_INNER), F32), seq)
    gates = _proj(h, g, wf.astype(BF16), [("hgrn_g", 1.0)] * 16, 1024, F32, tabs,
                  lower_bound.reshape(1, D_INNER).astype(F32), seq)
    n = pd.shape[0]
    nh = D_HEADS
    return pl.pallas_call(
        functools.partial(_hgrn_body, seq=seq),
        grid=(batch, nh),
        in_specs=[
            pl.BlockSpec((seq, LANES), lambda b, h: (b, h)),
            pl.BlockSpec((seq, LANES), lambda b, h: (b, h)),
            pl.BlockSpec((seq, LANES), lambda b, h: (b, nh + h)),
            pl.BlockSpec((seq, LANES), lambda b, h: (b, 2 * nh + h)),
        ],
        out_specs=pl.BlockSpec((seq, LANES), lambda b, h: (b, h)),
        out_shape=jax.ShapeDtypeStruct((n, D_INNER), BF16),
        scratch_shapes=[
            pltpu.VMEM((D_V_DIM, D_K_DIM), F32),
            pltpu.VMEM((D_PAD + D_CHUNK, LANES), F32),
            pltpu.VMEM((D_PAD + D_CHUNK, LANES), F32),
            pltpu.VMEM((D_PAD + D_CHUNK, LANES), F32),
        ],
        compiler_params=_cparams(("parallel", "parallel")),
        name="hgrn2_recurrence",
    )(pd, gates, pd, pd)


def kernel(x, norm_g, final_g, a_w_in, a_w_out, b_w_in, b_w_out, c_w_in, c_w_out, d_w_in, d_w_out, hgrn_lb_logits):
    batch, seq, d = x.shape
    depth = norm_g.shape[0]
    lb_cum = jnp.cumsum(jax.nn.softmax(hgrn_lb_logits.astype(F32), axis=0), axis=0)
    lower_bounds = lb_cum - lb_cum[0]
    tabs = _rope_tables(seq, 128) + _rope_tables(seq, 64)
    h = x.reshape(batch * seq, d)
    for layer in range(depth):
        kind, slot = layer % N_MIXERS, layer // N_MIXERS
        g = norm_g[layer]
        if kind == 0:
            og, w_out = _mixer_a(h, g, a_w_in[slot], a_w_out[slot], batch, seq, tabs, None), a_w_out[slot]
        elif kind == 1:
            og, w_out = _mixer_b(h, g, b_w_in[slot], b_w_out[slot], batch, seq, tabs, None), b_w_out[slot]
        elif kind == 2:
            og, w_out = _mixer_c(h, g, c_w_in[slot], c_w_out[slot], batch, seq, tabs, None), c_w_out[slot]
        else:
            og, w_out = _mixer_d(h, g, d_w_in[slot], d_w_out[slot], lower_bounds[layer], batch, seq, tabs, None), d_w_out[slot]
        h = _outproj(og, w_out.astype(BF16), h, final_g, final=(layer == depth - 1))
    return h.reshape(batch, seq, d)
```

```python
import functools

import jax
import jax.numpy as jnp
import numpy as np
from jax import lax
from jax.experimental import pallas as pl
from jax.experimental.pallas import tpu as pltpu

F32 = jnp.float32
BF16 = jnp.bfloat16

D_MODEL = 1024
D_INNER = 2048
N_MIXERS = 4
NORM_EPS = 1e-6
ROPE_THETA = 10000.0
LANES = 128
NEG_BIG = -1e30

A_HEADS, A_HEAD_DIM = 16, 128
A_IDX_HEADS, A_IDX_DIM = 16, 64
A_TOPK_MAX = 256
A_IDX_BLOCK = 256
B_HEADS, B_HEAD_DIM = 16, 128
B_CONFIGS = ((128, 1), (512, 4), (2048, 16))
C_HEADS, C_QK_DIM, C_V_DIM, C_CHUNK = 8, 128, 256, 128
D_HEADS, D_K_DIM, D_V_DIM = 16, 128, 128
D_CHUNK = 64
D_SUB = 8
D_HEAD_GROUP = 2
D_UNROLL = 2

VMEM_LIMIT = 56 * 1024 * 1024


def _cparams(sem):
    return pltpu.CompilerParams(dimension_semantics=sem, vmem_limit_bytes=VMEM_LIMIT)


def _dot(a, b):
    return jnp.dot(a, b, preferred_element_type=F32)


def _dot_nt(a, b):
    return lax.dot_general(a, b, (((1,), (1,)), ((), ())), preferred_element_type=F32)


def _dot_tn(a, b):
    return lax.dot_general(a, b, (((0,), (0,)), ((), ())), preferred_element_type=F32)


def _rope_tables(seq, dh):
    freqs = ROPE_THETA ** (-jnp.arange(0, dh, 2, dtype=F32) / dh)
    ang = jnp.arange(seq, dtype=F32)[:, None] * freqs[None, :]
    cos, sin = jnp.cos(ang), jnp.sin(ang)
    cosf = jnp.concatenate([cos, cos], axis=-1)
    sinf = jnp.concatenate([-sin, sin], axis=-1)
    rep = LANES // dh
    return jnp.tile(cosf, (1, rep)), jnp.tile(sinf, (1, rep))


def _epilogue(kind, scale, x, tabs, aux):
    c128, s128, c64, s64 = tabs
    if kind == "rot128":
        y = x * c128 + pltpu.roll(x, 64, 1) * s128
    elif kind == "rot64":
        lane = lax.broadcasted_iota(jnp.int32, x.shape, 1)
        partner = jnp.where((lane % 64) < 32, pltpu.roll(x, 96, 1), pltpu.roll(x, 32, 1))
        y = x * c64 + partner * s64
    elif kind == "silu":
        y = x * jax.nn.sigmoid(x)
    elif kind == "hgrn_g":
        y = aux + (1.0 - aux) * jax.nn.sigmoid(x)
    else:
        y = x
    if scale != 1.0:
        y = y * scale
    return y


def _proj_body(h_ref, g_ref, w_ref, c128_ref, s128_ref, c64_ref, s64_ref, aux_ref, o_ref, u_ref,
               *, patterns, tile_pat):
    j = pl.program_id(1)

    @pl.when(j == 0)
    def _norm():
        x = h_ref[...]
        ms = jnp.mean(x * x, axis=-1, keepdims=True)
        u_ref[...] = (x * lax.rsqrt(ms + NORM_EPS) * g_ref[...]).astype(BF16)

    acc = _dot(u_ref[...], w_ref[...])

    def epi(pat):
        kinds = {k for k, _ in pat}
        tabs = (c128_ref[...] if "rot128" in kinds else None, s128_ref[...] if "rot128" in kinds else None,
                c64_ref[...] if "rot64" in kinds else None, s64_ref[...] if "rot64" in kinds else None)
        for c, (kind, scale) in enumerate(pat):
            sl = slice(c * LANES, (c + 1) * LANES)
            aux = aux_ref[:, sl] if kind == "hgrn_g" else None
            o_ref[:, sl] = _epilogue(kind, scale, acc[:, sl], tabs, aux).astype(o_ref.dtype)

    if len(patterns) == 1:
        epi(patterns[0])
    else:
        for pid, pat in enumerate(patterns):
            tiles = [t for t, p in enumerate(tile_pat) if p == pid]
            cond = functools.reduce(jnp.logical_or, [j == t for t in tiles])
            pl.when(cond)(functools.partial(epi, pat))


def _proj(h, g, w, chunk_kinds, tn, out_dtype, tabs, aux, seq):
    n, d = h.shape
    c = w.shape[1]
    tm = min(1024, seq)
    per_tile = tn // LANES
    tile_kinds = [tuple(chunk_kinds[t * per_tile:(t + 1) * per_tile]) for t in range(c // tn)]
    patterns = tuple(dict.fromkeys(tile_kinds))
    tile_pat = tuple(patterns.index(t) for t in tile_kinds)
    pos_blocks = seq // tm
    tab_spec = pl.BlockSpec((tm, LANES), lambda i, j: (i % pos_blocks, 0))
    return pl.pallas_call(
        functools.partial(_proj_body, patterns=patterns, tile_pat=tile_pat),
        grid=(n // tm, c // tn),
        in_specs=[
            pl.BlockSpec((tm, d), lambda i, j: (i, 0)),
            pl.BlockSpec((1, d), lambda i, j: (0, 0)),
            pl.BlockSpec((d, tn), lambda i, j: (0, j)),
            tab_spec, tab_spec, tab_spec, tab_spec,
            pl.BlockSpec((1, tn), lambda i, j: (0, j)),
        ],
        out_specs=pl.BlockSpec((tm, tn), lambda i, j: (i, j)),
        out_shape=jax.ShapeDtypeStruct((n, c), out_dtype),
        scratch_shapes=[pltpu.VMEM((tm, d), BF16)],
        compiler_params=_cparams(("parallel", "arbitrary")),
        name="norm_in_proj",
    )(h, g.reshape(1, d), w, *tabs, aux)


def _outproj_body(og_ref, w_ref, h_ref, gf_ref, o_ref, *, final):
    y = h_ref[...] + _dot(og_ref[...], w_ref[...])
    if final:
        ms = jnp.mean(y * y, axis=-1, keepdims=True)
        y = y * lax.rsqrt(ms + NORM_EPS) * gf_ref[...]
    o_ref[...] = y


def _outproj(og, w, h, gf, final):
    n, di = og.shape
    d = w.shape[1]
    tm = min(512, n)
    return pl.pallas_call(
        functools.partial(_outproj_body, final=final),
        grid=(n // tm,),
        in_specs=[
            pl.BlockSpec((tm, di), lambda i: (i, 0)),
            pl.BlockSpec((di, d), lambda i: (0, 0)),
            pl.BlockSpec((tm, d), lambda i: (i, 0)),
            pl.BlockSpec((1, d), lambda i: (0, 0)),
        ],
        out_specs=pl.BlockSpec((tm, d), lambda i: (i, 0)),
        out_shape=jax.ShapeDtypeStruct((n, d), F32),
        compiler_params=_cparams(("parallel",)),
        name="out_proj_residual",
    )(og, w, h, gf.reshape(1, d))


def _float_order_key(x):
    bits = pltpu.bitcast(x, jnp.int32)
    return jnp.where(bits < 0, bits ^ jnp.int32(0x7FFFFFFF), bits)


def _dsa_index_body(iq_ref, ik_ref, iw_ref, bias_ref, key_ref, *, topk, tq, seq):
    i = pl.program_id(1)
    nkc = seq // tq
    int_min = jnp.int32(-2 ** 31)
    lane = lax.broadcasted_iota(jnp.int32, (tq, LANES), 1)
    iw = iw_ref[...].astype(F32)
    t_pos = i * tq + lax.broadcasted_iota(jnp.int32, (tq, tq), 0)
    s_loc = lax.broadcasted_iota(jnp.int32, (tq, tq), 1)

    def score_chunk(c, carry):
        ik2 = ik_ref[pl.ds(pl.multiple_of(c * tq, tq), tq), :]
        ik_lo = jnp.where(lane < A_IDX_DIM, ik2, jnp.zeros_like(ik2))
        ik_hi = jnp.where(lane >= A_IDX_DIM, ik2, jnp.zeros_like(ik2))
        score = jnp.zeros((tq, tq), F32)
        for p in range(A_IDX_HEADS // 2):
            iq2 = iq_ref[:, p * LANES:(p + 1) * LANES]
            score += jnp.maximum(_dot_nt(iq2, ik_lo), 0.0) * iw[:, 2 * p:2 * p + 1]
            score += jnp.maximum(_dot_nt(iq2, ik_hi), 0.0) * iw[:, 2 * p + 1:2 * p + 2]
        key_ref[c] = jnp.where(c * tq + s_loc <= t_pos, _float_order_key(score), int_min)
        return carry

    lax.fori_loop(0, i + 1, score_chunk, 0)

    def bisect(it, prefix):
        cand = prefix | (jnp.int32(1) << (31 - it))
        thr = cand ^ int_min

        def count(c, cnt):
            return cnt + jnp.sum((key_ref[c] >= thr).astype(F32), axis=1, keepdims=True)

        cnt = lax.fori_loop(0, i + 1, count, jnp.zeros((tq, 1), F32))
        return jnp.where(cnt >= topk, cand, prefix)

    n_iter = jnp.where((i + 1) * tq <= topk, 0, 32)
    prefix = lax.fori_loop(0, n_iter, bisect, jnp.zeros((tq, 1), jnp.int32))
    thr = prefix ^ int_min

    def emit(c, carry):
        sel = (key_ref[c] >= thr) & (c * tq + s_loc <= t_pos)
        bias_ref[c] = jnp.where(sel, 0.0, NEG_BIG).astype(bias_ref.dtype)
        return carry

    lax.fori_loop(0, i + 1, emit, 0)

    def emit_masked(c, carry):
        bias_ref[c] = jnp.full((tq, tq), NEG_BIG, bias_ref.dtype)
        return carry

    lax.fori_loop(i + 1, nkc, emit_masked, 0)


def _dsa_index(pa, batch, seq, topk):
    tq = min(A_IDX_BLOCK, seq)
    nq = seq // tq
    return pl.pallas_call(
        functools.partial(_dsa_index_body, topk=topk, tq=tq, seq=seq),
        grid=(batch, nq),
        in_specs=[
            pl.BlockSpec((tq, 1024), lambda b, i: (b * nq + i, 4)),
            pl.BlockSpec((seq, LANES), lambda b, i: (b, 42)),
            pl.BlockSpec((tq, LANES), lambda b, i: (b * nq + i, 43)),
        ],
        out_specs=pl.BlockSpec((None, nq, tq, tq), lambda b, i: (b, 0, i, 0)),
        out_shape=jax.ShapeDtypeStruct((batch, nq, seq, tq), BF16),
        scratch_shapes=[pltpu.VMEM((nq, tq, tq), jnp.int32)],
        compiler_params=_cparams(("parallel", "parallel")),
        name="dsa_indexer_topk_mask",
    )(pa, pa, pa)


def _dsa_attn_body(q_ref, z_ref, k_ref, v_ref, bias_ref, o_ref, qc_ref, m_ref, l_ref, acc_ref, *, tq, kb):
    i = pl.program_id(1)
    j = pl.program_id(2)
    nh = A_HEADS

    @pl.when(j == 0)
    def _init():
        for h in range(nh):
            qc_ref[h * tq:(h + 1) * tq, :] = q_ref[:, h * LANES:(h + 1) * LANES]
        m_ref[...] = jnp.full(m_ref.shape, NEG_BIG, F32)
        l_ref[...] = jnp.zeros(l_ref.shape, F32)
        acc_ref[...] = jnp.zeros(acc_ref.shape, F32)

    @pl.when(j * kb <= i * tq + tq - 1)
    def _step():
        s = _dot_nt(qc_ref[...], k_ref[...])
        bias = jnp.concatenate([bias_ref[c] for c in range(bias_ref.shape[0])], axis=1).astype(F32)
        s = (s.reshape(nh, tq, kb) + bias[None]).reshape(nh * tq, kb)
        m_old = m_ref[...]
        m_new = jnp.maximum(m_old, jnp.max(s, axis=1, keepdims=True))
        alpha = jnp.exp(m_old - m_new)
        p = jnp.exp(s - m_new)
        l_ref[...] = alpha * l_ref[...] + jnp.sum(p, axis=1, keepdims=True)
        acc_ref[...] = alpha * acc_ref[...] + _dot(p.astype(BF16), v_ref[...])
        m_ref[...] = m_new

    @pl.when(j == pl.num_programs(2) - 1)
    def _fin():
        o = acc_ref[...] / l_ref[...]
        for h in range(nh):
            sl = slice(h * LANES, (h + 1) * LANES)
            o_ref[:, sl] = (o[h * tq:(h + 1) * tq, :] * z_ref[:, sl].astype(F32)).astype(o_ref.dtype)


def _dsa_attn(pa, bias, batch, seq):
    n = pa.shape[0]
    tq = 128
    kb = min(512, seq)
    kc = bias.shape[-1]
    nq, nk = seq // tq, seq // kb

    def kj(i, j):
        return jnp.minimum(j, (i * tq + tq - 1) // kb)

    return pl.pallas_call(
        functools.partial(_dsa_attn_body, tq=tq, kb=kb),
        grid=(batch, nq, nk),
        in_specs=[
            pl.BlockSpec((tq, D_INNER), lambda b, i, j: (b * nq + i, 0)),
            pl.BlockSpec((tq, D_INNER), lambda b, i, j: (b * nq + i, 1)),
            pl.BlockSpec((kb, LANES), lambda b, i, j: (b * nk + kj(i, j), 40)),
            pl.BlockSpec((kb, LANES), lambda b, i, j: (b * nk + kj(i, j), 41)),
            pl.BlockSpec((None, kb // kc, tq, kc), lambda b, i, j: (b, kj(i, j), i, 0)),
        ],
        out_specs=pl.BlockSpec((tq, D_INNER), lambda b, i, j: (b * nq + i, 0)),
        out_shape=jax.ShapeDtypeStruct((n, D_INNER), BF16),
        scratch_shapes=[
            pltpu.VMEM((A_HEADS * tq, LANES), BF16),
            pltpu.VMEM((A_HEADS * tq, 1), F32),
            pltpu.VMEM((A_HEADS * tq, 1), F32),
            pltpu.VMEM((A_HEADS * tq, LANES), F32),
        ],
        compiler_params=_cparams(("parallel", "parallel", "arbitrary")),
        name="dsa_masked_attention",
    )(pa, pa, pa, pa, bias)


def _mixer_a(h, g, w_in, w_out, batch, seq, tabs, zero_aux):
    hd = A_HEADS * A_HEAD_DIM
    hi = A_IDX_HEADS * A_IDX_DIM
    o0 = np.cumsum([0, hd, A_HEAD_DIM, A_HEAD_DIM, hi, A_IDX_DIM, A_IDX_HEADS, D_INNER])
    wq, wk, wv, wiq, wik, wiw, wz = (w_in[:, o0[t]:o0[t + 1]] for t in range(7))
    pad = jnp.zeros((D_MODEL, LANES - A_IDX_HEADS), F32)
    w = jnp.concatenate([wq, wz, wiq, wk, wv, wik, wik, wiw, pad], axis=1).astype(BF16)
    kinds = ([("rot128", A_HEAD_DIM ** -0.5)] * 16 + [("silu", 1.0)] * 16 + [("rot64", 1.0)] * 8
             + [("rot128", 1.0), ("none", 1.0), ("rot64", 1.0), ("none", A_IDX_HEADS ** -0.5 * A_IDX_DIM ** -0.5)])
    pa = _proj(h, g, w, kinds, 1408, BF16, tabs, jnp.zeros((1, w.shape[1]), F32), seq)
    bias = _dsa_index(pa, batch, seq, min(A_TOPK_MAX, seq // 4))
    return _dsa_attn(pa, bias, batch, seq)


def _band_body(*refs, blk, band_prev, cfg_prev, final):
    refs = list(refs)
    q_ref, kc_ref, vc_ref = refs[:3]
    del refs[:3]
    if band_prev:
        kp_ref, vp_ref = refs[:2]
        del refs[:2]
    if cfg_prev:
        op_ref, lp_ref = refs[:2]
        del refs[:2]
    if final:
        z_ref, o_ref = refs
    else:
        o_ref, l_ref = refs
    ki = lax.broadcasted_iota(jnp.int32, (blk, blk), 0)
    qi = lax.broadcasted_iota(jnp.int32, (blk, blk), 1)
    mask_cur = ki <= qi
    eye = ki == qi
    if band_prev:
        mask_prev = (ki >= qi) & (pl.program_id(2) > 0)
    if cfg_prev:
        lse_prev_t = _pad_rows(lp_ref[...].T, LANES)
    heads = [slice(h * LANES, (h + 1) * LANES) for h in range(B_HEADS)]
    s_c = [jnp.where(mask_cur, _dot_nt(kc_ref[:, sl], q_ref[:, sl]), NEG_BIG) for sl in heads]
    if band_prev:
        s_p = [jnp.where(mask_prev, _dot_nt(kp_ref[:, sl], q_ref[:, sl]), NEG_BIG) for sl in heads]
    lse_rows, p_c, p_p, d_old = [], [], [], []
    for h in range(B_HEADS):
        m = jnp.max(s_c[h], axis=0, keepdims=True)
        if band_prev:
            m = jnp.maximum(m, jnp.max(s_p[h], axis=0, keepdims=True))
        e_c = jnp.exp(s_c[h] - m)
        den = jnp.sum(e_c, axis=0, keepdims=True)
        if band_prev:
            e_p = jnp.exp(s_p[h] - m)
            den += jnp.sum(e_p, axis=0, keepdims=True)
        lse = m + jnp.log(den)
        scale = 1.0 / den
        if cfg_prev:
            lse_prev = lse_prev_t[h:h + 1, :]
            m2 = jnp.maximum(lse, lse_prev)
            w_new, w_old = jnp.exp(lse - m2), jnp.exp(lse_prev - m2)
            scale = scale * w_new / (w_new + w_old)
            d_old.append(jnp.where(eye, w_old / (w_new + w_old), 0.0).astype(BF16))
            lse = m2 + jnp.log(w_new + w_old)
        p_c.append((e_c * scale).astype(BF16))
        if band_prev:
            p_p.append((e_p * scale).astype(BF16))
        lse_rows.append(lse)
    for h, sl in enumerate(heads):
        o = _dot_tn(p_c[h], vc_ref[:, sl])
        if band_prev:
            o += _dot_tn(p_p[h], vp_ref[:, sl])
        if cfg_prev:
            o += _dot(d_old[h], op_ref[:, sl])
        if final:
            o_ref[:, sl] = (o * z_ref[:, sl].astype(F32)).astype(o_ref.dtype)
        else:
            o_ref[:, sl] = o.astype(o_ref.dtype)
    if not final:
        lse_t = _pad_rows(jnp.concatenate(lse_rows, axis=0), LANES)
        l_ref[...] = lse_t.T


def _pad_rows(x, rows):
    if x.shape[0] == rows:
        return x
    return jnp.concatenate([x, jnp.zeros((rows - x.shape[0], x.shape[1]), x.dtype)], axis=0)


def _band_attn(pb, batch, seq, dil, prev, final):
    n = pb.shape[0]
    length = seq // dil
    blk = min(128, length)
    nbl = length // blk
    band_prev = nbl > 1
    width = 4 * D_INNER
    pbv = pb.reshape(n // dil, dil * width)
    cpb = width // D_INNER

    def spec(col, prev_blk=False):
        if prev_blk:
            return pl.BlockSpec((blk, D_INNER), lambda b, r, i: (b * nbl + jnp.maximum(i - 1, 0), r * cpb + col))
        return pl.BlockSpec((blk, D_INNER), lambda b, r, i: (b * nbl + i, r * cpb + col))

    o_spec = pl.BlockSpec((blk, D_INNER), lambda b, r, i: (b * nbl + i, r))
    l_spec = pl.BlockSpec((blk, LANES), lambda b, r, i: (b * nbl + i, r))
    in_specs, args = [spec(0), spec(1), spec(2)], [pbv, pbv, pbv]
    if band_prev:
        in_specs += [spec(1, True), spec(2, True)]
        args += [pbv, pbv]
    if prev is not None:
        in_specs += [o_spec, l_spec]
        args += [prev[0].reshape(n // dil, dil * D_INNER), prev[1].reshape(n // dil, dil * LANES)]
    o_shape = jax.ShapeDtypeStruct((n // dil, dil * D_INNER), BF16)
    if final:
        in_specs.append(spec(3))
        args.append(pbv)
        out_specs, out_shape = o_spec, o_shape
    else:
        out_specs = [o_spec, l_spec]
        out_shape = [o_shape, jax.ShapeDtypeStruct((n // dil, dil * LANES), F32)]
    res = pl.pallas_call(
        functools.partial(_band_body, blk=blk, band_prev=band_prev, cfg_prev=prev is not None, final=final),
        grid=(batch, dil, nbl),
        in_specs=in_specs,
        out_specs=out_specs,
        out_shape=out_shape,
        compiler_params=_cparams(("parallel", "parallel", "parallel")),
        name="dilated_band_attention",
    )(*args)
    if final:
        return res.reshape(n, D_INNER)
    return res[0].reshape(n, D_INNER), res[1].reshape(n, LANES)


def _mixer_b(h, g, w_in, w_out, batch, seq, tabs, zero_aux):
    kinds = ([("rot128", B_HEAD_DIM ** -0.5)] * 16 + [("rot128", 1.0)] * 16 + [("none", 1.0)] * 16
             + [("silu", 1.0)] * 16)
    pb = _proj(h, g, w_in.astype(BF16), kinds, 1024, BF16, tabs, jnp.zeros((1, 4 * D_INNER), F32), seq)
    prev = None
    configs = sorted(B_CONFIGS, key=lambda wd: -wd[1])
    for window, dil in configs:
        assert window // dil == 128 and seq % dil == 0
        last = dil == configs[-1][1]
        prev = _band_attn(pb, batch, seq, dil, prev, final=last)
    return prev


def _retention_body(lg_ref, q_ref, k_ref, v_ref, z_ref, o_ref, st_ref, *, seq):
    c = C_CHUNK
    lg = lg_ref[pl.program_id(1)]
    ri = lax.broadcasted_iota(jnp.int32, (c, c), 0)
    ci = lax.broadcasted_iota(jnp.int32, (c, c), 1)
    rel = (ri - ci).astype(F32)
    decay = jnp.where(rel >= 0, jnp.exp(jnp.maximum(rel, 0.0) * lg), 0.0)
    idx = lax.broadcasted_iota(jnp.int32, (c, 1), 0).astype(F32)
    k_decay = jnp.exp((c - 1 - idx) * lg)
    q_decay = jnp.exp((idx + 1.0) * lg)
    chunk_decay = jnp.exp(jnp.full((1, C_QK_DIM), c, F32) * lg)
    st_ref[...] = jnp.zeros(st_ref.shape, F32)

    def chunk(nidx, carry):
        r0 = pl.multiple_of(nidx * c, c)
        q = q_ref[pl.ds(r0, c), :]
        k = k_ref[pl.ds(r0, c), :]
        v = v_ref[pl.ds(r0, c), :]
        scores = _dot_nt(q, k) * decay
        o = _dot(scores.astype(BF16), v)
        st = st_ref[...]
        o += _dot_nt((q.astype(F32) * q_decay).astype(BF16), st.astype(BF16))
        st_ref[...] = st * chunk_decay + _dot_tn(v, (k.astype(F32) * k_decay).astype(BF16))
        o = o * lax.rsqrt(jnp.mean(o * o, axis=-1, keepdims=True) + NORM_EPS)
        o_ref[pl.ds(r0, c), :] = (o * z_ref[pl.ds(r0, c), :].astype(F32)).astype(o_ref.dtype)
        return carry

    lax.fori_loop(0, seq // c, chunk, 0)


def _mixer_c(h, g, w_in, w_out, batch, seq, tabs, zero_aux):
    kinds = ([("rot128", 1.0)] * 8 + [("rot128", C_QK_DIM ** -0.5)] * 8 + [("none", 1.0)] * 16
             + [("silu", 1.0)] * 16)
    pc = _proj(h, g, w_in.astype(BF16), kinds, 1024, BF16, tabs, jnp.zeros((1, w_in.shape[1]), F32), seq)
    n = pc.shape[0]
    nh = C_HEADS
    log_gamma = jnp.log1p(-jnp.exp2(-5.0 - jnp.arange(nh, dtype=F32)))
    return pl.pallas_call(
        functools.partial(_retention_body, seq=seq),
        grid=(batch, nh),
        in_specs=[
            pl.BlockSpec(memory_space=pltpu.SMEM),
            pl.BlockSpec((seq, C_QK_DIM), lambda b, h: (b, h)),
            pl.BlockSpec((seq, C_QK_DIM), lambda b, h: (b, nh + h)),
            pl.BlockSpec((seq, C_V_DIM), lambda b, h: (b, nh + h)),
            pl.BlockSpec((seq, C_V_DIM), lambda b, h: (b, 2 * nh + h)),
        ],
        out_specs=pl.BlockSpec((seq, C_V_DIM), lambda b, h: (b, h)),
        out_shape=jax.ShapeDtypeStruct((n, D_INNER), BF16),
        scratch_shapes=[pltpu.VMEM((C_V_DIM, C_QK_DIM), F32)],
        compiler_params=_cparams(("parallel", "parallel")),
        name="retention",
    )(log_gamma, pc, pc, pc, pc)


def _hgrn_body(q_ref, g_ref, v_ref, z_ref, o_ref, st_ref, *, seq):
    c = D_CHUNK
    w = D_HEAD_GROUP * LANES
    nsub = c // D_SUB
    halves = [D_SUB * 2 ** lv for lv in range(16) if D_SUB * 2 ** (lv + 1) <= c]
    ri = lax.broadcasted_iota(jnp.int32, (c, c), 0)
    ci = lax.broadcasted_iota(jnp.int32, (c, c), 1)
    mats = [ri >= ci]
    pair_masks = []
    for half in halves:
        block = 2 * half
        mid = (ri // block) * block + half - 1
        right = (ri % block) >= half
        mats.append((right & (ci > mid) & (ci <= ri)) | (jnp.logical_not(right) & (ci > ri) & (ci <= mid)))
        pair_masks.append(((ri // block) == (ci // block)) & right & ((ci % block) < half))
    m1 = jnp.concatenate([m.astype(BF16) for m in mats], axis=0)
    m3 = jnp.concatenate([m1, m1, m1], axis=1)
    rows = lax.broadcasted_iota(jnp.int32, (c, w), 0)
    right_rows = [(rows % (2 * half)) >= half for half in halves]
    sub = rows % D_SUB
    st_ref[...] = jnp.zeros(st_ref.shape, F32)

    def bcast_row(x, j):
        x3 = x.reshape(nsub, D_SUB, w)
        return jnp.broadcast_to(x3[:, j:j + 1, :], (nsub, D_SUB, w)).reshape(c, w)

    head_sl = [slice(hd * LANES, (hd + 1) * LANES) for hd in range(D_HEAD_GROUP)]
    unroll = min(D_UNROLL, seq // c)

    def group(gidx, carry):
        rows0 = [pl.multiple_of((gidx * unroll + u) * c, c) for u in range(unroll)]
        gates = [g_ref[pl.ds(r0, c), :] for r0 in rows0]
        bds = []
        for gate in gates:
            lf = jnp.log(gate)
            hi = lf.astype(BF16)
            r1 = lf - hi.astype(F32)
            mid_part = r1.astype(BF16)
            lo = (r1 - mid_part.astype(F32)).astype(BF16)
            bds.append(_dot(m3, jnp.concatenate([hi, mid_part, lo], axis=0)))
        prep = []
        for u, r0 in enumerate(rows0):
            bd, kk = bds[u], 1.0 - gates[u]
            b = bd[0:c]
            q = q_ref[pl.ds(r0, c), :].astype(F32)
            v = v_ref[pl.ds(r0, c), :]
            vf = v.astype(F32)
            qe = (q * jnp.exp(b)).astype(BF16)
            qts, kts = [], []
            for lv in range(len(halves)):
                e = jnp.exp(bd[(lv + 1) * c:(lv + 2) * c])
                qts.append(jnp.where(right_rows[lv], q * e, 0.0).astype(BF16))
                kts.append(jnp.where(right_rows[lv], 0.0, kk * e).astype(BF16))
            o_diag = [jnp.zeros((c, LANES), F32) for _ in head_sl]
            for j in range(D_SUB):
                x = q * bcast_row(kk, j) * jnp.exp(b - bcast_row(b, j))
                if j:
                    x = jnp.where(sub >= j, x, 0.0)
                vs = bcast_row(vf, j)
                for hd, sl in enumerate(head_sl):
                    o_diag[hd] = o_diag[hd] + jnp.sum(x[:, sl], axis=1, keepdims=True) * vs[:, sl]
            b_last = b[c - 1:c, :]
            k_dec = (kk * jnp.exp(b_last - b)).astype(BF16)
            prep.append((qe, qts, kts, v, o_diag, k_dec, jnp.exp(b_last)))
        o_local = []
        for qe, qts, kts, v, o_diag, k_dec, eb_last in prep:
            per_head = []
            for hd, sl in enumerate(head_sl):
                a = jnp.zeros((c, c), F32)
                for lv in range(len(halves)):
                    a = a + jnp.where(pair_masks[lv], _dot_nt(qts[lv][:, sl], kts[lv][:, sl]), 0.0)
                per_head.append(_dot(a.astype(BF16), v[:, sl]) + o_diag[hd])
            o_local.append(per_head)
        outs = []
        for u, (qe, qts, kts, v, o_diag, k_dec, eb_last) in enumerate(prep):
            per_head = []
            for hd, sl in enumerate(head_sl):
                st = st_ref[hd]
                per_head.append(o_local[u][hd] + _dot_nt(qe[:, sl], st.astype(BF16)))
                st_ref[hd] = st * eb_last[:, sl] + _dot_tn(v[:, sl], k_dec[:, sl])
            outs.append(per_head)
        for u, r0 in enumerate(rows0):
            for hd, sl in enumerate(head_sl):
                o = outs[u][hd]
                o = o * lax.rsqrt(jnp.mean(o * o, axis=-1, keepdims=True) + NORM_EPS)
                o_ref[pl.ds(r0, c), sl] = (o * z_ref[pl.ds(r0, c), sl].astype(F32)).astype(o_ref.dtype)
        return carry

    lax.fori_loop(0, seq // (c * unroll), group, 0)


def _mixer_d(h, g, w_in, w_out, lower_bound, batch, seq, tabs, zero_aux):
    wq, wf, wi, wz = (w_in[:, t * D_INNER:(t + 1) * D_INNER] for t in range(4))
    w1 = jnp.concatenate([wq, wi, wz], axis=1).astype(BF16)
    kinds = [("silu", D_K_DIM ** -0.5)] * 16 + [("none", 1.0)] * 16 + [("silu", 1.0)] * 16
    pd = _proj(h, g, w1, kinds, 1024, BF16, tabs, jnp.zeros((1, 3 * D_INNER), F32), seq)
    gates = _proj(h, g, wf.astype(BF16), [("hgrn_g", 1.0)] * 16, 1024, F32, tabs,
                  lower_bound.reshape(1, D_INNER).astype(F32), seq)
    n = pd.shape[0]
    ng = D_HEADS // D_HEAD_GROUP
    w = D_HEAD_GROUP * LANES
    return pl.pallas_call(
        functools.partial(_hgrn_body, seq=seq),
        grid=(batch, ng),
        in_specs=[
            pl.BlockSpec((seq, w), lambda b, h: (b, h)),
            pl.BlockSpec((seq, w), lambda b, h: (b, h)),
            pl.BlockSpec((seq, w), lambda b, h: (b, ng + h)),
            pl.BlockSpec((seq, w), lambda b, h: (b, 2 * ng + h)),
        ],
        out_specs=pl.BlockSpec((seq, w), lambda b, h: (b, h)),
        out_shape=jax.ShapeDtypeStruct((n, D_INNER), BF16),
        scratch_shapes=[pltpu.VMEM((D_HEAD_GROUP, D_V_DIM, D_K_DIM), F32)],
        compiler_params=_cparams(("parallel", "parallel")),
        name="hgrn2_recurrence",
    )(pd, gates, pd, pd)


def kernel(x, norm_g, final_g, a_w_in, a_w_out, b_w_in, b_w_out, c_w_in, c_w_out, d_w_in, d_w_out, hgrn_lb_logits):
    batch, seq, d = x.shape
    depth = norm_g.shape[0]
    lb_cum = jnp.cumsum(jax.nn.softmax(hgrn_lb_logits.astype(F32), axis=0), axis=0)
    lower_bounds = lb_cum - lb_cum[0]
    tabs = _rope_tables(seq, 128) + _rope_tables(seq, 64)
    h = x.reshape(batch * seq, d)
    for layer in range(depth):
        kind, slot = layer % N_MIXERS, layer // N_MIXERS
        g = norm_g[layer]
        if kind == 0:
            og, w_out = _mixer_a(h, g, a_w_in[slot], a_w_out[slot], batch, seq, tabs, None), a_w_out[slot]
        elif kind == 1:
            og, w_out = _mixer_b(h, g, b_w_in[slot], b_w_out[slot], batch, seq, tabs, None), b_w_out[slot]
        elif kind == 2:
            og, w_out = _mixer_c(h, g, c_w_in[slot], c_w_out[slot], batch, seq, tabs, None), c_w_out[slot]
        else:
            og, w_out = _mixer_d(h, g, d_w_in[slot], d_w_out[slot], lower_bounds[layer], batch, seq, tabs, None), d_w_out[slot]
        h = _outproj(og, w_out.astype(BF16), h, final_g, final=(layer == depth - 1))
    return h.reshape(batch, seq, d)
```

```python
import functools

import jax
import jax.numpy as jnp
import numpy as np
from jax import lax
from jax.experimental import pallas as pl
from jax.experimental.pallas import tpu as pltpu

F32 = jnp.float32
BF16 = jnp.bfloat16

D_MODEL = 1024
D_INNER = 2048
N_MIXERS = 4
NORM_EPS = 1e-6
ROPE_THETA = 10000.0
LANES = 128
NEG_BIG = -1e30
LOG2E = 1.4426950408889634

A_HEADS, A_HEAD_DIM = 16, 128
A_IDX_HEADS, A_IDX_DIM = 16, 64
A_TOPK_MAX = 256
A_Q_BLOCK = 256
A_K_BLOCK = 512
A_HEAD_STAGE = 16
B_HEADS, B_HEAD_DIM = 16, 128
B_CONFIGS = ((128, 1), (512, 4), (2048, 16))
B_DIL_RATIO = 4
C_HEADS, C_QK_DIM, C_V_DIM, C_CHUNK = 8, 128, 256, 128
D_HEADS, D_K_DIM, D_V_DIM = 16, 128, 128
D_CHUNK = 64
D_SUB = 8
D_HEAD_GROUP = 2
D_UNROLL = 2

VMEM_LIMIT = 56 * 1024 * 1024


def _cparams(sem):
    return pltpu.CompilerParams(dimension_semantics=sem, vmem_limit_bytes=VMEM_LIMIT)


def _dot(a, b):
    return jnp.dot(a, b, preferred_element_type=F32)


def _dot_nt(a, b):
    return lax.dot_general(a, b, (((1,), (1,)), ((), ())), preferred_element_type=F32)


def _dot_tn(a, b):
    return lax.dot_general(a, b, (((0,), (0,)), ((), ())), preferred_element_type=F32)


def _rope_tables(seq, dh):
    freqs = ROPE_THETA ** (-jnp.arange(0, dh, 2, dtype=F32) / dh)
    ang = jnp.arange(seq, dtype=F32)[:, None] * freqs[None, :]
    cos, sin = jnp.cos(ang), jnp.sin(ang)
    cosf = jnp.concatenate([cos, cos], axis=-1)
    sinf = jnp.concatenate([-sin, sin], axis=-1)
    rep = LANES // dh
    return jnp.tile(cosf, (1, rep)), jnp.tile(sinf, (1, rep))


def _epilogue(kind, scale, x, tabs, aux):
    c128, s128, c64, s64 = tabs
    if kind == "rot128":
        y = x * c128 + pltpu.roll(x, 64, 1) * s128
    elif kind == "rot64":
        lane = lax.broadcasted_iota(jnp.int32, x.shape, 1)
        partner = jnp.where((lane % 64) < 32, pltpu.roll(x, 96, 1), pltpu.roll(x, 32, 1))
        y = x * c64 + partner * s64
    elif kind == "silu":
        y = x * jax.nn.sigmoid(x)
    elif kind == "hgrn_g":
        y = aux + (1.0 - aux) * jax.nn.sigmoid(x)
    else:
        y = x
    if scale != 1.0:
        y = y * scale
    return y


def _proj_body(h_ref, g_ref, w_ref, c128_ref, s128_ref, c64_ref, s64_ref, aux_ref, *rest, patterns, tile_pat, res_dils):
    o_ref = rest[0]
    res_refs = rest[1:1 + len(res_dils)]
    u_ref = rest[1 + len(res_dils)]
    j = pl.program_id(1)
    tm = o_ref.shape[0]

    @pl.when(j == 0)
    def _norm():
        x = h_ref[...]
        ms = jnp.mean(x * x, axis=-1, keepdims=True)
        u_ref[...] = (x * lax.rsqrt(ms + NORM_EPS) * g_ref[...]).astype(BF16)

    acc = _dot(u_ref[...], w_ref[...])

    def epi(pat):
        kinds = {p[0] for p in pat}
        tabs = (c128_ref[...] if "rot128" in kinds else None, s128_ref[...] if "rot128" in kinds else None,
                c64_ref[...] if "rot64" in kinds else None, s64_ref[...] if "rot64" in kinds else None)
        for c, (kind, scale, residue) in enumerate(pat):
            sl = slice(c * LANES, (c + 1) * LANES)
            aux = aux_ref[:, sl] if kind == "hgrn_g" else None
            y = _epilogue(kind, scale, acc[:, sl], tabs, aux)
            o_ref[:, sl] = y.astype(o_ref.dtype)
            if residue and res_dils:
                y_ref = rest[-1]
                y_ref[...] = y
                for dil, r_ref in zip(res_dils, res_refs):
                    for r in range(dil):
                        r_ref[r, :, sl] = y_ref[pl.ds(r, tm // dil, stride=dil), :].astype(r_ref.dtype)

    if len(patterns) == 1:
        epi(patterns[0])
    else:
        for pid, pat in enumerate(patterns):
            tiles = [t for t, p in enumerate(tile_pat) if p == pid]
            cond = functools.reduce(jnp.logical_or, [j == t for t in tiles])
            pl.when(cond)(functools.partial(epi, pat))


def _proj(h, g, w, chunk_kinds, tn, out_dtype, tabs, aux, seq, res_dils=()):
    n, d = h.shape
    c = w.shape[1]
    tm = min(1024, seq)
    per_tile = tn // LANES
    chunk_kinds = [tuple(k) + (False,) * (3 - len(k)) for k in chunk_kinds]
    tile_kinds = [tuple(chunk_kinds[t * per_tile:(t + 1) * per_tile]) for t in range(c // tn)]
    patterns = tuple(dict.fromkeys(tile_kinds))
    tile_pat = tuple(patterns.index(t) for t in tile_kinds)
    pos_blocks = seq // tm
    tab_spec = pl.BlockSpec((tm, LANES), lambda i, j: (i % pos_blocks, 0))
    out_specs = [pl.BlockSpec((tm, tn), lambda i, j: (i, j))]
    out_shape = [jax.ShapeDtypeStruct((n, c), out_dtype)]
    scratch = [pltpu.VMEM((tm, d), BF16)]
    if res_dils:
        res_tiles = sum(all(k[2] for k in t) for t in tile_kinds)
        assert res_tiles * per_tile == sum(k[2] for k in chunk_kinds)
        for dil in res_dils:
            out_specs.append(pl.BlockSpec((dil, tm // dil, tn), lambda i, j: (0, i, jnp.minimum(j, res_tiles - 1))))
            out_shape.append(jax.ShapeDtypeStruct((dil, n // dil, res_tiles * tn), out_dtype))
        scratch.append(pltpu.VMEM((tm, LANES), F32))
    res = pl.pallas_call(
        functools.partial(_proj_body, patterns=patterns, tile_pat=tile_pat, res_dils=tuple(res_dils)),
        grid=(n // tm, c // tn),
        in_specs=[
            pl.BlockSpec((tm, d), lambda i, j: (i, 0)),
            pl.BlockSpec((1, d), lambda i, j: (0, 0)),
            pl.BlockSpec((d, tn), lambda i, j: (0, j)),
            tab_spec, tab_spec, tab_spec, tab_spec,
            pl.BlockSpec((1, tn), lambda i, j: (0, j)),
        ],
        out_specs=out_specs,
        out_shape=out_shape,
        scratch_shapes=scratch,
        compiler_params=_cparams(("parallel", "arbitrary")),
        name="norm_in_proj",
    )(h, g.reshape(1, d), w, *tabs, aux)
    return res if res_dils else res[0]


def _outproj_body(og_ref, w_ref, h_ref, gf_ref, o_ref, *, final):
    y = h_ref[...] + _dot(og_ref[...], w_ref[...])
    if final:
        ms = jnp.mean(y * y, axis=-1, keepdims=True)
        y = y * lax.rsqrt(ms + NORM_EPS) * gf_ref[...]
    o_ref[...] = y


def _outproj(og, w, h, gf, final):
    n, di = og.shape
    d = w.shape[1]
    tm = min(512, n)
    return pl.pallas_call(
        functools.partial(_outproj_body, final=final),
        grid=(n // tm,),
        in_specs=[
            pl.BlockSpec((tm, di), lambda i: (i, 0)),
            pl.BlockSpec((di, d), lambda i: (0, 0)),
            pl.BlockSpec((tm, d), lambda i: (i, 0)),
            pl.BlockSpec((1, d), lambda i: (0, 0)),
        ],
        out_specs=pl.BlockSpec((tm, d), lambda i: (i, 0)),
        out_shape=jax.ShapeDtypeStruct((n, d), F32),
        compiler_params=_cparams(("parallel",)),
        name="out_proj_residual",
    )(og, w, h, gf.reshape(1, d))


def _dsa_index_body(iq_ref, ik_ref, iw_ref, bias_ref, sc_ref, *, topk, tq, kc, seq):
    i = pl.program_id(1)
    n_chunks = ((i + 1) * tq + kc - 1) // kc
    int_min = jnp.int32(-2 ** 31)
    iw_t = iw_ref[...].astype(F32).T
    lane = lax.broadcasted_iota(jnp.int32, (kc, LANES), 1)
    s_loc = lax.broadcasted_iota(jnp.int32, (kc, tq), 0)
    t_pos = i * tq + lax.broadcasted_iota(jnp.int32, (kc, tq), 1)

    def score_chunk(c, carry):
        k0 = pl.multiple_of(c * kc, kc)
        ik2 = ik_ref[pl.ds(k0, kc), :]
        ik_lo = jnp.where(lane < A_IDX_DIM, ik2, jnp.zeros_like(ik2))
        ik_hi = jnp.where(lane >= A_IDX_DIM, ik2, jnp.zeros_like(ik2))
        score = jnp.zeros((kc, tq), F32)
        for p in range(A_IDX_HEADS // 2):
            iq2 = iq_ref[:, p * LANES:(p + 1) * LANES]
            score += jnp.maximum(_dot_nt(ik_lo, iq2), 0.0) * iw_t[2 * p:2 * p + 1, :]
            score += jnp.maximum(_dot_nt(ik_hi, iq2), 0.0) * iw_t[2 * p + 1:2 * p + 2, :]
        sc_ref[pl.ds(k0, kc), :] = jnp.where(k0 + s_loc <= t_pos, score, -jnp.inf)
        return carry

    lax.fori_loop(0, n_chunks, score_chunk, 0)

    def threshold(prefix):
        key = prefix ^ int_min
        return pltpu.bitcast(jnp.where(key < 0, key ^ jnp.int32(0x7FFFFFFF), key), F32)

    def bisect(it, prefix):
        cand = prefix | (jnp.int32(1) << (31 - it))
        thr = threshold(cand)

        def count(c, cnt):
            blk = sc_ref[pl.ds(pl.multiple_of(c * kc, kc), kc), :]
            return cnt + jnp.sum(jnp.where(blk >= thr, 1.0, 0.0), axis=0, keepdims=True)

        cnt = lax.fori_loop(0, n_chunks, count, jnp.zeros((1, tq), F32))
        return jnp.where(cnt >= topk, cand, prefix)

    n_iter = jnp.where((i + 1) * tq <= topk, 0, 32)
    prefix = lax.fori_loop(0, n_iter, bisect, jnp.zeros((1, tq), jnp.int32))
    thr = threshold(prefix)
    take_all = prefix == 0

    def emit(c, carry):
        k0 = pl.multiple_of(c * kc, kc)
        sel = ((sc_ref[pl.ds(k0, kc), :] >= thr) | take_all) & (k0 + s_loc <= t_pos)
        bias_ref[pl.ds(k0, kc), :] = jnp.where(sel, 0.0, NEG_BIG).astype(bias_ref.dtype)
        return carry

    lax.fori_loop(0, n_chunks, emit, 0)

    def emit_masked(c, carry):
        bias_ref[pl.ds(pl.multiple_of(c * kc, kc), kc), :] = jnp.full((kc, tq), NEG_BIG, bias_ref.dtype)
        return carry

    lax.fori_loop(n_chunks, seq // kc, emit_masked, 0)


def _dsa_index(pa, batch, seq, topk):
    tq = min(A_Q_BLOCK, seq)
    kc = min(A_K_BLOCK, seq)
    nq = seq // tq
    return pl.pallas_call(
        functools.partial(_dsa_index_body, topk=topk, tq=tq, kc=kc, seq=seq),
        grid=(batch, nq),
        in_specs=[
            pl.BlockSpec((tq, 1024), lambda b, i: (b * nq + i, 4)),
            pl.BlockSpec((seq, LANES), lambda b, i: (b, 42)),
            pl.BlockSpec((tq, LANES), lambda b, i: (b * nq + i, 43)),
        ],
        out_specs=pl.BlockSpec((None, None, seq, tq), lambda b, i: (b, i, 0, 0)),
        out_shape=jax.ShapeDtypeStruct((batch, nq, seq, tq), BF16),
        scratch_shapes=[pltpu.VMEM((seq, tq), F32)],
        compiler_params=_cparams(("parallel", "parallel")),
        name="dsa_indexer_topk_mask",
    )(pa, pa, pa)


def _dsa_attn_body(q_ref, z_ref, k_ref, v_ref, bias_ref, o_ref, m_ref, l_ref, acc_ref, *, tq, kb):
    i = pl.program_id(1)
    j = pl.program_id(2)
    nh = A_HEADS
    dv = A_HEAD_DIM

    @pl.when(j == 0)
    def _init():
        m_ref[...] = jnp.full(m_ref.shape, NEG_BIG, F32)
        l_ref[...] = jnp.zeros(l_ref.shape, F32)
        acc_ref[...] = jnp.zeros(acc_ref.shape, F32)

    @pl.when(j * kb <= i * tq + tq - 1)
    def _step():
        k = k_ref[...]
        v_t = v_ref[...].astype(F32).T.astype(BF16)
        bias = bias_ref[...].astype(F32)
        for h0 in range(0, nh, A_HEAD_STAGE):
            hs = range(h0, h0 + A_HEAD_STAGE)
            s = [_dot_nt(k, q_ref[:, h * LANES:(h + 1) * LANES]) + bias for h in hs]
            ps, alphas = [], []
            for n, h in enumerate(hs):
                m_old = m_ref[h:h + 1, :]
                m_new = jnp.maximum(m_old, jnp.max(s[n], axis=0, keepdims=True))
                alpha = jnp.exp2(m_old - m_new)
                p = jnp.exp2(s[n] - m_new)
                l_ref[h:h + 1, :] = alpha * l_ref[h:h + 1, :] + jnp.sum(p, axis=0, keepdims=True)
                m_ref[h:h + 1, :] = m_new
                ps.append(p.astype(BF16))
                alphas.append(alpha)
            for n, h in enumerate(hs):
                rows = slice(h * dv, (h + 1) * dv)
                acc_ref[rows, :] = alphas[n] * acc_ref[rows, :] + _dot(v_t, ps[n])

    @pl.when(j == pl.num_programs(2) - 1)
    def _fin():
        for h in range(nh):
            sl = slice(h * LANES, (h + 1) * LANES)
            o_t = acc_ref[h * dv:(h + 1) * dv, :] * (1.0 / l_ref[h:h + 1, :])
            o_ref[:, sl] = (o_t.T * z_ref[:, sl].astype(F32)).astype(o_ref.dtype)


def _dsa_attn(pa, bias, batch, seq):
    n = pa.shape[0]
    tq = bias.shape[-1]
    kb = min(A_K_BLOCK, seq)
    nq, nk = seq // tq, seq // kb

    def kj(i, j):
        return jnp.minimum(j, (i * tq + tq - 1) // kb)

    return pl.pallas_call(
        functools.partial(_dsa_attn_body, tq=tq, kb=kb),
        grid=(batch, nq, nk),
        in_specs=[
            pl.BlockSpec((tq, D_INNER), lambda b, i, j: (b * nq + i, 0)),
            pl.BlockSpec((tq, D_INNER), lambda b, i, j: (b * nq + i, 1)),
            pl.BlockSpec((kb, LANES), lambda b, i, j: (b * nk + kj(i, j), 40)),
            pl.BlockSpec((kb, LANES), lambda b, i, j: (b * nk + kj(i, j), 41)),
            pl.BlockSpec((None, None, kb, tq), lambda b, i, j: (b, i, kj(i, j), 0)),
        ],
        out_specs=pl.BlockSpec((tq, D_INNER), lambda b, i, j: (b * nq + i, 0)),
        out_shape=jax.ShapeDtypeStruct((n, D_INNER), BF16),
        scratch_shapes=[
            pltpu.VMEM((A_HEADS, tq), F32),
            pltpu.VMEM((A_HEADS, tq), F32),
            pltpu.VMEM((A_HEADS * A_HEAD_DIM, tq), F32),
        ],
        compiler_params=_cparams(("parallel", "parallel", "arbitrary")),
        name="dsa_masked_attention",
    )(pa, pa, pa, pa, bias)


def _mixer_a(h, g, w_in, w_out, batch, seq, tabs, zero_aux):
    hd = A_HEADS * A_HEAD_DIM
    hi = A_IDX_HEADS * A_IDX_DIM
    o0 = np.cumsum([0, hd, A_HEAD_DIM, A_HEAD_DIM, hi, A_IDX_DIM, A_IDX_HEADS, D_INNER])
    wq, wk, wv, wiq, wik, wiw, wz = (w_in[:, o0[t]:o0[t + 1]] for t in range(7))
    pad = jnp.zeros((D_MODEL, LANES - A_IDX_HEADS), F32)
    w = jnp.concatenate([wq, wz, wiq, wk, wv, wik, wik, wiw, pad], axis=1).astype(BF16)
    kinds = ([("rot128", A_HEAD_DIM ** -0.5 * LOG2E)] * 16 + [("silu", 1.0)] * 16 + [("rot64", 1.0)] * 8
             + [("rot128", 1.0), ("none", 1.0), ("rot64", 1.0), ("none", A_IDX_HEADS ** -0.5 * A_IDX_DIM ** -0.5)])
    pa = _proj(h, g, w, kinds, 1408, BF16, tabs, jnp.zeros((1, w.shape[1]), F32), seq)
    bias = _dsa_index(pa, batch, seq, min(A_TOPK_MAX, seq // 4))
    return _dsa_attn(pa, bias, batch, seq)


def _band_body(*refs, blk, band_prev, cfg_prev, final):
    refs = list(refs)
    q_ref, kc_ref, vc_ref = refs[:3]
    del refs[:3]
    if band_prev:
        kp_ref, vp_ref = refs[:2]
        del refs[:2]
    if cfg_prev:
        op4_ref, lp4_ref = refs[:2]
        del refs[:2]
        op_ref, lp_ref = refs[-2:]
        del refs[-2:]
        for a in range(B_DIL_RATIO):
            rows = pl.ds(a, blk // B_DIL_RATIO, stride=B_DIL_RATIO)
            lp_ref[rows, :] = lp4_ref[a]
            for h in range(B_HEADS):
                op_ref[h, rows, :] = op4_ref[a, :, h * LANES:(h + 1) * LANES].astype(F32)
    if final:
        z_ref, o_ref = refs
    else:
        o_ref, l_ref = refs
    ki = lax.broadcasted_iota(jnp.int32, (blk, blk), 0)
    qi = lax.broadcasted_iota(jnp.int32, (blk, blk), 1)
    mask_cur = ki <= qi
    eye = ki == qi
    if band_prev:
        mask_prev = (ki >= qi) & (pl.program_id(2) > 0)
    if cfg_prev:
        lse_prev_t = _pad_rows(lp_ref[...].T, LANES)
    heads = [slice(h * LANES, (h + 1) * LANES) for h in range(B_HEADS)]
    s_c = [jnp.where(mask_cur, _dot_nt(kc_ref[:, sl], q_ref[:, sl]), NEG_BIG) for sl in heads]
    if band_prev:
        s_p = [jnp.where(mask_prev, _dot_nt(kp_ref[:, sl], q_ref[:, sl]), NEG_BIG) for sl in heads]
    lse_rows, p_c, p_p, d_old = [], [], [], []
    for h in range(B_HEADS):
        m = jnp.max(s_c[h], axis=0, keepdims=True)
        if band_prev:
            m = jnp.maximum(m, jnp.max(s_p[h], axis=0, keepdims=True))
        e_c = jnp.exp(s_c[h] - m)
        den = jnp.sum(e_c, axis=0, keepdims=True)
        if band_prev:
            e_p = jnp.exp(s_p[h] - m)
            den += jnp.sum(e_p, axis=0, keepdims=True)
        lse = m + jnp.log(den)
        scale = 1.0 / den
        if cfg_prev:
            lse_prev = lse_prev_t[h:h + 1, :]
            m2 = jnp.maximum(lse, lse_prev)
            w_new, w_old = jnp.exp(lse - m2), jnp.exp(lse_prev - m2)
            scale = scale * w_new / (w_new + w_old)
            d_old.append(jnp.where(eye, w_old / (w_new + w_old), 0.0).astype(BF16))
            lse = m2 + jnp.log(w_new + w_old)
        p_c.append((e_c * scale).astype(BF16))
        if band_prev:
            p_p.append((e_p * scale).astype(BF16))
        lse_rows.append(lse)
    for h, sl in enumerate(heads):
        o = _dot_tn(p_c[h], vc_ref[:, sl])
        if band_prev:
            o += _dot_tn(p_p[h], vp_ref[:, sl])
        if cfg_prev:
            o += _dot(d_old[h], op_ref[h].astype(BF16))
        if final:
            o_ref[:, sl] = (o * z_ref[:, sl].astype(F32)).astype(o_ref.dtype)
        else:
            o_ref[:, sl] = o.astype(o_ref.dtype)
    if not final:
        lse_t = _pad_rows(jnp.concatenate(lse_rows, axis=0), LANES)
        l_ref[...] = lse_t.T


def _pad_rows(x, rows):
    if x.shape[0] == rows:
        return x
    return jnp.concatenate([x, jnp.zeros((rows - x.shape[0], x.shape[1]), x.dtype)], axis=0)


def _band_attn(src, batch, seq, dil, prev, final):
    n = src.shape[0] * src.shape[1]
    length = seq // dil
    blk = min(128, length)
    nbl = length // blk
    band_prev = nbl > 1

    def spec(col, prev_blk=False):
        if prev_blk:
            return pl.BlockSpec((None, blk, D_INNER), lambda b, r, i: (r, b * nbl + jnp.maximum(i - 1, 0), col))
        return pl.BlockSpec((None, blk, D_INNER), lambda b, r, i: (r, b * nbl + i, col))

    o_spec = pl.BlockSpec((None, blk, D_INNER), lambda b, r, i: (r, b * nbl + i, 0))
    l_spec = pl.BlockSpec((None, blk, LANES), lambda b, r, i: (r, b * nbl + i, 0))
    in_specs, args, scratch = [spec(0), spec(1), spec(2)], [src, src, src], []
    if band_prev:
        in_specs += [spec(1, True), spec(2, True)]
        args += [src, src]
    if prev is not None:
        coarse = B_DIL_RATIO * dil
        assert prev[0].shape[0] == coarse and blk % (16 * B_DIL_RATIO) == 0
        pblk = blk // B_DIL_RATIO
        in_specs += [pl.BlockSpec((B_DIL_RATIO, None, pblk, D_INNER), lambda b, r, i: (0, r, b * nbl + i, 0)),
                     pl.BlockSpec((B_DIL_RATIO, None, pblk, LANES), lambda b, r, i: (0, r, b * nbl + i, 0))]
        args += [prev[0].reshape(B_DIL_RATIO, dil, n // coarse, D_INNER),
                 prev[1].reshape(B_DIL_RATIO, dil, n // coarse, LANES)]
        scratch = [pltpu.VMEM((B_HEADS, blk, LANES), F32), pltpu.VMEM((blk, LANES), F32)]
    o_shape = jax.ShapeDtypeStruct((dil, n // dil, D_INNER), BF16)
    if final:
        in_specs.append(spec(3))
        args.append(src)
        out_specs, out_shape = o_spec, o_shape
    else:
        out_specs = [o_spec, l_spec]
        out_shape = [o_shape, jax.ShapeDtypeStruct((dil, n // dil, LANES), F32)]
    return pl.pallas_call(
        functools.partial(_band_body, blk=blk, band_prev=band_prev, cfg_prev=prev is not None, final=final),
        grid=(batch, dil, nbl),
        in_specs=in_specs,
        out_specs=out_specs,
        out_shape=out_shape,
        scratch_shapes=scratch,
        compiler_params=_cparams(("parallel", "parallel", "parallel")),
        name="dilated_band_attention",
    )(*args)


def _mixer_b(h, g, w_in, w_out, batch, seq, tabs, zero_aux):
    configs = sorted(B_CONFIGS, key=lambda wd: -wd[1])
    dils = [dil for _, dil in configs]
    assert dils[-1] == 1 and all(a == B_DIL_RATIO * b for a, b in zip(dils, dils[1:]))
    assert all(window // dil == 128 and seq % dil == 0 for window, dil in configs)
    kinds = ([("rot128", B_HEAD_DIM ** -0.5, True)] * 16 + [("rot128", 1.0, True)] * 16 + [("none", 1.0, True)] * 16
             + [("silu", 1.0, False)] * 16)
    outs = _proj(h, g, w_in.astype(BF16), kinds, 1024, BF16, tabs, jnp.zeros((1, 4 * D_INNER), F32), seq,
                 res_dils=dils[:-1])
    srcs = list(outs[1:]) + [outs[0].reshape(1, *outs[0].shape)]
    prev = None
    for src, dil in zip(srcs, dils):
        prev = _band_attn(src, batch, seq, dil, prev, final=dil == 1)
    return prev.reshape(-1, D_INNER)


def _retention_body(lg_ref, q_ref, k_ref, v_ref, z_ref, o_ref, st_ref, *, seq):
    c = C_CHUNK
    lg = lg_ref[pl.program_id(1)]
    ri = lax.broadcasted_iota(jnp.int32, (c, c), 0)
    ci = lax.broadcasted_iota(jnp.int32, (c, c), 1)
    rel = (ri - ci).astype(F32)
    decay = jnp.where(rel >= 0, jnp.exp(jnp.maximum(rel, 0.0) * lg), 0.0)
    idx = lax.broadcasted_iota(jnp.int32, (c, 1), 0).astype(F32)
    k_decay = jnp.exp((c - 1 - idx) * lg)
    q_decay = jnp.exp((idx + 1.0) * lg)
    chunk_decay = jnp.exp(jnp.full((1, C_QK_DIM), c, F32) * lg)
    st_ref[...] = jnp.zeros(st_ref.shape, F32)

    def chunk(nidx, carry):
        r0 = pl.multiple_of(nidx * c, c)
        q = q_ref[pl.ds(r0, c), :]
        k = k_ref[pl.ds(r0, c), :]
        v = v_ref[pl.ds(r0, c), :]
        scores = _dot_nt(q, k) * decay
        o = _dot(scores.astype(BF16), v)
        st = st_ref[...]
        o += _dot_nt((q.astype(F32) * q_decay).astype(BF16), st.astype(BF16))
        st_ref[...] = st * chunk_decay + _dot_tn(v, (k.astype(F32) * k_decay).astype(BF16))
        o = o * lax.rsqrt(jnp.mean(o * o, axis=-1, keepdims=True) + NORM_EPS)
        o_ref[pl.ds(r0, c), :] = (o * z_ref[pl.ds(r0, c), :].astype(F32)).astype(o_ref.dtype)
        return carry

    lax.fori_loop(0, seq // c, chunk, 0)


def _mixer_c(h, g, w_in, w_out, batch, seq, tabs, zero_aux):
    kinds = ([("rot128", 1.0)] * 8 + [("rot128", C_QK_DIM ** -0.5)] * 8 + [("none", 1.0)] * 16
             + [("silu", 1.0)] * 16)
    pc = _proj(h, g, w_in.astype(BF16), kinds, 1024, BF16, tabs, jnp.zeros((1, w_in.shape[1]), F32), seq)
    n = pc.shape[0]
    nh = C_HEADS
    log_gamma = jnp.log1p(-jnp.exp2(-5.0 - jnp.arange(nh, dtype=F32)))
    return pl.pallas_call(
        functools.partial(_retention_body, seq=seq),
        grid=(batch, nh),
        in_specs=[
            pl.BlockSpec(memory_space=pltpu.SMEM),
            pl.BlockSpec((seq, C_QK_DIM), lambda b, h: (b, h)),
            pl.BlockSpec((seq, C_QK_DIM), lambda b, h: (b, nh + h)),
            pl.BlockSpec((seq, C_V_DIM), lambda b, h: (b, nh + h)),
            pl.BlockSpec((seq, C_V_DIM), lambda b, h: (b, 2 * nh + h)),
        ],
        out_specs=pl.BlockSpec((seq, C_V_DIM), lambda b, h: (b, h)),
        out_shape=jax.ShapeDtypeStruct((n, D_INNER), BF16),
        scratch_shapes=[pltpu.VMEM((C_V_DIM, C_QK_DIM), F32)],
        compiler_params=_cparams(("parallel", "parallel")),
        name="retention",
    )(log_gamma, pc, pc, pc, pc)


def _hgrn_body(q_ref, g_ref, v_ref, z_ref, o_ref, st_ref, *, seq):
    c = D_CHUNK
    w = D_HEAD_GROUP * LANES
    nsub = c // D_SUB
    halves = [D_SUB * 2 ** lv for lv in range(16) if D_SUB * 2 ** (lv + 1) <= c]
    ri = lax.broadcasted_iota(jnp.int32, (c, c), 0)
    ci = lax.broadcasted_iota(jnp.int32, (c, c), 1)
    mats = [ri >= ci]
    pair_masks = []
    for half in halves:
        block = 2 * half
        mid = (ri // block) * block + half - 1
        right = (ri % block) >= half
        mats.append((right & (ci > mid) & (ci <= ri)) | (jnp.logical_not(right) & (ci > ri) & (ci <= mid)))
        pair_masks.append(((ri // block) == (ci // block)) & right & ((ci % block) < half))
    m1 = jnp.concatenate([m.astype(BF16) for m in mats], axis=0)
    m3 = jnp.concatenate([m1, m1, m1], axis=1)
    rows = lax.broadcasted_iota(jnp.int32, (c, w), 0)
    right_rows = [(rows % (2 * half)) >= half for half in halves]
    sub = rows % D_SUB
    st_ref[...] = jnp.zeros(st_ref.shape, F32)

    def bcast_row(x, j):
        x3 = x.reshape(nsub, D_SUB, w)
        return jnp.broadcast_to(x3[:, j:j + 1, :], (nsub, D_SUB, w)).reshape(c, w)

    head_sl = [slice(hd * LANES, (hd + 1) * LANES) for hd in range(D_HEAD_GROUP)]
    unroll = min(D_UNROLL, seq // c)

    def group(gidx, carry):
        rows0 = [pl.multiple_of((gidx * unroll + u) * c, c) for u in range(unroll)]
        gates = [g_ref[pl.ds(r0, c), :] for r0 in rows0]
        bds = []
        for gate in gates:
            lf = jnp.log(gate)
            hi = lf.astype(BF16)
            r1 = lf - hi.astype(F32)
            mid_part = r1.astype(BF16)
            lo = (r1 - mid_part.astype(F32)).astype(BF16)
            bds.append(_dot(m3, jnp.concatenate([hi, mid_part, lo], axis=0)))
        prep = []
        for u, r0 in enumerate(rows0):
            bd, kk = bds[u], 1.0 - gates[u]
            b = bd[0:c]
            q = q_ref[pl.ds(r0, c), :].astype(F32)
            v = v_ref[pl.ds(r0, c), :]
            vf = v.astype(F32)
            qe = (q * jnp.exp(b)).astype(BF16)
            qts, kts = [], []
            for lv in range(len(halves)):
                e = jnp.exp(bd[(lv + 1) * c:(lv + 2) * c])
                qts.append(jnp.where(right_rows[lv], q * e, 0.0).astype(BF16))
                kts.append(jnp.where(right_rows[lv], 0.0, kk * e).astype(BF16))
            o_diag = [jnp.zeros((c, LANES), F32) for _ in head_sl]
            for j in range(D_SUB):
                x = q * bcast_row(kk, j) * jnp.exp(b - bcast_row(b, j))
                if j:
                    x = jnp.where(sub >= j, x, 0.0)
                vs = bcast_row(vf, j)
                for hd, sl in enumerate(head_sl):
                    o_diag[hd] = o_diag[hd] + jnp.sum(x[:, sl], axis=1, keepdims=True) * vs[:, sl]
            b_last = b[c - 1:c, :]
            k_dec = (kk * jnp.exp(b_last - b)).astype(BF16)
            prep.append((qe, qts, kts, v, o_diag, k_dec, jnp.exp(b_last)))
        o_local = []
        for qe, qts, kts, v, o_diag, k_dec, eb_last in prep:
            per_head = []
            for hd, sl in enumerate(head_sl):
                a = jnp.zeros((c, c), F32)
                for lv in range(len(halves)):
                    a = a + jnp.where(pair_masks[lv], _dot_nt(qts[lv][:, sl], kts[lv][:, sl]), 0.0)
                per_head.append(_dot(a.astype(BF16), v[:, sl]) + o_diag[hd])
            o_local.append(per_head)
        outs = []
        for u, (qe, qts, kts, v, o_diag, k_dec, eb_last) in enumerate(prep):
            per_head = []
            for hd, sl in enumerate(head_sl):
                st = st_ref[hd]
                per_head.append(o_local[u][hd] + _dot_nt(qe[:, sl], st.astype(BF16)))
                st_ref[hd] = st * eb_last[:, sl] + _dot_tn(v[:, sl], k_dec[:, sl])
            outs.append(per_head)
        for u, r0 in enumerate(rows0):
            for hd, sl in enumerate(head_sl):
                o = outs[u][hd]
                o = o * lax.rsqrt(jnp.mean(o * o, axis=-1, keepdims=True) + NORM_EPS)
                o_ref[pl.ds(r0, c), sl] = (o * z_ref[pl.ds(r0, c), sl].astype(F32)).astype(o_ref.dtype)
        return carry

    lax.fori_loop(0, seq // (c * unroll), group, 0)


def _mixer_d(h, g, w_in, w_out, lower_bound, batch, seq, tabs, zero_aux):
    wq, wf, wi, wz = (w_in[:, t * D_INNER:(t + 1) * D_INNER] for t in range(4))
    w1 = jnp.concatenate([wq, wi, wz], axis=1).astype(BF16)
    kinds = [("silu", D_K_DIM ** -0.5)] * 16 + [("none", 1.0)] * 16 + [("silu", 1.0)] * 16
    pd = _proj(h, g, w1, kinds, 1024, BF16, tabs, jnp.zeros((1, 3 * D_INNER), F32), seq)
    gates = _proj(h, g, wf.astype(BF16), [("hgrn_g", 1.0)] * 16, 1024, F32, tabs,
                  lower_bound.reshape(1, D_INNER).astype(F32), seq)
    n = pd.shape[0]
    ng = D_HEADS // D_HEAD_GROUP
    w = D_HEAD_GROUP * LANES
    return pl.pallas_call(
        functools.partial(_hgrn_body, seq=seq),
        grid=(batch, ng),
        in_specs=[
            pl.BlockSpec((seq, w), lambda b, h: (b, h)),
            pl.BlockSpec((seq, w), lambda b, h: (b, h)),
            pl.BlockSpec((seq, w), lambda b, h: (b, ng + h)),
            pl.BlockSpec((seq, w), lambda b, h: (b, 2 * ng + h)),
        ],
        out_specs=pl.BlockSpec((seq, w), lambda b, h: (b, h)),
        out_shape=jax.ShapeDtypeStruct((n, D_INNER), BF16),
        scratch_shapes=[pltpu.VMEM((D_HEAD_GROUP, D_V_DIM, D_K_DIM), F32)],
        compiler_params=_cparams(("parallel", "parallel")),
        name="hgrn2_recurrence",
    )(pd, gates, pd, pd)


def kernel(x, norm_g, final_g, a_w_in, a_w_out, b_w_in, b_w_out, c_w_in, c_w_out, d_w_in, d_w_out, hgrn_lb_logits):
    batch, seq, d = x.shape
    depth = norm_g.shape[0]
    lb_cum = jnp.cumsum(jax.nn.softmax(hgrn_lb_logits.astype(F32), axis=0), axis=0)
    lower_bounds = lb_cum - lb_cum[0]
    tabs = _rope_tables(seq, 128) + _rope_tables(seq, 64)
    h = x.reshape(batch * seq, d)
    for layer in range(depth):
        kind, slot = layer % N_MIXERS, layer // N_MIXERS
        g = norm_g[layer]
        if kind == 0:
            og, w_out = _mixer_a(h, g, a_w_in[slot], a_w_out[slot], batch, seq, tabs, None), a_w_out[slot]
        elif kind == 1:
            og, w_out = _mixer_b(h, g, b_w_in[slot], b_w_out[slot], batch, seq, tabs, None), b_w_out[slot]
        elif kind == 2:
            og, w_out = _mixer_c(h, g, c_w_in[slot], c_w_out[slot], batch, seq, tabs, None), c_w_out[slot]
        else:
            og, w_out = _mixer_d(h, g, d_w_in[slot], d_w_out[slot], lower_bounds[layer], batch, seq, tabs, None), d_w_out[slot]
        h = _outproj(og, w_out.astype(BF16), h, final_g, final=(layer == depth - 1))
    return h.reshape(batch, seq, d)
```

```python
import functools

import jax
import jax.numpy as jnp
import numpy as np
from jax import lax
from jax.experimental import pallas as pl
from jax.experimental.pallas import tpu as pltpu

F32 = jnp.float32
BF16 = jnp.bfloat16

D_MODEL = 1024
D_INNER = 2048
N_MIXERS = 4
NORM_EPS = 1e-6
ROPE_THETA = 10000.0
LANES = 128
NEG_BIG = -1e30
LOG2E = 1.4426950408889634

A_HEADS, A_HEAD_DIM = 16, 128
A_IDX_HEADS, A_IDX_DIM = 16, 64
A_TOPK_MAX = 256
A_Q_BLOCK = 256
A_K_BLOCK = 512
A_HEAD_STAGE = 16
B_HEADS, B_HEAD_DIM = 16, 128
B_CONFIGS = ((128, 1), (512, 4), (2048, 16))
B_DIL_RATIO = 4
C_HEADS, C_QK_DIM, C_V_DIM, C_CHUNK = 8, 128, 256, 128
C_UNROLL = 4
D_HEADS, D_K_DIM, D_V_DIM = 16, 128, 128
D_CHUNK = 64
D_SUB = 8
D_HEAD_GROUP = 2
D_UNROLL = 2

PROJ_TILE_N = 1024
VMEM_LIMIT = 56 * 1024 * 1024


def _cparams(sem):
    return pltpu.CompilerParams(dimension_semantics=sem, vmem_limit_bytes=VMEM_LIMIT)


def _dot(a, b):
    return jnp.dot(a, b, preferred_element_type=F32)


def _dot_nt(a, b):
    return lax.dot_general(a, b, (((1,), (1,)), ((), ())), preferred_element_type=F32)


def _dot_tn(a, b):
    return lax.dot_general(a, b, (((0,), (0,)), ((), ())), preferred_element_type=F32)


def _rope_tables(seq, dh):
    freqs = ROPE_THETA ** (-jnp.arange(0, dh, 2, dtype=F32) / dh)
    ang = jnp.arange(seq, dtype=F32)[:, None] * freqs[None, :]
    cos, sin = jnp.cos(ang), jnp.sin(ang)
    cosf = jnp.concatenate([cos, cos], axis=-1)
    sinf = jnp.concatenate([-sin, sin], axis=-1)
    rep = LANES // dh
    return jnp.tile(cosf, (1, rep)), jnp.tile(sinf, (1, rep))


def _epilogue(kind, scale, x, tabs, aux):
    c128, s128, c64, s64 = tabs
    if kind == "rot128":
        y = x * c128 + pltpu.roll(x, 64, 1) * s128
    elif kind == "rot64":
        lane = lax.broadcasted_iota(jnp.int32, x.shape, 1)
        partner = jnp.where((lane % 64) < 32, pltpu.roll(x, 96, 1), pltpu.roll(x, 32, 1))
        y = x * c64 + partner * s64
    elif kind == "silu":
        y = x * jax.nn.sigmoid(x)
    elif kind == "hgrn_g":
        y = aux + (1.0 - aux) * jax.nn.sigmoid(x)
    else:
        y = x
    if scale is not None:
        y = y * scale
    return y


def _proj_body(h_ref, g_ref, w_ref, c128_ref, s128_ref, c64_ref, s64_ref, aux_ref, scale_ref, *rest,
               kinds, scaled, res_dils):
    o_ref = rest[0]
    res_refs = rest[1:1 + len(res_dils)]
    u_ref = rest[1 + len(res_dils)]
    tm = o_ref.shape[0]

    @pl.when(pl.program_id(1) == 0)
    def _norm():
        x = h_ref[...]
        ms = jnp.mean(x * x, axis=-1, keepdims=True)
        u_ref[...] = (x * lax.rsqrt(ms + NORM_EPS) * g_ref[...]).astype(BF16)

    tabs = (c128_ref[...] if "rot128" in kinds else None, s128_ref[...] if "rot128" in kinds else None,
            c64_ref[...] if "rot64" in kinds else None, s64_ref[...] if "rot64" in kinds else None)
    def pair_dot(p):
        return _dot(u_ref[...], w_ref[:, 2 * p * LANES:min(2 * p + 2, len(kinds)) * LANES])

    n_pairs = (len(kinds) + 1) // 2
    accs = {0: pair_dot(0)}
    for c, kind in enumerate(kinds):
        sl = slice(c * LANES, (c + 1) * LANES)
        aux = aux_ref[:, sl] if kind == "hgrn_g" else None
        if c % 2 == 0 and c // 2 + 1 < n_pairs:
            accs[c // 2 + 1] = pair_dot(c // 2 + 1)
        acc = accs[c // 2][:, (c % 2) * LANES:(c % 2 + 1) * LANES]
        y = _epilogue(kind, scale_ref[:, sl] if scaled else None, acc, tabs, aux)
        o_ref[:, sl] = y.astype(o_ref.dtype)
        if res_dils:
            y_ref = rest[-1]
            y_ref[c] = y
            for dil, r_ref in zip(res_dils, res_refs):
                for r in range(dil):
                    r_ref[r, :, sl] = y_ref[c, pl.ds(r, tm // dil, stride=dil), :].astype(r_ref.dtype)


def _proj(h, g, w, kinds, seq, tabs, *, scales=None, aux=None, out_dtype=None, res_dils=()):
    n, d = h.shape
    c = w.shape[1]
    tm = min(1024, seq)
    tn = min(PROJ_TILE_N, c)
    if isinstance(kinds, str):
        kinds = (kinds,) * (tn // LANES)
    assert len(kinds) == tn // LANES and (c == tn or len(set(kinds)) == 1)
    if scales is None:
        scale_row = jnp.zeros((1, c), F32)
    else:
        scale_row = jnp.repeat(jnp.asarray(scales, F32), LANES).reshape(1, c)
    if aux is None:
        aux = jnp.zeros((1, c), F32)
    out_dtype = out_dtype or BF16
    pos_blocks = seq // tm
    tab_spec = pl.BlockSpec((tm, LANES), lambda i, j: (i % pos_blocks, 0))
    row_spec = pl.BlockSpec((1, tn), lambda i, j: (0, j))
    out_specs = [pl.BlockSpec((tm, tn), lambda i, j: (i, j))]
    out_shape = [jax.ShapeDtypeStruct((n, c), out_dtype)]
    scratch = [pltpu.VMEM((tm, d), BF16)]
    for dil in res_dils:
        out_specs.append(pl.BlockSpec((dil, tm // dil, tn), lambda i, j: (0, i, j)))
        out_shape.append(jax.ShapeDtypeStruct((dil, n // dil, c), out_dtype))
    if res_dils:
        scratch.append(pltpu.VMEM((tn // LANES, tm, LANES), F32))
    res = pl.pallas_call(
        functools.partial(_proj_body, kinds=tuple(kinds), scaled=scales is not None, res_dils=tuple(res_dils)),
        grid=(n // tm, c // tn),
        in_specs=[
            pl.BlockSpec((tm, d), lambda i, j: (i, 0)),
            pl.BlockSpec((1, d), lambda i, j: (0, 0)),
            pl.BlockSpec((d, tn), lambda i, j: (0, j)),
            tab_spec, tab_spec, tab_spec, tab_spec,
            row_spec, row_spec,
        ],
        out_specs=out_specs,
        out_shape=out_shape,
        scratch_shapes=scratch,
        compiler_params=_cparams(("parallel", "arbitrary")),
        name="norm_in_proj",
    )(h, g.reshape(1, d), w, *tabs, aux, scale_row)
    return res if res_dils else res[0]


def _outproj_body(og_ref, w_ref, h_ref, gf_ref, o_ref, *, final):
    y = h_ref[...] + _dot(og_ref[...], w_ref[...])
    if final:
        ms = jnp.mean(y * y, axis=-1, keepdims=True)
        y = y * lax.rsqrt(ms + NORM_EPS) * gf_ref[...]
    o_ref[...] = y


def _outproj(og, w, h, gf, final):
    n, di = og.shape
    d = w.shape[1]
    tm = min(512, n)
    return pl.pallas_call(
        functools.partial(_outproj_body, final=final),
        grid=(n // tm,),
        in_specs=[
            pl.BlockSpec((tm, di), lambda i: (i, 0)),
            pl.BlockSpec((di, d), lambda i: (0, 0)),
            pl.BlockSpec((tm, d), lambda i: (i, 0)),
            pl.BlockSpec((1, d), lambda i: (0, 0)),
        ],
        out_specs=pl.BlockSpec((tm, d), lambda i: (i, 0)),
        out_shape=jax.ShapeDtypeStruct((n, d), F32),
        compiler_params=_cparams(("parallel",)),
        name="out_proj_residual",
    )(og, w, h, gf.reshape(1, d))


def _dsa_index_body(iq_ref, ik_ref, iw_ref, bias_ref, sc_ref, *, topk, tq, kc, seq):
    i = pl.program_id(1)
    n_chunks = ((i + 1) * tq + kc - 1) // kc
    int_min = jnp.int32(-2 ** 31)
    iw_t = iw_ref[...].astype(F32).T
    lane = lax.broadcasted_iota(jnp.int32, (kc, LANES), 1)
    s_loc = lax.broadcasted_iota(jnp.int32, (kc, tq), 0)
    t_pos = i * tq + lax.broadcasted_iota(jnp.int32, (kc, tq), 1)

    def score_chunk(c, carry):
        k0 = pl.multiple_of(c * kc, kc)
        ik2 = ik_ref[pl.ds(k0, kc), :]
        ik_lo = jnp.where(lane < A_IDX_DIM, ik2, jnp.zeros_like(ik2))
        ik_hi = jnp.where(lane >= A_IDX_DIM, ik2, jnp.zeros_like(ik2))
        score = jnp.zeros((kc, tq), F32)
        for p in range(A_IDX_HEADS // 2):
            iq2 = iq_ref[:, p * LANES:(p + 1) * LANES]
            score += jnp.maximum(_dot_nt(ik_lo, iq2), 0.0) * iw_t[2 * p:2 * p + 1, :]
            score += jnp.maximum(_dot_nt(ik_hi, iq2), 0.0) * iw_t[2 * p + 1:2 * p + 2, :]
        sc_ref[pl.ds(k0, kc), :] = jnp.where(k0 + s_loc <= t_pos, score, -jnp.inf)
        return carry

    lax.fori_loop(0, n_chunks, score_chunk, 0)

    def threshold(prefix):
        key = prefix ^ int_min
        return pltpu.bitcast(jnp.where(key < 0, key ^ jnp.int32(0x7FFFFFFF), key), F32)

    def bisect(it, prefix):
        cand = prefix | (jnp.int32(1) << (31 - it))
        thr = threshold(cand)

        def count(c, cnt):
            blk = sc_ref[pl.ds(pl.multiple_of(c * kc, kc), kc), :]
            return cnt + jnp.sum(jnp.where(blk >= thr, 1.0, 0.0), axis=0, keepdims=True)

        cnt = lax.fori_loop(0, n_chunks, count, jnp.zeros((1, tq), F32))
        return jnp.where(cnt >= topk, cand, prefix)

    n_iter = jnp.where((i + 1) * tq <= topk, 0, 32)
    prefix = lax.fori_loop(0, n_iter, bisect, jnp.zeros((1, tq), jnp.int32))
    thr = threshold(prefix)
    take_all = prefix == 0

    def emit(c, carry):
        k0 = pl.multiple_of(c * kc, kc)
        sel = ((sc_ref[pl.ds(k0, kc), :] >= thr) | take_all) & (k0 + s_loc <= t_pos)
        bias_ref[pl.ds(k0, kc), :] = jnp.where(sel, 0.0, NEG_BIG).astype(bias_ref.dtype)
        return carry

    lax.fori_loop(0, n_chunks, emit, 0)

    def emit_masked(c, carry):
        bias_ref[pl.ds(pl.multiple_of(c * kc, kc), kc), :] = jnp.full((kc, tq), NEG_BIG, bias_ref.dtype)
        return carry

    lax.fori_loop(n_chunks, seq // kc, emit_masked, 0)


def _dsa_index(iq, kvi, batch, seq, topk):
    tq = min(A_Q_BLOCK, seq)
    kc = min(A_K_BLOCK, seq)
    nq = seq // tq
    return pl.pallas_call(
        functools.partial(_dsa_index_body, topk=topk, tq=tq, kc=kc, seq=seq),
        grid=(batch, nq),
        in_specs=[
            pl.BlockSpec((tq, A_IDX_HEADS * A_IDX_DIM), lambda b, i: (b * nq + i, 0)),
            pl.BlockSpec((seq, LANES), lambda b, i: (b, 2)),
            pl.BlockSpec((tq, LANES), lambda b, i: (b * nq + i, 3)),
        ],
        out_specs=pl.BlockSpec((None, None, seq, tq), lambda b, i: (b, i, 0, 0)),
        out_shape=jax.ShapeDtypeStruct((batch, nq, seq, tq), BF16),
        scratch_shapes=[pltpu.VMEM((seq, tq), F32)],
        compiler_params=_cparams(("parallel", "parallel")),
        name="dsa_indexer_topk_mask",
    )(iq, kvi, kvi)


def _dsa_attn_body(q_ref, z_ref, k_ref, v_ref, bias_ref, o_ref, m_ref, l_ref, acc_ref, *, tq, kb):
    i = pl.program_id(1)
    j = pl.program_id(2)
    nh = A_HEADS
    dv = A_HEAD_DIM

    @pl.when(j == 0)
    def _init():
        m_ref[...] = jnp.full(m_ref.shape, NEG_BIG, F32)
        l_ref[...] = jnp.zeros(l_ref.shape, F32)
        acc_ref[...] = jnp.zeros(acc_ref.shape, F32)

    @pl.when(j * kb <= i * tq + tq - 1)
    def _step():
        k = k_ref[...]
        v_t = v_ref[...].astype(F32).T.astype(BF16)
        bias = bias_ref[...].astype(F32)
        for h0 in range(0, nh, A_HEAD_STAGE):
            hs = range(h0, h0 + A_HEAD_STAGE)
            s = [_dot_nt(k, q_ref[:, h * LANES:(h + 1) * LANES]) + bias for h in hs]
            ps, alphas = [], []
            for n, h in enumerate(hs):
                m_old = m_ref[h:h + 1, :]
                m_new = jnp.maximum(m_old, jnp.max(s[n], axis=0, keepdims=True))
                alpha = jnp.exp2(m_old - m_new)
                p = jnp.exp2(s[n] - m_new)
                l_ref[h:h + 1, :] = alpha * l_ref[h:h + 1, :] + jnp.sum(p, axis=0, keepdims=True)
                m_ref[h:h + 1, :] = m_new
                ps.append(p.astype(BF16))
                alphas.append(alpha)
            for n, h in enumerate(hs):
                rows = slice(h * dv, (h + 1) * dv)
                acc_ref[rows, :] = alphas[n] * acc_ref[rows, :] + _dot(v_t, ps[n])

    @pl.when(j == pl.num_programs(2) - 1)
    def _fin():
        for h in range(nh):
            sl = slice(h * LANES, (h + 1) * LANES)
            o_t = acc_ref[h * dv:(h + 1) * dv, :] * (1.0 / l_ref[h:h + 1, :])
            o_ref[:, sl] = (o_t.T * z_ref[:, sl].astype(F32)).astype(o_ref.dtype)


def _dsa_attn(q, z, kvi, bias, batch, seq):
    n = q.shape[0]
    tq = bias.shape[-1]
    kb = min(A_K_BLOCK, seq)
    nq, nk = seq // tq, seq // kb

    def kj(i, j):
        return jnp.minimum(j, (i * tq + tq - 1) // kb)

    return pl.pallas_call(
        functools.partial(_dsa_attn_body, tq=tq, kb=kb),
        grid=(batch, nq, nk),
        in_specs=[
            pl.BlockSpec((tq, D_INNER), lambda b, i, j: (b * nq + i, 0)),
            pl.BlockSpec((tq, D_INNER), lambda b, i, j: (b * nq + i, 0)),
            pl.BlockSpec((kb, LANES), lambda b, i, j: (b * nk + kj(i, j), 0)),
            pl.BlockSpec((kb, LANES), lambda b, i, j: (b * nk + kj(i, j), 1)),
            pl.BlockSpec((None, None, kb, tq), lambda b, i, j: (b, i, kj(i, j), 0)),
        ],
        out_specs=pl.BlockSpec((tq, D_INNER), lambda b, i, j: (b * nq + i, 0)),
        out_shape=jax.ShapeDtypeStruct((n, D_INNER), BF16),
        scratch_shapes=[
            pltpu.VMEM((A_HEADS, tq), F32),
            pltpu.VMEM((A_HEADS, tq), F32),
            pltpu.VMEM((A_HEADS * A_HEAD_DIM, tq), F32),
        ],
        compiler_params=_cparams(("parallel", "parallel", "arbitrary")),
        name="dsa_masked_attention",
    )(q, z, kvi, kvi, bias)


def _mixer_a(h, g, w_in, w_out, batch, seq, tabs, zero_aux):
    hd = A_HEADS * A_HEAD_DIM
    hi = A_IDX_HEADS * A_IDX_DIM
    o0 = np.cumsum([0, hd, A_HEAD_DIM, A_HEAD_DIM, hi, A_IDX_DIM, A_IDX_HEADS, D_INNER])
    wq, wk, wv, wiq, wik, wiw, wz = (w_in[:, o0[t]:o0[t + 1]] for t in range(7))
    pad = jnp.zeros((D_MODEL, LANES - A_IDX_HEADS), F32)
    w_kvi = jnp.concatenate([wk, wv, wik, wik, wiw, pad], axis=1).astype(BF16)
    q = _proj(h, g, wq.astype(BF16), "rot128", seq, tabs, scales=[A_HEAD_DIM ** -0.5 * LOG2E] * A_HEADS)
    z = _proj(h, g, wz.astype(BF16), "silu", seq, tabs)
    iq = _proj(h, g, wiq.astype(BF16), "rot64", seq, tabs)
    kvi = _proj(h, g, w_kvi, ["rot128", "none", "rot64", "none"], seq, tabs,
                scales=[1.0, 1.0, 1.0, A_IDX_HEADS ** -0.5 * A_IDX_DIM ** -0.5])
    bias = _dsa_index(iq, kvi, batch, seq, min(A_TOPK_MAX, seq // 4))
    return _dsa_attn(q, z, kvi, bias, batch, seq)


def _band_body(*refs, blk, band_prev, cfg_prev, final):
    refs = list(refs)
    q_ref, kc_ref, vc_ref = refs[:3]
    del refs[:3]
    if band_prev:
        kp_ref, vp_ref = refs[:2]
        del refs[:2]
    if cfg_prev:
        op4_ref, lp4_ref = refs[:2]
        del refs[:2]
        op_ref, lp_ref = refs[-2:]
        del refs[-2:]
        for a in range(B_DIL_RATIO):
            rows = pl.ds(a, blk // B_DIL_RATIO, stride=B_DIL_RATIO)
            lp_ref[rows, :] = lp4_ref[a]
            for h in range(B_HEADS):
                op_ref[h, rows, :] = op4_ref[a, :, h * LANES:(h + 1) * LANES].astype(F32)
    if final:
        z_ref, o_ref = refs
    else:
        o_ref, l_ref = refs
    ki = lax.broadcasted_iota(jnp.int32, (blk, blk), 0)
    qi = lax.broadcasted_iota(jnp.int32, (blk, blk), 1)
    mask_cur = ki <= qi
    eye = ki == qi
    if band_prev:
        mask_prev = (ki >= qi) & (pl.program_id(2) > 0)
    if cfg_prev:
        lse_prev_t = _pad_rows(lp_ref[...].T, LANES)
    heads = [slice(h * LANES, (h + 1) * LANES) for h in range(B_HEADS)]
    s_c = [jnp.where(mask_cur, _dot_nt(kc_ref[:, sl], q_ref[:, sl]), NEG_BIG) for sl in heads]
    if band_prev:
        s_p = [jnp.where(mask_prev, _dot_nt(kp_ref[:, sl], q_ref[:, sl]), NEG_BIG) for sl in heads]
    lse_rows, p_c, p_p, d_old = [], [], [], []
    for h in range(B_HEADS):
        m = jnp.max(s_c[h], axis=0, keepdims=True)
        if band_prev:
            m = jnp.maximum(m, jnp.max(s_p[h], axis=0, keepdims=True))
        e_c = jnp.exp(s_c[h] - m)
        den = jnp.sum(e_c, axis=0, keepdims=True)
        if band_prev:
            e_p = jnp.exp(s_p[h] - m)
            den += jnp.sum(e_p, axis=0, keepdims=True)
        lse = m + jnp.log(den)
        scale = 1.0 / den
        if cfg_prev:
            lse_prev = lse_prev_t[h:h + 1, :]
            m2 = jnp.maximum(lse, lse_prev)
            w_new, w_old = jnp.exp(lse - m2), jnp.exp(lse_prev - m2)
            scale = scale * w_new / (w_new + w_old)
            d_old.append(jnp.where(eye, w_old / (w_new + w_old), 0.0).astype(BF16))
            lse = m2 + jnp.log(w_new + w_old)
        p_c.append((e_c * scale).astype(BF16))
        if band_prev:
            p_p.append((e_p * scale).astype(BF16))
        lse_rows.append(lse)
    for h, sl in enumerate(heads):
        o = _dot_tn(p_c[h], vc_ref[:, sl])
        if band_prev:
            o += _dot_tn(p_p[h], vp_ref[:, sl])
        if cfg_prev:
            o += _dot(d_old[h], op_ref[h].astype(BF16))
        if final:
            o_ref[:, sl] = (o * z_ref[:, sl].astype(F32)).astype(o_ref.dtype)
        else:
            o_ref[:, sl] = o.astype(o_ref.dtype)
    if not final:
        lse_t = _pad_rows(jnp.concatenate(lse_rows, axis=0), LANES)
        l_ref[...] = lse_t.T


def _pad_rows(x, rows):
    if x.shape[0] == rows:
        return x
    return jnp.concatenate([x, jnp.zeros((rows - x.shape[0], x.shape[1]), x.dtype)], axis=0)


def _band_attn(qk, v, gate, batch, seq, dil, prev):
    final = gate is not None
    n = qk.shape[0] * qk.shape[1]
    length = seq // dil
    blk = min(128, length)
    nbl = length // blk
    band_prev = nbl > 1

    def spec(col, prev_blk=False):
        if prev_blk:
            return pl.BlockSpec((None, blk, D_INNER), lambda b, r, i: (r, b * nbl + jnp.maximum(i - 1, 0), col))
        return pl.BlockSpec((None, blk, D_INNER), lambda b, r, i: (r, b * nbl + i, col))

    o_spec = pl.BlockSpec((None, blk, D_INNER), lambda b, r, i: (r, b * nbl + i, 0))
    l_spec = pl.BlockSpec((None, blk, LANES), lambda b, r, i: (r, b * nbl + i, 0))
    in_specs, args, scratch = [spec(0), spec(1), spec(0)], [qk, qk, v], []
    if band_prev:
        in_specs += [spec(1, True), spec(0, True)]
        args += [qk, v]
    if prev is not None:
        coarse = B_DIL_RATIO * dil
        assert prev[0].shape[0] == coarse and blk % (16 * B_DIL_RATIO) == 0
        pblk = blk // B_DIL_RATIO
        in_specs += [pl.BlockSpec((B_DIL_RATIO, None, pblk, D_INNER), lambda b, r, i: (0, r, b * nbl + i, 0)),
                     pl.BlockSpec((B_DIL_RATIO, None, pblk, LANES), lambda b, r, i: (0, r, b * nbl + i, 0))]
        args += [prev[0].reshape(B_DIL_RATIO, dil, n // coarse, D_INNER),
                 prev[1].reshape(B_DIL_RATIO, dil, n // coarse, LANES)]
        scratch = [pltpu.VMEM((B_HEADS, blk, LANES), F32), pltpu.VMEM((blk, LANES), F32)]
    o_shape = jax.ShapeDtypeStruct((dil, n // dil, D_INNER), BF16)
    if final:
        in_specs.append(spec(0))
        args.append(gate)
        out_specs, out_shape = o_spec, o_shape
    else:
        out_specs = [o_spec, l_spec]
        out_shape = [o_shape, jax.ShapeDtypeStruct((dil, n // dil, LANES), F32)]
    return pl.pallas_call(
        functools.partial(_band_body, blk=blk, band_prev=band_prev, cfg_prev=prev is not None, final=final),
        grid=(batch, dil, nbl),
        in_specs=in_specs,
        out_specs=out_specs,
        out_shape=out_shape,
        scratch_shapes=scratch,
        compiler_params=_cparams(("parallel", "parallel", "parallel")),
        name="dilated_band_attention",
    )(*args)


def _mixer_b(h, g, w_in, w_out, batch, seq, tabs, zero_aux):
    configs = sorted(B_CONFIGS, key=lambda wd: -wd[1])
    dils = [dil for _, dil in configs]
    assert dils[-1] == 1 and all(a == B_DIL_RATIO * b for a, b in zip(dils, dils[1:]))
    assert all(window // dil == 128 and seq % dil == 0 for window, dil in configs)
    w = w_in.astype(BF16)
    qk = _proj(h, g, w[:, :2 * D_INNER], "rot128", seq, tabs, res_dils=dils[:-1],
               scales=[B_HEAD_DIM ** -0.5] * B_HEADS + [1.0] * B_HEADS)
    v = _proj(h, g, w[:, 2 * D_INNER:3 * D_INNER], "none", seq, tabs, res_dils=dils[:-1])
    z = _proj(h, g, w[:, 3 * D_INNER:], "silu", seq, tabs)
    qks = list(qk[1:]) + [qk[0][None]]
    vs = list(v[1:]) + [v[0][None]]
    prev = None
    for qk_d, v_d, dil in zip(qks, vs, dils):
        prev = _band_attn(qk_d, v_d, z[None] if dil == 1 else None, batch, seq, dil, prev)
    return prev.reshape(-1, D_INNER)


def _retention_body(lg_ref, q_ref, k_ref, v_ref, z_ref, o_ref, st_ref, *, seq):
    c = C_CHUNK
    lg = lg_ref[pl.program_id(1)]
    ri = lax.broadcasted_iota(jnp.int32, (c, c), 0)
    ci = lax.broadcasted_iota(jnp.int32, (c, c), 1)
    rel = (ri - ci).astype(F32)
    decay = jnp.where(rel >= 0, jnp.exp(jnp.maximum(rel, 0.0) * lg), 0.0)
    idx = lax.broadcasted_iota(jnp.int32, (c, 1), 0).astype(F32)
    k_decay = jnp.exp((c - 1 - idx) * lg)
    q_decay = jnp.exp((idx + 1.0) * lg)
    chunk_decay = jnp.exp(jnp.full((1, C_QK_DIM), c, F32) * lg)
    st_ref[...] = jnp.zeros(st_ref.shape, F32)

    unroll = min(C_UNROLL, seq // c)

    def group(gidx, carry):
        rows0 = [pl.multiple_of((gidx * unroll + u) * c, c) for u in range(unroll)]
        qs = [q_ref[pl.ds(r0, c), :] for r0 in rows0]
        ks = [k_ref[pl.ds(r0, c), :] for r0 in rows0]
        vs = [v_ref[pl.ds(r0, c), :] for r0 in rows0]
        scores = [(_dot_nt(q, k) * decay).astype(BF16) for q, k in zip(qs, ks)]
        o_intra = [_dot(s, v) for s, v in zip(scores, vs)]
        q_dec = [(q.astype(F32) * q_decay).astype(BF16) for q in qs]
        k_dec = [(k.astype(F32) * k_decay).astype(BF16) for k in ks]
        outs = []
        for u in range(unroll):
            st = st_ref[...]
            outs.append(o_intra[u] + _dot_nt(q_dec[u], st.astype(BF16)))
            st_ref[...] = st * chunk_decay + _dot_tn(vs[u], k_dec[u])
        for u, r0 in enumerate(rows0):
            o = outs[u]
            o = o * lax.rsqrt(jnp.mean(o * o, axis=-1, keepdims=True) + NORM_EPS)
            o_ref[pl.ds(r0, c), :] = (o * z_ref[pl.ds(r0, c), :].astype(F32)).astype(o_ref.dtype)
        return carry

    lax.fori_loop(0, seq // (c * unroll), group, 0)


def _mixer_c(h, g, w_in, w_out, batch, seq, tabs, zero_aux):
    w = w_in.astype(BF16)
    nh = C_HEADS
    dqk = 2 * nh * C_QK_DIM
    qk = _proj(h, g, w[:, :dqk], "rot128", seq, tabs, scales=[1.0] * nh + [C_QK_DIM ** -0.5] * nh)
    v = _proj(h, g, w[:, dqk:dqk + D_INNER], "none", seq, tabs)
    z = _proj(h, g, w[:, dqk + D_INNER:], "silu", seq, tabs)
    n = qk.shape[0]
    log_gamma = jnp.log1p(-jnp.exp2(-5.0 - jnp.arange(nh, dtype=F32)))
    return pl.pallas_call(
        functools.partial(_retention_body, seq=seq),
        grid=(batch, nh),
        in_specs=[
            pl.BlockSpec(memory_space=pltpu.SMEM),
            pl.BlockSpec((seq, C_QK_DIM), lambda b, h: (b, h)),
            pl.BlockSpec((seq, C_QK_DIM), lambda b, h: (b, nh + h)),
            pl.BlockSpec((seq, C_V_DIM), lambda b, h: (b, h)),
            pl.BlockSpec((seq, C_V_DIM), lambda b, h: (b, h)),
        ],
        out_specs=pl.BlockSpec((seq, C_V_DIM), lambda b, h: (b, h)),
        out_shape=jax.ShapeDtypeStruct((n, D_INNER), BF16),
        scratch_shapes=[pltpu.VMEM((C_V_DIM, C_QK_DIM), F32)],
        compiler_params=_cparams(("parallel", "parallel")),
        name="retention",
    )(log_gamma, qk, qk, v, z)


def _hgrn_body(q_ref, g_ref, v_ref, z_ref, o_ref, st_ref, *, seq):
    c = D_CHUNK
    w = D_HEAD_GROUP * LANES
    nsub = c // D_SUB
    halves = [D_SUB * 2 ** lv for lv in range(16) if D_SUB * 2 ** (lv + 1) <= c]
    ri = lax.broadcasted_iota(jnp.int32, (c, c), 0)
    ci = lax.broadcasted_iota(jnp.int32, (c, c), 1)
    mats = [ri >= ci]
    pair_masks = []
    for half in halves:
        block = 2 * half
        mid = (ri // block) * block + half - 1
        right = (ri % block) >= half
        mats.append((right & (ci > mid) & (ci <= ri)) | (jnp.logical_not(right) & (ci > ri) & (ci <= mid)))
        pair_masks.append(((ri // block) == (ci // block)) & right & ((ci % block) < half))
    m1 = jnp.concatenate([m.astype(BF16) for m in mats], axis=0)
    m3 = jnp.concatenate([m1, m1, m1], axis=1)
    rows = lax.broadcasted_iota(jnp.int32, (c, w), 0)
    right_rows = [(rows % (2 * half)) >= half for half in halves]
    sub = rows % D_SUB
    st_ref[...] = jnp.zeros(st_ref.shape, F32)

    def bcast_row(x, j):
        x3 = x.reshape(nsub, D_SUB, w)
        return jnp.broadcast_to(x3[:, j:j + 1, :], (nsub, D_SUB, w)).reshape(c, w)

    head_sl = [slice(hd * LANES, (hd + 1) * LANES) for hd in range(D_HEAD_GROUP)]
    unroll = min(D_UNROLL, seq // c)

    def group(gidx, carry):
        rows0 = [pl.multiple_of((gidx * unroll + u) * c, c) for u in range(unroll)]
        gates = [g_ref[pl.ds(r0, c), :] for r0 in rows0]
        bds = []
        for gate in gates:
            lf = jnp.log(gate)
            hi = lf.astype(BF16)
            r1 = lf - hi.astype(F32)
            mid_part = r1.astype(BF16)
            lo = (r1 - mid_part.astype(F32)).astype(BF16)
            bds.append(_dot(m3, jnp.concatenate([hi, mid_part, lo], axis=0)))
        prep = []
        for u, r0 in enumerate(rows0):
            bd, kk = bds[u], 1.0 - gates[u]
            b = bd[0:c]
            q = q_ref[pl.ds(r0, c), :].astype(F32)
            v = v_ref[pl.ds(r0, c), :]
            vf = v.astype(F32)
            qe = (q * jnp.exp(b)).astype(BF16)
            qts, kts = [], []
            for lv in range(len(halves)):
                e = jnp.exp(bd[(lv + 1) * c:(lv + 2) * c])
                qts.append(jnp.where(right_rows[lv], q * e, 0.0).astype(BF16))
                kts.append(jnp.where(right_rows[lv], 0.0, kk * e).astype(BF16))
            o_diag = [jnp.zeros((c, LANES), F32) for _ in head_sl]
            for j in range(D_SUB):
                x = q * bcast_row(kk, j) * jnp.exp(b - bcast_row(b, j))
                if j:
                    x = jnp.where(sub >= j, x, 0.0)
                vs = bcast_row(vf, j)
                for hd, sl in enumerate(head_sl):
                    o_diag[hd] = o_diag[hd] + jnp.sum(x[:, sl], axis=1, keepdims=True) * vs[:, sl]
            b_last = b[c - 1:c, :]
            k_dec = (kk * jnp.exp(b_last - b)).astype(BF16)
            prep.append((qe, qts, kts, v, o_diag, k_dec, jnp.exp(b_last)))
        o_local = []
        for qe, qts, kts, v, o_diag, k_dec, eb_last in prep:
            per_head = []
            for hd, sl in enumerate(head_sl):
                a = jnp.zeros((c, c), F32)
                for lv in range(len(halves)):
                    a = a + jnp.where(pair_masks[lv], _dot_nt(qts[lv][:, sl], kts[lv][:, sl]), 0.0)
                per_head.append(_dot(a.astype(BF16), v[:, sl]) + o_diag[hd])
            o_local.append(per_head)
        outs = []
        for u, (qe, qts, kts, v, o_diag, k_dec, eb_last) in enumerate(prep):
            per_head = []
            for hd, sl in enumerate(head_sl):
                st = st_ref[hd]
                per_head.append(o_local[u][hd] + _dot_nt(qe[:, sl], st.astype(BF16)))
                st_ref[hd] = st * eb_last[:, sl] + _dot_tn(v[:, sl], k_dec[:, sl])
            outs.append(per_head)
        for u, r0 in enumerate(rows0):
            for hd, sl in enumerate(head_sl):
                o = outs[u][hd]
                o = o * lax.rsqrt(jnp.mean(o * o, axis=-1, keepdims=True) + NORM_EPS)
                o_ref[pl.ds(r0, c), sl] = (o * z_ref[pl.ds(r0, c), sl].astype(F32)).astype(o_ref.dtype)
        return carry

    lax.fori_loop(0, seq // (c * unroll), group, 0)


def _mixer_d(h, g, w_in, w_out, lower_bound, batch, seq, tabs, zero_aux):
    wq, wf, wi, wz = (w_in[:, t * D_INNER:(t + 1) * D_INNER] for t in range(4))
    qz = _proj(h, g, jnp.concatenate([wq, wz], axis=1).astype(BF16), "silu", seq, tabs,
               scales=[D_K_DIM ** -0.5] * D_HEADS + [1.0] * D_HEADS)
    v = _proj(h, g, wi.astype(BF16), "none", seq, tabs)
    gates = _proj(h, g, wf.astype(BF16), "hgrn_g", seq, tabs, out_dtype=F32,
                  aux=lower_bound.reshape(1, D_INNER).astype(F32))
    n = qz.shape[0]
    ng = D_HEADS // D_HEAD_GROUP
    w = D_HEAD_GROUP * LANES
    return pl.pallas_call(
        functools.partial(_hgrn_body, seq=seq),
        grid=(batch, ng),
        in_specs=[
            pl.BlockSpec((seq, w), lambda b, h: (b, h)),
            pl.BlockSpec((seq, w), lambda b, h: (b, h)),
            pl.BlockSpec((seq, w), lambda b, h: (b, h)),
            pl.BlockSpec((seq, w), lambda b, h: (b, ng + h)),
        ],
        out_specs=pl.BlockSpec((seq, w), lambda b, h: (b, h)),
        out_shape=jax.ShapeDtypeStruct((n, D_INNER), BF16),
        scratch_shapes=[pltpu.VMEM((D_HEAD_GROUP, D_V_DIM, D_K_DIM), F32)],
        compiler_params=_cparams(("parallel", "parallel")),
        name="hgrn2_recurrence",
    )(qz, gates, v, qz)


def kernel(x, norm_g, final_g, a_w_in, a_w_out, b_w_in, b_w_out, c_w_in, c_w_out, d_w_in, d_w_out, hgrn_lb_logits):
    batch, seq, d = x.shape
    depth = norm_g.shape[0]
    lb_cum = jnp.cumsum(jax.nn.softmax(hgrn_lb_logits.astype(F32), axis=0), axis=0)
    lower_bounds = lb_cum - lb_cum[0]
    tabs = _rope_tables(seq, 128) + _rope_tables(seq, 64)
    h = x.reshape(batch * seq, d)
    for layer in range(depth):
        kind, slot = layer % N_MIXERS, layer // N_MIXERS
        g = norm_g[layer]
        if kind == 0:
            og, w_out = _mixer_a(h, g, a_w_in[slot], a_w_out[slot], batch, seq, tabs, None), a_w_out[slot]
        elif kind == 1:
            og, w_out = _mixer_b(h, g, b_w_in[slot], b_w_out[slot], batch, seq, tabs, None), b_w_out[slot]
        elif kind == 2:
            og, w_out = _mixer_c(h, g, c_w_in[slot], c_w_out[slot], batch, seq, tabs, None), c_w_out[slot]
        else:
            og, w_out = _mixer_d(h, g, d_w_in[slot], d_w_out[slot], lower_bounds[layer], batch, seq, tabs, None), d_w_out[slot]
        h = _outproj(og, w_out.astype(BF16), h, final_g, final=(layer == depth - 1))
    return h.reshape(batch, seq, d)
```

```python
import functools

import jax
import jax.numpy as jnp
import numpy as np
from jax import lax
from jax.experimental import pallas as pl
from jax.experimental.pallas import tpu as pltpu

F32 = jnp.float32
BF16 = jnp.bfloat16

D_MODEL = 1024
D_INNER = 2048
N_MIXERS = 4
NORM_EPS = 1e-6
ROPE_THETA = 10000.0
LANES = 128
NEG_BIG = -1e30
LOG2E = 1.4426950408889634

A_HEADS, A_HEAD_DIM = 16, 128
A_IDX_HEADS, A_IDX_DIM = 16, 64
A_TOPK_MAX = 256
A_Q_BLOCK = 256
A_K_BLOCK = 512
A_CNT_ROWS = 64
A_HEAD_STAGE = 16
B_HEADS, B_HEAD_DIM = 16, 128
B_CONFIGS = ((128, 1), (512, 4), (2048, 16))
B_DIL_RATIO = 4
C_HEADS, C_QK_DIM, C_V_DIM, C_CHUNK = 8, 128, 256, 128
C_UNROLL = 4
D_HEADS, D_K_DIM, D_V_DIM = 16, 128, 128
D_CHUNK = 64
D_SUB = 8
D_HEAD_GROUP = 2
D_UNROLL = 4

PROJ_TILE_N = 2048
PROJ_TILE_N_RES = 1024
VMEM_LIMIT = 56 * 1024 * 1024


def _cparams(sem):
    return pltpu.CompilerParams(dimension_semantics=sem, vmem_limit_bytes=VMEM_LIMIT)


def _dot(a, b):
    return jnp.dot(a, b, preferred_element_type=F32)


def _dot_nt(a, b):
    return lax.dot_general(a, b, (((1,), (1,)), ((), ())), preferred_element_type=F32)


def _dot_tn(a, b):
    return lax.dot_general(a, b, (((0,), (0,)), ((), ())), preferred_element_type=F32)


def _rope_tables(seq, dh):
    freqs = ROPE_THETA ** (-jnp.arange(0, dh, 2, dtype=F32) / dh)
    ang = jnp.arange(seq, dtype=F32)[:, None] * freqs[None, :]
    cos, sin = jnp.cos(ang), jnp.sin(ang)
    cosf = jnp.concatenate([cos, cos], axis=-1)
    sinf = jnp.concatenate([-sin, sin], axis=-1)
    rep = LANES // dh
    return jnp.tile(cosf, (1, rep)), jnp.tile(sinf, (1, rep))


def _epilogue(kind, scale, x, tabs, aux):
    c128, s128, c64, s64 = tabs
    if kind == "rot128":
        y = x * c128 + pltpu.roll(x, 64, 1) * s128
    elif kind == "rot64":
        lane = lax.broadcasted_iota(jnp.int32, x.shape, 1)
        partner = jnp.where((lane % 64) < 32, pltpu.roll(x, 96, 1), pltpu.roll(x, 32, 1))
        y = x * c64 + partner * s64
    elif kind == "silu":
        y = x * jax.nn.sigmoid(x)
    elif kind == "hgrn_g":
        y = aux + (1.0 - aux) * jax.nn.sigmoid(x)
    else:
        y = x
    if scale is not None:
        y = y * scale
    return y


def _proj_body(h_ref, g_ref, w_ref, c128_ref, s128_ref, c64_ref, s64_ref, aux_ref, scale_ref, *rest,
               kinds, scaled, res_dils):
    o_ref = rest[0]
    res_refs = rest[1:1 + len(res_dils)]
    u_ref = rest[1 + len(res_dils)]
    tm = o_ref.shape[0]

    @pl.when(pl.program_id(1) == 0)
    def _norm():
        x = h_ref[...]
        ms = jnp.mean(x * x, axis=-1, keepdims=True)
        u_ref[...] = (x * lax.rsqrt(ms + NORM_EPS) * g_ref[...]).astype(BF16)

    tabs = (c128_ref[...] if "rot128" in kinds else None, s128_ref[...] if "rot128" in kinds else None,
            c64_ref[...] if "rot64" in kinds else None, s64_ref[...] if "rot64" in kinds else None)
    def pair_dot(p):
        return _dot(u_ref[...], w_ref[:, 2 * p * LANES:min(2 * p + 2, len(kinds)) * LANES])

    n_pairs = (len(kinds) + 1) // 2
    accs = {0: pair_dot(0)}
    for c, kind in enumerate(kinds):
        sl = slice(c * LANES, (c + 1) * LANES)
        aux = aux_ref[:, sl] if kind == "hgrn_g" else None
        if c % 2 == 0 and c // 2 + 1 < n_pairs:
            accs[c // 2 + 1] = pair_dot(c // 2 + 1)
        acc = accs[c // 2][:, (c % 2) * LANES:(c % 2 + 1) * LANES]
        y = _epilogue(kind, scale_ref[:, sl] if scaled else None, acc, tabs, aux)
        o_ref[:, sl] = y.astype(o_ref.dtype)
        if res_dils:
            y_ref = rest[-1]
            y_ref[c] = y
            for dil, r_ref in zip(res_dils, res_refs):
                for r in range(dil):
                    r_ref[r, :, sl] = y_ref[c, pl.ds(r, tm // dil, stride=dil), :].astype(r_ref.dtype)


def _proj(h, g, w, kinds, seq, tabs, *, scales=None, aux=None, out_dtype=None, res_dils=()):
    n, d = h.shape
    c = w.shape[1]
    tm = min(1024, seq)
    tn = min(PROJ_TILE_N_RES if res_dils else PROJ_TILE_N, c)
    if isinstance(kinds, str):
        kinds = (kinds,) * (tn // LANES)
    assert len(kinds) == tn // LANES and (c == tn or len(set(kinds)) == 1)
    if scales is None:
        scale_row = jnp.zeros((1, c), F32)
    else:
        scale_row = jnp.repeat(jnp.asarray(scales, F32), LANES).reshape(1, c)
    if aux is None:
        aux = jnp.zeros((1, c), F32)
    out_dtype = out_dtype or BF16
    pos_blocks = seq // tm
    tab_spec = pl.BlockSpec((tm, LANES), lambda i, j: (i % pos_blocks, 0))
    row_spec = pl.BlockSpec((1, tn), lambda i, j: (0, j))
    out_specs = [pl.BlockSpec((tm, tn), lambda i, j: (i, j))]
    out_shape = [jax.ShapeDtypeStruct((n, c), out_dtype)]
    scratch = [pltpu.VMEM((tm, d), BF16)]
    for dil in res_dils:
        out_specs.append(pl.BlockSpec((dil, tm // dil, tn), lambda i, j: (0, i, j)))
        out_shape.append(jax.ShapeDtypeStruct((dil, n // dil, c), out_dtype))
    if res_dils:
        scratch.append(pltpu.VMEM((tn // LANES, tm, LANES), F32))
    res = pl.pallas_call(
        functools.partial(_proj_body, kinds=tuple(kinds), scaled=scales is not None, res_dils=tuple(res_dils)),
        grid=(n // tm, c // tn),
        in_specs=[
            pl.BlockSpec((tm, d), lambda i, j: (i, 0)),
            pl.BlockSpec((1, d), lambda i, j: (0, 0)),
            pl.BlockSpec((d, tn), lambda i, j: (0, j)),
            tab_spec, tab_spec, tab_spec, tab_spec,
            row_spec, row_spec,
        ],
        out_specs=out_specs,
        out_shape=out_shape,
        scratch_shapes=scratch,
        compiler_params=_cparams(("parallel", "arbitrary")),
        name="norm_in_proj",
    )(h, g.reshape(1, d), w, *tabs, aux, scale_row)
    return res if res_dils else res[0]


def _outproj_body(og_ref, w_ref, h_ref, gf_ref, o_ref, *, final):
    y = h_ref[...] + _dot(og_ref[...], w_ref[...])
    if final:
        ms = jnp.mean(y * y, axis=-1, keepdims=True)
        y = y * lax.rsqrt(ms + NORM_EPS) * gf_ref[...]
    o_ref[...] = y


def _outproj(og, w, h, gf, final):
    n, di = og.shape
    d = w.shape[1]
    tm = min(1024, n)
    return pl.pallas_call(
        functools.partial(_outproj_body, final=final),
        grid=(n // tm,),
        in_specs=[
            pl.BlockSpec((tm, di), lambda i: (i, 0)),
            pl.BlockSpec((di, d), lambda i: (0, 0)),
            pl.BlockSpec((tm, d), lambda i: (i, 0)),
            pl.BlockSpec((1, d), lambda i: (0, 0)),
        ],
        out_specs=pl.BlockSpec((tm, d), lambda i: (i, 0)),
        out_shape=jax.ShapeDtypeStruct((n, d), F32),
        compiler_params=_cparams(("parallel",)),
        name="out_proj_residual",
    )(og, w, h, gf.reshape(1, d))


def _dsa_index_body(iq_ref, ik_ref, iw_ref, bias_ref, sc_ref, *, topk, tq, kc, seq):
    i = pl.program_id(1)
    n_chunks = ((i + 1) * tq + kc - 1) // kc
    int_min = jnp.int32(-2 ** 31)
    iw_t = iw_ref[...].astype(F32).T
    lane = lax.broadcasted_iota(jnp.int32, (kc, LANES), 1)
    s_loc = lax.broadcasted_iota(jnp.int32, (kc, tq), 0)
    t_pos = i * tq + lax.broadcasted_iota(jnp.int32, (kc, tq), 1)

    def score_chunk(c, carry):
        k0 = pl.multiple_of(c * kc, kc)
        ik2 = ik_ref[pl.ds(k0, kc), :]
        ik_lo = jnp.where(lane < A_IDX_DIM, ik2, jnp.zeros_like(ik2))
        ik_hi = jnp.where(lane >= A_IDX_DIM, ik2, jnp.zeros_like(ik2))
        score = jnp.zeros((kc, tq), F32)
        for p in range(A_IDX_HEADS // 2):
            iq2 = iq_ref[:, p * LANES:(p + 1) * LANES]
            score += jnp.maximum(_dot_nt(ik_lo, iq2), 0.0) * iw_t[2 * p:2 * p + 1, :]
            score += jnp.maximum(_dot_nt(ik_hi, iq2), 0.0) * iw_t[2 * p + 1:2 * p + 2, :]
        sc_ref[pl.ds(k0, kc), :] = jnp.where(k0 + s_loc <= t_pos, score, -jnp.inf)
        return carry

    lax.fori_loop(0, n_chunks, score_chunk, 0)

    def threshold(prefix):
        key = prefix ^ int_min
        return pltpu.bitcast(jnp.where(key < 0, key ^ jnp.int32(0x7FFFFFFF), key), F32)

    def bisect(it, prefix):
        cand = prefix | (jnp.int32(1) << (31 - it))
        thr = threshold(cand)

        def count(c, part):
            blk = sc_ref[pl.ds(pl.multiple_of(c * kc, kc), kc), :]
            hits = jnp.where(blk >= thr, 1.0, 0.0).reshape(kc // A_CNT_ROWS, A_CNT_ROWS, tq)
            return part + jnp.sum(hits, axis=0)

        part = lax.fori_loop(0, n_chunks, count, jnp.zeros((A_CNT_ROWS, tq), F32))
        cnt = jnp.sum(part, axis=0, keepdims=True)
        return jnp.where(cnt >= topk, cand, prefix)

    n_iter = jnp.where((i + 1) * tq <= topk, 0, 32)
    prefix = lax.fori_loop(0, n_iter, bisect, jnp.zeros((1, tq), jnp.int32))
    thr = threshold(prefix)
    take_all = prefix == 0

    def emit(c, carry):
        k0 = pl.multiple_of(c * kc, kc)
        sel = ((sc_ref[pl.ds(k0, kc), :] >= thr) | take_all) & (k0 + s_loc <= t_pos)
        bias_ref[pl.ds(k0, kc), :] = jnp.where(sel, 0.0, NEG_BIG).astype(bias_ref.dtype)
        return carry

    lax.fori_loop(0, n_chunks, emit, 0)

    def emit_masked(c, carry):
        bias_ref[pl.ds(pl.multiple_of(c * kc, kc), kc), :] = jnp.full((kc, tq), NEG_BIG, bias_ref.dtype)
        return carry

    lax.fori_loop(n_chunks, seq // kc, emit_masked, 0)


def _dsa_index(iq, kvi, batch, seq, topk):
    tq = min(A_Q_BLOCK, seq)
    kc = min(A_K_BLOCK, seq)
    nq = seq // tq
    return pl.pallas_call(
        functools.partial(_dsa_index_body, topk=topk, tq=tq, kc=kc, seq=seq),
        grid=(batch, nq),
        in_specs=[
            pl.BlockSpec((tq, A_IDX_HEADS * A_IDX_DIM), lambda b, i: (b * nq + i, 0)),
            pl.BlockSpec((seq, LANES), lambda b, i: (b, 2)),
            pl.BlockSpec((tq, LANES), lambda b, i: (b * nq + i, 3)),
        ],
        out_specs=pl.BlockSpec((None, None, seq, tq), lambda b, i: (b, i, 0, 0)),
        out_shape=jax.ShapeDtypeStruct((batch, nq, seq, tq), BF16),
        scratch_shapes=[pltpu.VMEM((seq, tq), F32)],
        compiler_params=_cparams(("parallel", "parallel")),
        name="dsa_indexer_topk_mask",
    )(iq, kvi, kvi)


def _dsa_attn_body(q_ref, z_ref, k_ref, v_ref, bias_ref, o_ref, m_ref, l_ref, acc_ref, *, tq, kb):
    i = pl.program_id(1)
    j = pl.program_id(2)
    nh = A_HEADS
    dv = A_HEAD_DIM

    @pl.when(j == 0)
    def _init():
        m_ref[...] = jnp.full(m_ref.shape, NEG_BIG, F32)
        l_ref[...] = jnp.zeros(l_ref.shape, F32)
        acc_ref[...] = jnp.zeros(acc_ref.shape, F32)

    @pl.when(j * kb <= i * tq + tq - 1)
    def _step():
        k = k_ref[...]
        v_t = v_ref[...].astype(F32).T.astype(BF16)
        bias = bias_ref[...].astype(F32)
        for h0 in range(0, nh, A_HEAD_STAGE):
            hs = range(h0, h0 + A_HEAD_STAGE)
            s = [_dot_nt(k, q_ref[:, h * LANES:(h + 1) * LANES]) + bias for h in hs]
            ps, alphas = [], []
            for n, h in enumerate(hs):
                m_old = m_ref[h:h + 1, :]
                m_new = jnp.maximum(m_old, jnp.max(s[n], axis=0, keepdims=True))
                alpha = jnp.exp2(m_old - m_new)
                p = jnp.exp2(s[n] - m_new)
                l_ref[h:h + 1, :] = alpha * l_ref[h:h + 1, :] + jnp.sum(p, axis=0, keepdims=True)
                m_ref[h:h + 1, :] = m_new
                ps.append(p.astype(BF16))
                alphas.append(alpha)
            for n, h in enumerate(hs):
                rows = slice(h * dv, (h + 1) * dv)
                acc_ref[rows, :] = alphas[n] * acc_ref[rows, :] + _dot(v_t, ps[n])

    @pl.when(j == pl.num_programs(2) - 1)
    def _fin():
        for h in range(nh):
            sl = slice(h * LANES, (h + 1) * LANES)
            o_t = acc_ref[h * dv:(h + 1) * dv, :] * (1.0 / l_ref[h:h + 1, :])
            o_ref[:, sl] = (o_t.T * z_ref[:, sl].astype(F32)).astype(o_ref.dtype)


def _dsa_attn(q, z, kvi, bias, batch, seq):
    n = q.shape[0]
    tq = bias.shape[-1]
    kb = min(A_K_BLOCK, seq)
    nq, nk = seq // tq, seq // kb

    def kj(i, j):
        return jnp.minimum(j, (i * tq + tq - 1) // kb)

    return pl.pallas_call(
        functools.partial(_dsa_attn_body, tq=tq, kb=kb),
        grid=(batch, nq, nk),
        in_specs=[
            pl.BlockSpec((tq, D_INNER), lambda b, i, j: (b * nq + i, 0)),
            pl.BlockSpec((tq, D_INNER), lambda b, i, j: (b * nq + i, 0)),
            pl.BlockSpec((kb, LANES), lambda b, i, j: (b * nk + kj(i, j), 0)),
            pl.BlockSpec((kb, LANES), lambda b, i, j: (b * nk + kj(i, j), 1)),
            pl.BlockSpec((None, None, kb, tq), lambda b, i, j: (b, i, kj(i, j), 0)),
        ],
        out_specs=pl.BlockSpec((tq, D_INNER), lambda b, i, j: (b * nq + i, 0)),
        out_shape=jax.ShapeDtypeStruct((n, D_INNER), BF16),
        scratch_shapes=[
            pltpu.VMEM((A_HEADS, tq), F32),
            pltpu.VMEM((A_HEADS, tq), F32),
            pltpu.VMEM((A_HEADS * A_HEAD_DIM, tq), F32),
        ],
        compiler_params=_cparams(("parallel", "parallel", "arbitrary")),
        name="dsa_masked_attention",
    )(q, z, kvi, kvi, bias)


def _mixer_a(h, g, w_in, w_out, batch, seq, tabs, zero_aux):
    hd = A_HEADS * A_HEAD_DIM
    hi = A_IDX_HEADS * A_IDX_DIM
    o0 = np.cumsum([0, hd, A_HEAD_DIM, A_HEAD_DIM, hi, A_IDX_DIM, A_IDX_HEADS, D_INNER])
    wq, wk, wv, wiq, wik, wiw, wz = (w_in[:, o0[t]:o0[t + 1]] for t in range(7))
    pad = jnp.zeros((D_MODEL, LANES - A_IDX_HEADS), F32)
    w_kvi = jnp.concatenate([wk, wv, wik, wik, wiw, pad], axis=1).astype(BF16)
    q = _proj(h, g, wq.astype(BF16), "rot128", seq, tabs, scales=[A_HEAD_DIM ** -0.5 * LOG2E] * A_HEADS)
    z = _proj(h, g, wz.astype(BF16), "silu", seq, tabs)
    iq = _proj(h, g, wiq.astype(BF16), "rot64", seq, tabs)
    kvi = _proj(h, g, w_kvi, ["rot128", "none", "rot64", "none"], seq, tabs,
                scales=[1.0, 1.0, 1.0, A_IDX_HEADS ** -0.5 * A_IDX_DIM ** -0.5])
    bias = _dsa_index(iq, kvi, batch, seq, min(A_TOPK_MAX, seq // 4))
    return _dsa_attn(q, z, kvi, bias, batch, seq)


def _band_body(*refs, blk, band_prev, cfg_prev, final):
    refs = list(refs)
    q_ref, kc_ref, vc_ref = refs[:3]
    del refs[:3]
    if band_prev:
        kp_ref, vp_ref = refs[-2:]
        del refs[-2:]

        @pl.when(pl.program_id(2) == 0)
        def _no_prev():
            kp_ref[...] = jnp.zeros(kp_ref.shape, kp_ref.dtype)
            vp_ref[...] = jnp.zeros(vp_ref.shape, vp_ref.dtype)
    if cfg_prev:
        op4_ref, lp4_ref = refs[:2]
        del refs[:2]
        op_ref, lp_ref = refs[-2:]
        del refs[-2:]
        for a in range(B_DIL_RATIO):
            rows = pl.ds(a, blk // B_DIL_RATIO, stride=B_DIL_RATIO)
            lp_ref[rows, :] = lp4_ref[a]
            for h in range(B_HEADS):
                op_ref[h, rows, :] = op4_ref[a, :, h * LANES:(h + 1) * LANES].astype(F32)
    if final:
        z_ref, o_ref = refs
    else:
        o_ref, l_ref = refs
    ki = lax.broadcasted_iota(jnp.int32, (blk, blk), 0)
    qi = lax.broadcasted_iota(jnp.int32, (blk, blk), 1)
    mask_cur = ki <= qi
    eye = ki == qi
    if band_prev:
        mask_prev = (ki >= qi) & (pl.program_id(2) > 0)
    if cfg_prev:
        lse_prev_t = _pad_rows(lp_ref[...].T, LANES)
    heads = [slice(h * LANES, (h + 1) * LANES) for h in range(B_HEADS)]
    s_c = [jnp.where(mask_cur, _dot_nt(kc_ref[:, sl], q_ref[:, sl]), NEG_BIG) for sl in heads]
    if band_prev:
        s_p = [jnp.where(mask_prev, _dot_nt(kp_ref[:, sl], q_ref[:, sl]), NEG_BIG) for sl in heads]
    lse_rows, p_c, p_p, d_old = [], [], [], []
    for h in range(B_HEADS):
        m = jnp.max(s_c[h], axis=0, keepdims=True)
        if band_prev:
            m = jnp.maximum(m, jnp.max(s_p[h], axis=0, keepdims=True))
        e_c = jnp.exp(s_c[h] - m)
        den = jnp.sum(e_c, axis=0, keepdims=True)
        if band_prev:
            e_p = jnp.exp(s_p[h] - m)
            den += jnp.sum(e_p, axis=0, keepdims=True)
        lse = m + jnp.log(den)
        scale = 1.0 / den
        if cfg_prev:
            lse_prev = lse_prev_t[h:h + 1, :]
            m2 = jnp.maximum(lse, lse_prev)
            w_new, w_old = jnp.exp(lse - m2), jnp.exp(lse_prev - m2)
            scale = scale * w_new / (w_new + w_old)
            d_old.append(jnp.where(eye, w_old / (w_new + w_old), 0.0).astype(BF16))
            lse = m2 + jnp.log(w_new + w_old)
        p_c.append((e_c * scale).astype(BF16))
        if band_prev:
            p_p.append((e_p * scale).astype(BF16))
        lse_rows.append(lse)
    for h, sl in enumerate(heads):
        o = _dot_tn(p_c[h], vc_ref[:, sl])
        if band_prev:
            o += _dot_tn(p_p[h], vp_ref[:, sl])
        if cfg_prev:
            o += _dot(d_old[h], op_ref[h].astype(BF16))
        if final:
            o_ref[:, sl] = (o * z_ref[:, sl].astype(F32)).astype(o_ref.dtype)
        else:
            o_ref[:, sl] = o.astype(o_ref.dtype)
    if not final:
        lse_t = _pad_rows(jnp.concatenate(lse_rows, axis=0), LANES)
        l_ref[...] = lse_t.T
    if band_prev:
        kp_ref[...] = kc_ref[...]
        vp_ref[...] = vc_ref[...]


def _pad_rows(x, rows):
    if x.shape[0] == rows:
        return x
    return jnp.concatenate([x, jnp.zeros((rows - x.shape[0], x.shape[1]), x.dtype)], axis=0)


def _band_attn(qk, v, gate, batch, seq, dil, prev):
    final = gate is not None
    n = qk.shape[0] * qk.shape[1]
    length = seq // dil
    blk = min(128, length)
    nbl = length // blk
    band_prev = nbl > 1

    def spec(col):
        return pl.BlockSpec((None, blk, D_INNER), lambda b, r, i: (r, b * nbl + i, col))

    o_spec = pl.BlockSpec((None, blk, D_INNER), lambda b, r, i: (r, b * nbl + i, 0))
    l_spec = pl.BlockSpec((None, blk, LANES), lambda b, r, i: (r, b * nbl + i, 0))
    in_specs, args, scratch = [spec(0), spec(1), spec(0)], [qk, qk, v], []
    if prev is not None:
        coarse = B_DIL_RATIO * dil
        assert prev[0].shape[0] == coarse and blk % (16 * B_DIL_RATIO) == 0
        pblk = blk // B_DIL_RATIO
        in_specs += [pl.BlockSpec((B_DIL_RATIO, None, pblk, D_INNER), lambda b, r, i: (0, r, b * nbl + i, 0)),
                     pl.BlockSpec((B_DIL_RATIO, None, pblk, LANES), lambda b, r, i: (0, r, b * nbl + i, 0))]
        args += [prev[0].reshape(B_DIL_RATIO, dil, n // coarse, D_INNER),
                 prev[1].reshape(B_DIL_RATIO, dil, n // coarse, LANES)]
        scratch = [pltpu.VMEM((B_HEADS, blk, LANES), F32), pltpu.VMEM((blk, LANES), F32)]
    if band_prev:
        scratch += [pltpu.VMEM((blk, D_INNER), BF16), pltpu.VMEM((blk, D_INNER), BF16)]
    o_shape = jax.ShapeDtypeStruct((dil, n // dil, D_INNER), BF16)
    if final:
        in_specs.append(spec(0))
        args.append(gate)
        out_specs, out_shape = o_spec, o_shape
    else:
        out_specs = [o_spec, l_spec]
        out_shape = [o_shape, jax.ShapeDtypeStruct((dil, n // dil, LANES), F32)]
    return pl.pallas_call(
        functools.partial(_band_body, blk=blk, band_prev=band_prev, cfg_prev=prev is not None, final=final),
        grid=(batch, dil, nbl),
        in_specs=in_specs,
        out_specs=out_specs,
        out_shape=out_shape,
        scratch_shapes=scratch,
        compiler_params=_cparams(("parallel", "parallel", "arbitrary")),
        name="dilated_band_attention",
    )(*args)


def _mixer_b(h, g, w_in, w_out, batch, seq, tabs, zero_aux):
    configs = sorted(B_CONFIGS, key=lambda wd: -wd[1])
    dils = [dil for _, dil in configs]
    assert dils[-1] == 1 and all(a == B_DIL_RATIO * b for a, b in zip(dils, dils[1:]))
    assert all(window // dil == 128 and seq % dil == 0 for window, dil in configs)
    w = w_in.astype(BF16)
    qk = _proj(h, g, w[:, :2 * D_INNER], "rot128", seq, tabs, res_dils=dils[:-1],
               scales=[B_HEAD_DIM ** -0.5] * B_HEADS + [1.0] * B_HEADS)
    v = _proj(h, g, w[:, 2 * D_INNER:3 * D_INNER], "none", seq, tabs, res_dils=dils[:-1])
    z = _proj(h, g, w[:, 3 * D_INNER:], "silu", seq, tabs)
    qks = list(qk[1:]) + [qk[0][None]]
    vs = list(v[1:]) + [v[0][None]]
    prev = None
    for qk_d, v_d, dil in zip(qks, vs, dils):
        prev = _band_attn(qk_d, v_d, z[None] if dil == 1 else None, batch, seq, dil, prev)
    return prev.reshape(-1, D_INNER)


def _retention_body(lg_ref, q_ref, k_ref, v_ref, z_ref, o_ref, st_ref, *, seq):
    c = C_CHUNK
    lg = lg_ref[pl.program_id(1)]
    ri = lax.broadcasted_iota(jnp.int32, (c, c), 0)
    ci = lax.broadcasted_iota(jnp.int32, (c, c), 1)
    rel = (ri - ci).astype(F32)
    decay = jnp.where(rel >= 0, jnp.exp(jnp.maximum(rel, 0.0) * lg), 0.0)
    idx = lax.broadcasted_iota(jnp.int32, (c, 1), 0).astype(F32)
    k_decay = jnp.exp((c - 1 - idx) * lg)
    q_decay = jnp.exp((idx + 1.0) * lg)
    chunk_decay = jnp.exp(jnp.full((1, C_QK_DIM), c, F32) * lg)
    st_ref[...] = jnp.zeros(st_ref.shape, F32)

    unroll = min(C_UNROLL, seq // c)

    def group(gidx, carry):
        rows0 = [pl.multiple_of((gidx * unroll + u) * c, c) for u in range(unroll)]
        qs = [q_ref[pl.ds(r0, c), :] for r0 in rows0]
        ks = [k_ref[pl.ds(r0, c), :] for r0 in rows0]
        vs = [v_ref[pl.ds(r0, c), :] for r0 in rows0]
        scores = [(_dot_nt(q, k) * decay).astype(BF16) for q, k in zip(qs, ks)]
        o_intra = [_dot(s, v) for s, v in zip(scores, vs)]
        q_dec = [(q.astype(F32) * q_decay).astype(BF16) for q in qs]
        k_dec = [(k.astype(F32) * k_decay).astype(BF16) for k in ks]
        outs = []
        for u in range(unroll):
            st = st_ref[...]
            outs.append(o_intra[u] + _dot_nt(q_dec[u], st.astype(BF16)))
            st_ref[...] = st * chunk_decay + _dot_tn(vs[u], k_dec[u])
        for u, r0 in enumerate(rows0):
            o = outs[u]
            o = o * lax.rsqrt(jnp.mean(o * o, axis=-1, keepdims=True) + NORM_EPS)
            o_ref[pl.ds(r0, c), :] = (o * z_ref[pl.ds(r0, c), :].astype(F32)).astype(o_ref.dtype)
        return carry

    lax.fori_loop(0, seq // (c * unroll), group, 0)


def _mixer_c(h, g, w_in, w_out, batch, seq, tabs, zero_aux):
    w = w_in.astype(BF16)
    nh = C_HEADS
    dqk = 2 * nh * C_QK_DIM
    qk = _proj(h, g, w[:, :dqk], "rot128", seq, tabs, scales=[1.0] * nh + [C_QK_DIM ** -0.5] * nh)
    v = _proj(h, g, w[:, dqk:dqk + D_INNER], "none", seq, tabs)
    z = _proj(h, g, w[:, dqk + D_INNER:], "silu", seq, tabs)
    n = qk.shape[0]
    log_gamma = jnp.log1p(-jnp.exp2(-5.0 - jnp.arange(nh, dtype=F32)))
    return pl.pallas_call(
        functools.partial(_retention_body, seq=seq),
        grid=(batch, nh),
        in_specs=[
            pl.BlockSpec(memory_space=pltpu.SMEM),
            pl.BlockSpec((seq, C_QK_DIM), lambda b, h: (b, h)),
            pl.BlockSpec((seq, C_QK_DIM), lambda b, h: (b, nh + h)),
            pl.BlockSpec((seq, C_V_DIM), lambda b, h: (b, h)),
            pl.BlockSpec((seq, C_V_DIM), lambda b, h: (b, h)),
        ],
        out_specs=pl.BlockSpec((seq, C_V_DIM), lambda b, h: (b, h)),
        out_shape=jax.ShapeDtypeStruct((n, D_INNER), BF16),
        scratch_shapes=[pltpu.VMEM((C_V_DIM, C_QK_DIM), F32)],
        compiler_params=_cparams(("parallel", "parallel")),
        name="retention",
    )(log_gamma, qk, qk, v, z)


def _hgrn_body(q_ref, g_ref, v_ref, z_ref, o_ref, st_ref, *, seq):
    c = D_CHUNK
    w = D_HEAD_GROUP * LANES
    halves = [D_SUB * 2 ** lv for lv in range(16) if D_SUB * 2 ** (lv + 1) <= c]
    ri = lax.broadcasted_iota(jnp.int32, (c, c), 0)
    ci = lax.broadcasted_iota(jnp.int32, (c, c), 1)
    mats = [ri >= ci]
    pair_masks = []
    for half in halves:
        block = 2 * half
        mid = (ri // block) * block + half - 1
        right = (ri % block) >= half
        mats.append((right & (ci > mid) & (ci <= ri)) | (jnp.logical_not(right) & (ci > ri) & (ci <= mid)))
        pair_masks.append(((ri // block) == (ci // block)) & right & ((ci % block) < half))
    diag_masks = []
    for j in range(D_SUB):
        key = (ri // D_SUB) * D_SUB + j
        mats.append((ci > key) & (ci <= ri))
        diag_masks.append((ci == key) & (ri >= key))
    m1 = jnp.concatenate([m.astype(BF16) for m in mats], axis=0)
    m3 = jnp.concatenate([m1, m1, m1], axis=1)
    rows = lax.broadcasted_iota(jnp.int32, (c, w), 0)
    right_rows = [(rows % (2 * half)) >= half for half in halves]
    n_lv = len(halves)
    st_ref[...] = jnp.zeros(st_ref.shape, F32)

    head_sl = [slice(hd * LANES, (hd + 1) * LANES) for hd in range(D_HEAD_GROUP)]
    unroll = min(D_UNROLL, seq // c)

    def group(gidx, carry):
        rows0 = [pl.multiple_of((gidx * unroll + u) * c, c) for u in range(unroll)]
        gates = [g_ref[pl.ds(r0, c), :] for r0 in rows0]
        bds = []
        for gate in gates:
            lf = jnp.log(gate)
            hi = lf.astype(BF16)
            r1 = lf - hi.astype(F32)
            mid_part = r1.astype(BF16)
            lo = (r1 - mid_part.astype(F32)).astype(BF16)
            bds.append(_dot(m3, jnp.concatenate([hi, mid_part, lo], axis=0)))
        prep = []
        for u, r0 in enumerate(rows0):
            bd, kk = bds[u], 1.0 - gates[u]
            b = bd[0:c]
            q = q_ref[pl.ds(r0, c), :].astype(F32)
            v = v_ref[pl.ds(r0, c), :]
            qe = (q * jnp.exp(b)).astype(BF16)
            qts, kts = [], []
            for lv in range(len(halves)):
                e = jnp.exp(bd[(lv + 1) * c:(lv + 2) * c])
                qts.append(jnp.where(right_rows[lv], q * e, 0.0).astype(BF16))
                kts.append(jnp.where(right_rows[lv], 0.0, kk * e).astype(BF16))
            q_diag = jnp.concatenate(
                [(q * jnp.exp(bd[(1 + n_lv + j) * c:(2 + n_lv + j) * c])).astype(BF16) for j in range(D_SUB)], axis=0)
            b_last = b[c - 1:c, :]
            k_dec = (kk * jnp.exp(b_last - b)).astype(BF16)
            prep.append((qe, qts, kts, v, (q_diag, kk.astype(BF16)), k_dec, jnp.exp(b_last)))
        o_local = []
        for qe, qts, kts, v, (q_diag, kk16), k_dec, eb_last in prep:
            per_head = []
            for hd, sl in enumerate(head_sl):
                a = jnp.zeros((c, c), F32)
                for lv in range(n_lv):
                    a = a + jnp.where(pair_masks[lv], _dot_nt(qts[lv][:, sl], kts[lv][:, sl]), 0.0)
                g_diag = _dot_nt(q_diag[:, sl], kk16[:, sl])
                for j in range(D_SUB):
                    a = a + jnp.where(diag_masks[j], g_diag[j * c:(j + 1) * c], 0.0)
                per_head.append(_dot(a.astype(BF16), v[:, sl]))
            o_local.append(per_head)
        outs = []
        for u, (qe, qts, kts, v, _, k_dec, eb_last) in enumerate(prep):
            per_head = []
            for hd, sl in enumerate(head_sl):
                st = st_ref[hd]
                per_head.append(o_local[u][hd] + _dot_nt(qe[:, sl], st.astype(BF16)))
                st_ref[hd] = st * eb_last[:, sl] + _dot_tn(v[:, sl], k_dec[:, sl])
            outs.append(per_head)
        for u, r0 in enumerate(rows0):
            for hd, sl in enumerate(head_sl):
                o = outs[u][hd]
                o = o * lax.rsqrt(jnp.mean(o * o, axis=-1, keepdims=True) + NORM_EPS)
                o_ref[pl.ds(r0, c), sl] = (o * z_ref[pl.ds(r0, c), sl].astype(F32)).astype(o_ref.dtype)
        return carry

    lax.fori_loop(0, seq // (c * unroll), group, 0)


def _mixer_d(h, g, w_in, w_out, lower_bound, batch, seq, tabs, zero_aux):
    wq, wf, wi, wz = (w_in[:, t * D_INNER:(t + 1) * D_INNER] for t in range(4))
    qz = _proj(h, g, jnp.concatenate([wq, wz], axis=1).astype(BF16), "silu", seq, tabs,
               scales=[D_K_DIM ** -0.5] * D_HEADS + [1.0] * D_HEADS)
    v = _proj(h, g, wi.astype(BF16), "none", seq, tabs)
    gates = _proj(h, g, wf.astype(BF16), "hgrn_g", seq, tabs, out_dtype=F32,
                  aux=lower_bound.reshape(1, D_INNER).astype(F32))
    n = qz.shape[0]
    ng = D_HEADS // D_HEAD_GROUP
    w = D_HEAD_GROUP * LANES
    return pl.pallas_call(
        functools.partial(_hgrn_body, seq=seq),
        grid=(batch, ng),
        in_specs=[
            pl.BlockSpec((seq, w), lambda b, h: (b, h)),
            pl.BlockSpec((seq, w), lambda b, h: (b, h)),
            pl.BlockSpec((seq, w), lambda b, h: (b, h)),
            pl.BlockSpec((seq, w), lambda b, h: (b, ng + h)),
        ],
        out_specs=pl.BlockSpec((seq, w), lambda b, h: (b, h)),
        out_shape=jax.ShapeDtypeStruct((n, D_INNER), BF16),
        scratch_shapes=[pltpu.VMEM((D_HEAD_GROUP, D_V_DIM, D_K_DIM), F32)],
        compiler_params=_cparams(("parallel", "parallel")),
        name="hgrn2_recurrence",
    )(qz, gates, v, qz)


def kernel(x, norm_g, final_g, a_w_in, a_w_out, b_w_in, b_w_out, c_w_in, c_w_out, d_w_in, d_w_out, hgrn_lb_logits):
    batch, seq, d = x.shape
    depth = norm_g.shape[0]
    lb_cum = jnp.cumsum(jax.nn.softmax(hgrn_lb_logits.astype(F32), axis=0), axis=0)
    lower_bounds = lb_cum - lb_cum[0]
    tabs = _rope_tables(seq, 128) + _rope_tables(seq, 64)
    h = x.reshape(batch * seq, d)
    for layer in range(depth):
        kind, slot = layer % N_MIXERS, layer // N_MIXERS
        g = norm_g[layer]
        if kind == 0:
            og, w_out = _mixer_a(h, g, a_w_in[slot], a_w_out[slot], batch, seq, tabs, None), a_w_out[slot]
        elif kind == 1:
            og, w_out = _mixer_b(h, g, b_w_in[slot], b_w_out[slot], batch, seq, tabs, None), b_w_out[slot]
        elif kind == 2:
            og, w_out = _mixer_c(h, g, c_w_in[slot], c_w_out[slot], batch, seq, tabs, None), c_w_out[slot]
        else:
            og, w_out = _mixer_d(h, g, d_w_in[slot], d_w_out[slot], lower_bounds[layer], batch, seq, tabs, None), d_w_out[slot]
        h = _outproj(og, w_out.astype(BF16), h, final_g, final=(layer == depth - 1))
    return h.reshape(batch, seq, d)
```

```python
import functools

import jax
import jax.numpy as jnp
import numpy as np
from jax import lax
from jax.experimental import pallas as pl
from jax.experimental.pallas import tpu as pltpu

F32 = jnp.float32
BF16 = jnp.bfloat16

D_MODEL = 1024
D_INNER = 2048
N_MIXERS = 4
NORM_EPS = 1e-6
ROPE_THETA = 10000.0
LANES = 128
NEG_BIG = -1e30
LOG2E = 1.4426950408889634

A_HEADS, A_HEAD_DIM = 16, 128
A_IDX_HEADS, A_IDX_DIM = 16, 64
A_TOPK_MAX = 256
A_Q_BLOCK = 256
A_K_BLOCK = 512
A_CNT_ROWS = 64
A_HEAD_STAGE = 16
B_HEADS, B_HEAD_DIM = 16, 128
B_CONFIGS = ((128, 1), (512, 4), (2048, 16))
B_DIL_RATIO = 4
C_HEADS, C_QK_DIM, C_V_DIM, C_CHUNK = 8, 128, 256, 128
C_UNROLL = 4
D_HEADS, D_K_DIM, D_V_DIM = 16, 128, 128
D_CHUNK = 64
D_SUB = 2
D_HEAD_GROUP = 2
D_UNROLL = 4

PROJ_TILE_N = 2048
PROJ_TILE_N_RES = 1024
VMEM_LIMIT = 56 * 1024 * 1024


def _cparams(sem):
    return pltpu.CompilerParams(dimension_semantics=sem, vmem_limit_bytes=VMEM_LIMIT)


def _dot(a, b):
    return jnp.dot(a, b, preferred_element_type=F32)


def _dot_nt(a, b):
    return lax.dot_general(a, b, (((1,), (1,)), ((), ())), preferred_element_type=F32)


def _dot_tn(a, b):
    return lax.dot_general(a, b, (((0,), (0,)), ((), ())), preferred_element_type=F32)


def _rope_tables(seq, dh):
    freqs = ROPE_THETA ** (-jnp.arange(0, dh, 2, dtype=F32) / dh)
    ang = jnp.arange(seq, dtype=F32)[:, None] * freqs[None, :]
    cos, sin = jnp.cos(ang), jnp.sin(ang)
    cosf = jnp.concatenate([cos, cos], axis=-1)
    sinf = jnp.concatenate([-sin, sin], axis=-1)
    rep = LANES // dh
    return jnp.tile(cosf, (1, rep)), jnp.tile(sinf, (1, rep))


def _epilogue(kind, scale, x, tabs, aux):
    c128, s128, c64, s64 = tabs
    if kind == "rot128":
        y = x * c128 + pltpu.roll(x, 64, 1) * s128
    elif kind == "rot64":
        lane = lax.broadcasted_iota(jnp.int32, x.shape, 1)
        partner = jnp.where((lane % 64) < 32, pltpu.roll(x, 96, 1), pltpu.roll(x, 32, 1))
        y = x * c64 + partner * s64
    elif kind == "silu":
        y = x * jax.nn.sigmoid(x)
    elif kind == "hgrn_g":
        y = aux + (1.0 - aux) * jax.nn.sigmoid(x)
    else:
        y = x
    if scale is not None:
        y = y * scale
    return y


def _proj_body(h_ref, g_ref, w_ref, c128_ref, s128_ref, c64_ref, s64_ref, aux_ref, scale_ref, *rest,
               kinds, scaled, res_dils):
    o_ref = rest[0]
    res_refs = rest[1:1 + len(res_dils)]
    u_ref = rest[1 + len(res_dils)]
    tm = o_ref.shape[0]

    @pl.when(pl.program_id(1) == 0)
    def _norm():
        x = h_ref[...]
        ms = jnp.mean(x * x, axis=-1, keepdims=True)
        u_ref[...] = (x * lax.rsqrt(ms + NORM_EPS) * g_ref[...]).astype(BF16)

    tabs = (c128_ref[...] if "rot128" in kinds else None, s128_ref[...] if "rot128" in kinds else None,
            c64_ref[...] if "rot64" in kinds else None, s64_ref[...] if "rot64" in kinds else None)
    def pair_dot(p):
        return _dot(u_ref[...], w_ref[:, 2 * p * LANES:min(2 * p + 2, len(kinds)) * LANES])

    n_pairs = (len(kinds) + 1) // 2
    accs = {0: pair_dot(0)}
    for c, kind in enumerate(kinds):
        sl = slice(c * LANES, (c + 1) * LANES)
        aux = aux_ref[:, sl] if kind == "hgrn_g" else None
        if c % 2 == 0 and c // 2 + 1 < n_pairs:
            accs[c // 2 + 1] = pair_dot(c // 2 + 1)
        acc = accs[c // 2][:, (c % 2) * LANES:(c % 2 + 1) * LANES]
        y = _epilogue(kind, scale_ref[:, sl] if scaled else None, acc, tabs, aux)
        o_ref[:, sl] = y.astype(o_ref.dtype)
        if res_dils:
            y_ref = rest[-1]
            y_ref[c] = y
            for dil, r_ref in zip(res_dils, res_refs):
                for r in range(dil):
                    r_ref[r, :, sl] = y_ref[c, pl.ds(r, tm // dil, stride=dil), :].astype(r_ref.dtype)


def _proj(h, g, w, kinds, seq, tabs, *, scales=None, aux=None, out_dtype=None, res_dils=()):
    n, d = h.shape
    c = w.shape[1]
    tm = min(1024, seq)
    tn = min(PROJ_TILE_N_RES if res_dils else PROJ_TILE_N, c)
    if isinstance(kinds, str):
        kinds = (kinds,) * (tn // LANES)
    assert len(kinds) == tn // LANES and (c == tn or len(set(kinds)) == 1)
    if scales is None:
        scale_row = jnp.zeros((1, c), F32)
    else:
        scale_row = jnp.repeat(jnp.asarray(scales, F32), LANES).reshape(1, c)
    if aux is None:
        aux = jnp.zeros((1, c), F32)
    out_dtype = out_dtype or BF16
    pos_blocks = seq // tm
    tab_spec = pl.BlockSpec((tm, LANES), lambda i, j: (i % pos_blocks, 0))
    row_spec = pl.BlockSpec((1, tn), lambda i, j: (0, j))
    out_specs = [pl.BlockSpec((tm, tn), lambda i, j: (i, j))]
    out_shape = [jax.ShapeDtypeStruct((n, c), out_dtype)]
    scratch = [pltpu.VMEM((tm, d), BF16)]
    for dil in res_dils:
        out_specs.append(pl.BlockSpec((dil, tm // dil, tn), lambda i, j: (0, i, j)))
        out_shape.append(jax.ShapeDtypeStruct((dil, n // dil, c), out_dtype))
    if res_dils:
        scratch.append(pltpu.VMEM((tn // LANES, tm, LANES), F32))
    res = pl.pallas_call(
        functools.partial(_proj_body, kinds=tuple(kinds), scaled=scales is not None, res_dils=tuple(res_dils)),
        grid=(n // tm, c // tn),
        in_specs=[
            pl.BlockSpec((tm, d), lambda i, j: (i, 0)),
            pl.BlockSpec((1, d), lambda i, j: (0, 0)),
            pl.BlockSpec((d, tn), lambda i, j: (0, j)),
            tab_spec, tab_spec, tab_spec, tab_spec,
            row_spec, row_spec,
        ],
        out_specs=out_specs,
        out_shape=out_shape,
        scratch_shapes=scratch,
        compiler_params=_cparams(("parallel", "arbitrary")),
        name="norm_in_proj",
    )(h, g.reshape(1, d), w, *tabs, aux, scale_row)
    return res if res_dils else res[0]


def _outproj_body(og_ref, w_ref, h_ref, gf_ref, o_ref, *, final):
    y = h_ref[...] + _dot(og_ref[...], w_ref[...])
    if final:
        ms = jnp.mean(y * y, axis=-1, keepdims=True)
        y = y * lax.rsqrt(ms + NORM_EPS) * gf_ref[...]
    o_ref[...] = y


def _outproj(og, w, h, gf, final):
    n, di = og.shape
    d = w.shape[1]
    tm = min(1024, n)
    return pl.pallas_call(
        functools.partial(_outproj_body, final=final),
        grid=(n // tm,),
        in_specs=[
            pl.BlockSpec((tm, di), lambda i: (i, 0)),
            pl.BlockSpec((di, d), lambda i: (0, 0)),
            pl.BlockSpec((tm, d), lambda i: (i, 0)),
            pl.BlockSpec((1, d), lambda i: (0, 0)),
        ],
        out_specs=pl.BlockSpec((tm, d), lambda i: (i, 0)),
        out_shape=jax.ShapeDtypeStruct((n, d), F32),
        compiler_params=_cparams(("parallel",)),
        name="out_proj_residual",
    )(og, w, h, gf.reshape(1, d))


def _dsa_index_body(iq_ref, ik_ref, iw_ref, bias_ref, sc_ref, *, topk, tq, kc, seq):
    i = pl.program_id(1)
    n_chunks = ((i + 1) * tq + kc - 1) // kc
    int_min = jnp.int32(-2 ** 31)
    iw_t = iw_ref[...].astype(F32).T
    lane = lax.broadcasted_iota(jnp.int32, (kc, LANES), 1)
    s_loc = lax.broadcasted_iota(jnp.int32, (kc, tq), 0)
    t_pos = i * tq + lax.broadcasted_iota(jnp.int32, (kc, tq), 1)

    def score_chunk(c, carry):
        k0 = pl.multiple_of(c * kc, kc)
        ik2 = ik_ref[pl.ds(k0, kc), :]
        ik_lo = jnp.where(lane < A_IDX_DIM, ik2, jnp.zeros_like(ik2))
        ik_hi = jnp.where(lane >= A_IDX_DIM, ik2, jnp.zeros_like(ik2))
        score = jnp.zeros((kc, tq), F32)
        for p in range(A_IDX_HEADS // 2):
            iq2 = iq_ref[:, p * LANES:(p + 1) * LANES]
            score += jnp.maximum(_dot_nt(ik_lo, iq2), 0.0) * iw_t[2 * p:2 * p + 1, :]
            score += jnp.maximum(_dot_nt(ik_hi, iq2), 0.0) * iw_t[2 * p + 1:2 * p + 2, :]
        sc_ref[pl.ds(k0, kc), :] = jnp.where(k0 + s_loc <= t_pos, score, -jnp.inf)
        return carry

    lax.fori_loop(0, n_chunks, score_chunk, 0)

    def threshold(prefix):
        key = prefix ^ int_min
        return pltpu.bitcast(jnp.where(key < 0, key ^ jnp.int32(0x7FFFFFFF), key), F32)

    def bisect(it, prefix):
        cand = prefix | (jnp.int32(1) << (31 - it))
        thr = threshold(cand)

        def count(c, part):
            blk = sc_ref[pl.ds(pl.multiple_of(c * kc, kc), kc), :]
            hits = jnp.where(blk >= thr, 1.0, 0.0).reshape(kc // A_CNT_ROWS, A_CNT_ROWS, tq)
            return part + jnp.sum(hits, axis=0)

        part = lax.fori_loop(0, n_chunks, count, jnp.zeros((A_CNT_ROWS, tq), F32))
        cnt = jnp.sum(part, axis=0, keepdims=True)
        return jnp.where(cnt >= topk, cand, prefix)

    n_iter = jnp.where((i + 1) * tq <= topk, 0, 32)
    prefix = lax.fori_loop(0, n_iter, bisect, jnp.zeros((1, tq), jnp.int32))
    thr = threshold(prefix)
    take_all = prefix == 0

    def emit(c, carry):
        k0 = pl.multiple_of(c * kc, kc)
        sel = ((sc_ref[pl.ds(k0, kc), :] >= thr) | take_all) & (k0 + s_loc <= t_pos)
        bias_ref[pl.ds(k0, kc), :] = jnp.where(sel, 0.0, NEG_BIG).astype(bias_ref.dtype)
        return carry

    lax.fori_loop(0, n_chunks, emit, 0)

    def emit_masked(c, carry):
        bias_ref[pl.ds(pl.multiple_of(c * kc, kc), kc), :] = jnp.full((kc, tq), NEG_BIG, bias_ref.dtype)
        return carry

    lax.fori_loop(n_chunks, seq // kc, emit_masked, 0)


def _dsa_index(aux, batch, seq, topk):
    c_ik = A_IDX_HEADS * A_IDX_DIM // LANES + 2
    tq = min(A_Q_BLOCK, seq)
    kc = min(A_K_BLOCK, seq)
    nq = seq // tq
    return pl.pallas_call(
        functools.partial(_dsa_index_body, topk=topk, tq=tq, kc=kc, seq=seq),
        grid=(batch, nq),
        in_specs=[
            pl.BlockSpec((tq, A_IDX_HEADS * A_IDX_DIM), lambda b, i: (b * nq + i, 0)),
            pl.BlockSpec((seq, LANES), lambda b, i: (b, c_ik)),
            pl.BlockSpec((tq, LANES), lambda b, i: (b * nq + i, c_ik + 1)),
        ],
        out_specs=pl.BlockSpec((None, None, seq, tq), lambda b, i: (b, i, 0, 0)),
        out_shape=jax.ShapeDtypeStruct((batch, nq, seq, tq), BF16),
        scratch_shapes=[pltpu.VMEM((seq, tq), F32)],
        compiler_params=_cparams(("parallel", "parallel")),
        name="dsa_indexer_topk_mask",
    )(aux, aux, aux)


def _dsa_attn_body(q_ref, z_ref, k_ref, v_ref, bias_ref, o_ref, m_ref, l_ref, acc_ref, *, tq, kb):
    i = pl.program_id(1)
    j = pl.program_id(2)
    nh = A_HEADS
    dv = A_HEAD_DIM

    @pl.when(j == 0)
    def _init():
        m_ref[...] = jnp.full(m_ref.shape, NEG_BIG, F32)
        l_ref[...] = jnp.zeros(l_ref.shape, F32)
        acc_ref[...] = jnp.zeros(acc_ref.shape, F32)

    @pl.when(j * kb <= i * tq + tq - 1)
    def _step():
        k = k_ref[...]
        v_t = v_ref[...].astype(F32).T.astype(BF16)
        bias = bias_ref[...].astype(F32)
        for h0 in range(0, nh, A_HEAD_STAGE):
            hs = range(h0, h0 + A_HEAD_STAGE)
            s = [_dot_nt(k, q_ref[:, h * LANES:(h + 1) * LANES]) + bias for h in hs]
            ps, alphas = [], []
            for n, h in enumerate(hs):
                m_old = m_ref[h:h + 1, :]
                m_new = jnp.maximum(m_old, jnp.max(s[n], axis=0, keepdims=True))
                alpha = jnp.exp2(m_old - m_new)
                p = jnp.exp2(s[n] - m_new)
                l_ref[h:h + 1, :] = alpha * l_ref[h:h + 1, :] + jnp.sum(p, axis=0, keepdims=True)
                m_ref[h:h + 1, :] = m_new
                ps.append(p.astype(BF16))
                alphas.append(alpha)
            for n, h in enumerate(hs):
                rows = slice(h * dv, (h + 1) * dv)
                acc_ref[rows, :] = alphas[n] * acc_ref[rows, :] + _dot(v_t, ps[n])

    @pl.when(j == pl.num_programs(2) - 1)
    def _fin():
        for h in range(nh):
            sl = slice(h * LANES, (h + 1) * LANES)
            o_t = acc_ref[h * dv:(h + 1) * dv, :] * (1.0 / l_ref[h:h + 1, :])
            o_ref[:, sl] = (o_t.T * z_ref[:, sl].astype(F32)).astype(o_ref.dtype)


def _dsa_attn(q, z, aux, bias, batch, seq):
    n = q.shape[0]
    c_k = A_IDX_HEADS * A_IDX_DIM // LANES
    tq = bias.shape[-1]
    kb = min(A_K_BLOCK, seq)
    nq, nk = seq // tq, seq // kb

    def kj(i, j):
        return jnp.minimum(j, (i * tq + tq - 1) // kb)

    return pl.pallas_call(
        functools.partial(_dsa_attn_body, tq=tq, kb=kb),
        grid=(batch, nq, nk),
        in_specs=[
            pl.BlockSpec((tq, D_INNER), lambda b, i, j: (b * nq + i, 0)),
            pl.BlockSpec((tq, D_INNER), lambda b, i, j: (b * nq + i, 0)),
            pl.BlockSpec((kb, LANES), lambda b, i, j: (b * nk + kj(i, j), c_k)),
            pl.BlockSpec((kb, LANES), lambda b, i, j: (b * nk + kj(i, j), c_k + 1)),
            pl.BlockSpec((None, None, kb, tq), lambda b, i, j: (b, i, kj(i, j), 0)),
        ],
        out_specs=pl.BlockSpec((tq, D_INNER), lambda b, i, j: (b * nq + i, 0)),
        out_shape=jax.ShapeDtypeStruct((n, D_INNER), BF16),
        scratch_shapes=[
            pltpu.VMEM((A_HEADS, tq), F32),
            pltpu.VMEM((A_HEADS, tq), F32),
            pltpu.VMEM((A_HEADS * A_HEAD_DIM, tq), F32),
        ],
        compiler_params=_cparams(("parallel", "parallel", "arbitrary")),
        name="dsa_masked_attention",
    )(q, z, aux, aux, bias)


def _mixer_a(h, g, w_in, w_out, batch, seq, tabs, zero_aux):
    hd = A_HEADS * A_HEAD_DIM
    hi = A_IDX_HEADS * A_IDX_DIM
    o0 = np.cumsum([0, hd, A_HEAD_DIM, A_HEAD_DIM, hi, A_IDX_DIM, A_IDX_HEADS, D_INNER])
    wq, wk, wv, wiq, wik, wiw, wz = (w_in[:, o0[t]:o0[t + 1]] for t in range(7))
    pad = jnp.zeros((D_MODEL, LANES - A_IDX_HEADS), F32)
    w_aux = jnp.concatenate([wiq, wk, wv, wik, wik, wiw, pad], axis=1).astype(BF16)
    n_iq = hi // LANES
    q = _proj(h, g, wq.astype(BF16), "rot128", seq, tabs, scales=[A_HEAD_DIM ** -0.5 * LOG2E] * A_HEADS)
    z = _proj(h, g, wz.astype(BF16), "silu", seq, tabs)
    aux = _proj(h, g, w_aux, ["rot64"] * n_iq + ["rot128", "none", "rot64", "none"], seq, tabs,
                scales=[1.0] * (n_iq + 3) + [A_IDX_HEADS ** -0.5 * A_IDX_DIM ** -0.5])
    bias = _dsa_index(aux, batch, seq, min(A_TOPK_MAX, seq // 4))
    return _dsa_attn(q, z, aux, bias, batch, seq)


def _band_body(*refs, blk, band_prev, cfg_prev, final):
    refs = list(refs)
    q_ref, kc_ref, vc_ref = refs[:3]
    del refs[:3]
    if band_prev:
        kp_ref, vp_ref = refs[-2:]
        del refs[-2:]

        @pl.when(pl.program_id(2) == 0)
        def _no_prev():
            kp_ref[...] = jnp.zeros(kp_ref.shape, kp_ref.dtype)
            vp_ref[...] = jnp.zeros(vp_ref.shape, vp_ref.dtype)
    if cfg_prev:
        op4_ref, lp4_ref = refs[:2]
        del refs[:2]
        op_ref, lp_ref = refs[-2:]
        del refs[-2:]
        for a in range(B_DIL_RATIO):
            rows = pl.ds(a, blk // B_DIL_RATIO, stride=B_DIL_RATIO)
            lp_ref[rows, :] = lp4_ref[a]
            for h in range(B_HEADS):
                op_ref[h, rows, :] = op4_ref[a, :, h * LANES:(h + 1) * LANES].astype(F32)
    if final:
        z_ref, o_ref = refs
    else:
        o_ref, l_ref = refs
    ki = lax.broadcasted_iota(jnp.int32, (blk, blk), 0)
    qi = lax.broadcasted_iota(jnp.int32, (blk, blk), 1)
    mask_cur = ki <= qi
    eye = ki == qi
    if band_prev:
        mask_prev = (ki >= qi) & (pl.program_id(2) > 0)
    if cfg_prev:
        lse_prev_t = _pad_rows(lp_ref[...].T, LANES)
    heads = [slice(h * LANES, (h + 1) * LANES) for h in range(B_HEADS)]
    s_c = [jnp.where(mask_cur, _dot_nt(kc_ref[:, sl], q_ref[:, sl]), NEG_BIG) for sl in heads]
    if band_prev:
        s_p = [jnp.where(mask_prev, _dot_nt(kp_ref[:, sl], q_ref[:, sl]), NEG_BIG) for sl in heads]
    lse_rows, p_c, p_p, d_old = [], [], [], []
    for h in range(B_HEADS):
        m = jnp.max(s_c[h], axis=0, keepdims=True)
        if band_prev:
            m = jnp.maximum(m, jnp.max(s_p[h], axis=0, keepdims=True))
        e_c = jnp.exp2(s_c[h] - m)
        den = jnp.sum(e_c, axis=0, keepdims=True)
        if band_prev:
            e_p = jnp.exp2(s_p[h] - m)
            den += jnp.sum(e_p, axis=0, keepdims=True)
        lse = m + jnp.log2(den)
        scale = 1.0 / den
        if cfg_prev:
            lse_prev = lse_prev_t[h:h + 1, :]
            m2 = jnp.maximum(lse, lse_prev)
            w_new, w_old = jnp.exp2(lse - m2), jnp.exp2(lse_prev - m2)
            scale = scale * w_new / (w_new + w_old)
            d_old.append(jnp.where(eye, w_old / (w_new + w_old), 0.0).astype(BF16))
            lse = m2 + jnp.log2(w_new + w_old)
        p_c.append((e_c * scale).astype(BF16))
        if band_prev:
            p_p.append((e_p * scale).astype(BF16))
        lse_rows.append(lse)
    for h, sl in enumerate(heads):
        o = _dot_tn(p_c[h], vc_ref[:, sl])
        if band_prev:
            o += _dot_tn(p_p[h], vp_ref[:, sl])
        if cfg_prev:
            o += _dot(d_old[h], op_ref[h].astype(BF16))
        if final:
            o_ref[:, sl] = (o * z_ref[:, sl].astype(F32)).astype(o_ref.dtype)
        else:
            o_ref[:, sl] = o.astype(o_ref.dtype)
    if not final:
        lse_t = _pad_rows(jnp.concatenate(lse_rows, axis=0), LANES)
        l_ref[...] = lse_t.T
    if band_prev:
        kp_ref[...] = kc_ref[...]
        vp_ref[...] = vc_ref[...]


def _pad_rows(x, rows):
    if x.shape[0] == rows:
        return x
    return jnp.concatenate([x, jnp.zeros((rows - x.shape[0], x.shape[1]), x.dtype)], axis=0)


def _band_attn(qk, v, gate, batch, seq, dil, prev):
    final = gate is not None
    n = qk.shape[0] * qk.shape[1]
    length = seq // dil
    blk = min(128, length)
    nbl = length // blk
    band_prev = nbl > 1

    def spec(col):
        return pl.BlockSpec((None, blk, D_INNER), lambda b, r, i: (r, b * nbl + i, col))

    o_spec = pl.BlockSpec((None, blk, D_INNER), lambda b, r, i: (r, b * nbl + i, 0))
    l_spec = pl.BlockSpec((None, blk, LANES), lambda b, r, i: (r, b * nbl + i, 0))
    in_specs, args, scratch = [spec(0), spec(1), spec(0)], [qk, qk, v], []
    if prev is not None:
        coarse = B_DIL_RATIO * dil
        assert prev[0].shape[0] == coarse and blk % (16 * B_DIL_RATIO) == 0
        pblk = blk // B_DIL_RATIO
        in_specs += [pl.BlockSpec((B_DIL_RATIO, None, pblk, D_INNER), lambda b, r, i: (0, r, b * nbl + i, 0)),
                     pl.BlockSpec((B_DIL_RATIO, None, pblk, LANES), lambda b, r, i: (0, r, b * nbl + i, 0))]
        args += [prev[0].reshape(B_DIL_RATIO, dil, n // coarse, D_INNER),
                 prev[1].reshape(B_DIL_RATIO, dil, n // coarse, LANES)]
        scratch = [pltpu.VMEM((B_HEADS, blk, LANES), F32), pltpu.VMEM((blk, LANES), F32)]
    if band_prev:
        scratch += [pltpu.VMEM((blk, D_INNER), BF16), pltpu.VMEM((blk, D_INNER), BF16)]
    o_shape = jax.ShapeDtypeStruct((dil, n // dil, D_INNER), BF16)
    if final:
        in_specs.append(spec(0))
        args.append(gate)
        out_specs, out_shape = o_spec, o_shape
    else:
        out_specs = [o_spec, l_spec]
        out_shape = [o_shape, jax.ShapeDtypeStruct((dil, n // dil, LANES), F32)]
    return pl.pallas_call(
        functools.partial(_band_body, blk=blk, band_prev=band_prev, cfg_prev=prev is not None, final=final),
        grid=(batch, dil, nbl),
        in_specs=in_specs,
        out_specs=out_specs,
        out_shape=out_shape,
        scratch_shapes=scratch,
        compiler_params=_cparams(("parallel", "parallel", "arbitrary")),
        name="dilated_band_attention",
    )(*args)


def _mixer_b(h, g, w_in, w_out, batch, seq, tabs, zero_aux):
    configs = sorted(B_CONFIGS, key=lambda wd: -wd[1])
    dils = [dil for _, dil in configs]
    assert dils[-1] == 1 and all(a == B_DIL_RATIO * b for a, b in zip(dils, dils[1:]))
    assert all(window // dil == 128 and seq % dil == 0 for window, dil in configs)
    w = w_in.astype(BF16)
    qk = _proj(h, g, w[:, :2 * D_INNER], "rot128", seq, tabs, res_dils=dils[:-1],
               scales=[B_HEAD_DIM ** -0.5 * LOG2E] * B_HEADS + [1.0] * B_HEADS)
    v = _proj(h, g, w[:, 2 * D_INNER:3 * D_INNER], "none", seq, tabs, res_dils=dils[:-1])
    z = _proj(h, g, w[:, 3 * D_INNER:], "silu", seq, tabs)
    qks = list(qk[1:]) + [qk[0][None]]
    vs = list(v[1:]) + [v[0][None]]
    prev = None
    for qk_d, v_d, dil in zip(qks, vs, dils):
        prev = _band_attn(qk_d, v_d, z[None] if dil == 1 else None, batch, seq, dil, prev)
    return prev.reshape(-1, D_INNER)


def _retention_body(lg_ref, q_ref, k_ref, v_ref, z_ref, o_ref, st_ref, *, seq):
    c = C_CHUNK
    lg = lg_ref[pl.program_id(1)]
    ri = lax.broadcasted_iota(jnp.int32, (c, c), 0)
    ci = lax.broadcasted_iota(jnp.int32, (c, c), 1)
    rel = (ri - ci).astype(F32)
    decay = jnp.where(rel >= 0, jnp.exp(jnp.maximum(rel, 0.0) * lg), 0.0)
    idx = lax.broadcasted_iota(jnp.int32, (c, 1), 0).astype(F32)
    k_decay = jnp.exp((c - 1 - idx) * lg)
    q_decay = jnp.exp((idx + 1.0) * lg)
    chunk_decay = jnp.exp(jnp.full((1, C_QK_DIM), c, F32) * lg)
    st_ref[...] = jnp.zeros(st_ref.shape, F32)

    unroll = min(C_UNROLL, seq // c)

    def group(gidx, carry):
        rows0 = [pl.multiple_of((gidx * unroll + u) * c, c) for u in range(unroll)]
        qs = [q_ref[pl.ds(r0, c), :] for r0 in rows0]
        ks = [k_ref[pl.ds(r0, c), :] for r0 in rows0]
        vs = [v_ref[pl.ds(r0, c), :] for r0 in rows0]
        scores = [(_dot_nt(q, k) * decay).astype(BF16) for q, k in zip(qs, ks)]
        o_intra = [_dot(s, v) for s, v in zip(scores, vs)]
        q_dec = [(q.astype(F32) * q_decay).astype(BF16) for q in qs]
        k_dec = [(k.astype(F32) * k_decay).astype(BF16) for k in ks]
        outs = []
        for u in range(unroll):
            st = st_ref[...]
            outs.append(o_intra[u] + _dot_nt(q_dec[u], st.astype(BF16)))
            st_ref[...] = st * chunk_decay + _dot_tn(vs[u], k_dec[u])
        for u, r0 in enumerate(rows0):
            o = outs[u]
            o = o * lax.rsqrt(jnp.mean(o * o, axis=-1, keepdims=True) + NORM_EPS)
            o_ref[pl.ds(r0, c), :] = (o * z_ref[pl.ds(r0, c), :].astype(F32)).astype(o_ref.dtype)
        return carry

    lax.fori_loop(0, seq // (c * unroll), group, 0)


def _mixer_c(h, g, w_in, w_out, batch, seq, tabs, zero_aux):
    w = w_in.astype(BF16)
    nh = C_HEADS
    dqk = 2 * nh * C_QK_DIM
    qk = _proj(h, g, w[:, :dqk], "rot128", seq, tabs, scales=[1.0] * nh + [C_QK_DIM ** -0.5] * nh)
    v = _proj(h, g, w[:, dqk:dqk + D_INNER], "none", seq, tabs)
    z = _proj(h, g, w[:, dqk + D_INNER:], "silu", seq, tabs)
    n = qk.shape[0]
    log_gamma = jnp.log1p(-jnp.exp2(-5.0 - jnp.arange(nh, dtype=F32)))
    return pl.pallas_call(
        functools.partial(_retention_body, seq=seq),
        grid=(batch, nh),
        in_specs=[
            pl.BlockSpec(memory_space=pltpu.SMEM),
            pl.BlockSpec((seq, C_QK_DIM), lambda b, h: (b, h)),
            pl.BlockSpec((seq, C_QK_DIM), lambda b, h: (b, nh + h)),
            pl.BlockSpec((seq, C_V_DIM), lambda b, h: (b, h)),
            pl.BlockSpec((seq, C_V_DIM), lambda b, h: (b, h)),
        ],
        out_specs=pl.BlockSpec((seq, C_V_DIM), lambda b, h: (b, h)),
        out_shape=jax.ShapeDtypeStruct((n, D_INNER), BF16),
        scratch_shapes=[pltpu.VMEM((C_V_DIM, C_QK_DIM), F32)],
        compiler_params=_cparams(("parallel", "parallel")),
        name="retention",
    )(log_gamma, qk, qk, v, z)


def _hgrn_body(q_ref, g_ref, v_ref, z_ref, o_ref, st_ref, *, seq):
    c = D_CHUNK
    w = D_HEAD_GROUP * LANES
    halves = [D_SUB * 2 ** lv for lv in range(16) if D_SUB * 2 ** (lv + 1) <= c]
    ri = lax.broadcasted_iota(jnp.int32, (c, c), 0)
    ci = lax.broadcasted_iota(jnp.int32, (c, c), 1)
    mats = [ri >= ci]
    pair_masks = []
    for half in halves:
        block = 2 * half
        mid = (ri // block) * block + half - 1
        right = (ri % block) >= half
        mats.append((right & (ci > mid) & (ci <= ri)) | (jnp.logical_not(right) & (ci > ri) & (ci <= mid)))
        pair_masks.append(((ri // block) == (ci // block)) & right & ((ci % block) < half))
    diag_masks = []
    for j in range(D_SUB):
        key = (ri // D_SUB) * D_SUB + j
        mats.append((ci > key) & (ci <= ri))
        diag_masks.append((ci == key) & (ri >= key))
    m1 = jnp.concatenate([m.astype(BF16) for m in mats], axis=0)
    m3 = jnp.concatenate([m1, m1, m1], axis=1)
    rows = lax.broadcasted_iota(jnp.int32, (c, w), 0)
    right_rows = [(rows % (2 * half)) >= half for half in halves]
    n_lv = len(halves)
    st_ref[...] = jnp.zeros(st_ref.shape, F32)

    head_sl = [slice(hd * LANES, (hd + 1) * LANES) for hd in range(D_HEAD_GROUP)]
    unroll = min(D_UNROLL, seq // c)

    def group_rows(gidx):
        return [pl.multiple_of((gidx * unroll + u) * c, c) for u in range(unroll)]

    def prepare(gidx):
        rows0 = group_rows(gidx)
        gates = [g_ref[pl.ds(r0, c), :] for r0 in rows0]
        bds = []
        for gate in gates:
            lf = jnp.log2(gate)
            hi = lf.astype(BF16)
            r1 = lf - hi.astype(F32)
            mid_part = r1.astype(BF16)
            lo = (r1 - mid_part.astype(F32)).astype(BF16)
            bds.append(_dot(m3, jnp.concatenate([hi, mid_part, lo], axis=0)))
        prep = []
        for u, r0 in enumerate(rows0):
            bd, kk = bds[u], 1.0 - gates[u]
            b = bd[0:c]
            q = q_ref[pl.ds(r0, c), :].astype(F32)
            v = v_ref[pl.ds(r0, c), :]
            qe = (q * jnp.exp2(b)).astype(BF16)
            qts, kts = [], []
            for lv in range(len(halves)):
                e = jnp.exp2(bd[(lv + 1) * c:(lv + 2) * c])
                qts.append(jnp.where(right_rows[lv], q * e, 0.0).astype(BF16))
                kts.append(jnp.where(right_rows[lv], 0.0, kk * e).astype(BF16))
            q_diag = jnp.concatenate(
                [(q * jnp.exp2(bd[(1 + n_lv + j) * c:(2 + n_lv + j) * c])).astype(BF16) for j in range(D_SUB)], axis=0)
            b_last = b[c - 1:c, :]
            k_dec = (kk * jnp.exp2(b_last - b)).astype(BF16)
            prep.append((qe, qts, kts, v, (q_diag, kk.astype(BF16)), k_dec, jnp.exp2(b_last)))
        o_local = []
        for qe, qts, kts, v, (q_diag, kk16), k_dec, eb_last in prep:
            per_head = []
            for hd, sl in enumerate(head_sl):
                a = jnp.zeros((c, c), F32)
                for lv in range(n_lv):
                    a = a + jnp.where(pair_masks[lv], _dot_nt(qts[lv][:, sl], kts[lv][:, sl]), 0.0)
                g_diag = _dot_nt(q_diag[:, sl], kk16[:, sl])
                for j in range(D_SUB):
                    a = a + jnp.where(diag_masks[j], g_diag[j * c:(j + 1) * c], 0.0)
                per_head.append(_dot(a.astype(BF16), v[:, sl]))
            o_local.append(per_head)
        return [(o_local[u], p[0], p[3], p[5], p[6]) for u, p in enumerate(prep)]

    def finish(gidx, prepared):
        outs = []
        for o_loc, qe, v, k_dec, eb_last in prepared:
            per_head = []
            for hd, sl in enumerate(head_sl):
                st = st_ref[hd]
                per_head.append(o_loc[hd] + _dot_nt(qe[:, sl], st.astype(BF16)))
                st_ref[hd] = st * eb_last[:, sl] + _dot_tn(v[:, sl], k_dec[:, sl])
            outs.append(per_head)
        for u, r0 in enumerate(group_rows(gidx)):
            for hd, sl in enumerate(head_sl):
                o = outs[u][hd]
                o = o * lax.rsqrt(jnp.mean(o * o, axis=-1, keepdims=True) + NORM_EPS)
                o_ref[pl.ds(r0, c), sl] = (o * z_ref[pl.ds(r0, c), sl].astype(F32)).astype(o_ref.dtype)

    def trip(gidx, carry):
        finish(gidx, prepare(gidx))
        return carry

    lax.fori_loop(0, seq // (c * unroll), trip, 0)


def _mixer_d(h, g, w_in, w_out, lower_bound, batch, seq, tabs, zero_aux):
    wq, wf, wi, wz = (w_in[:, t * D_INNER:(t + 1) * D_INNER] for t in range(4))
    qz = _proj(h, g, jnp.concatenate([wq, wz], axis=1).astype(BF16), "silu", seq, tabs,
               scales=[D_K_DIM ** -0.5] * D_HEADS + [1.0] * D_HEADS)
    v = _proj(h, g, wi.astype(BF16), "none", seq, tabs)
    gates = _proj(h, g, wf.astype(BF16), "hgrn_g", seq, tabs, out_dtype=F32,
                  aux=lower_bound.reshape(1, D_INNER).astype(F32))
    n = qz.shape[0]
    ng = D_HEADS // D_HEAD_GROUP
    w = D_HEAD_GROUP * LANES
    return pl.pallas_call(
        functools.partial(_hgrn_body, seq=seq),
        grid=(batch, ng),
        in_specs=[
            pl.BlockSpec((seq, w), lambda b, h: (b, h)),
            pl.BlockSpec((seq, w), lambda b, h: (b, h)),
            pl.BlockSpec((seq, w), lambda b, h: (b, h)),
            pl.BlockSpec((seq, w), lambda b, h: (b, ng + h)),
        ],
        out_specs=pl.BlockSpec((seq, w), lambda b, h: (b, h)),
        out_shape=jax.ShapeDtypeStruct((n, D_INNER), BF16),
        scratch_shapes=[pltpu.VMEM((D_HEAD_GROUP, D_V_DIM, D_K_DIM), F32)],
        compiler_params=_cparams(("parallel", "parallel")),
        name="hgrn2_recurrence",
    )(qz, gates, v, qz)


def kernel(x, norm_g, final_g, a_w_in, a_w_out, b_w_in, b_w_out, c_w_in, c_w_out, d_w_in, d_w_out, hgrn_lb_logits):
    batch, seq, d = x.shape
    depth = norm_g.shape[0]
    lb_cum = jnp.cumsum(jax.nn.softmax(hgrn_lb_logits.astype(F32), axis=0), axis=0)
    lower_bounds = lb_cum - lb_cum[0]
    tabs = _rope_tables(seq, 128) + _rope_tables(seq, 64)
    h = x.reshape(batch * seq, d)
    for layer in range(depth):
        kind, slot = layer % N_MIXERS, layer // N_MIXERS
        g = norm_g[layer]
        if kind == 0:
            og, w_out = _mixer_a(h, g, a_w_in[slot], a_w_out[slot], batch, seq, tabs, None), a_w_out[slot]
        elif kind == 1:
            og, w_out = _mixer_b(h, g, b_w_in[slot], b_w_out[slot], batch, seq, tabs, None), b_w_out[slot]
        elif kind == 2:
            og, w_out = _mixer_c(h, g, c_w_in[slot], c_w_out[slot], batch, seq, tabs, None), c_w_out[slot]
        else:
            og, w_out = _mixer_d(h, g, d_w_in[slot], d_w_out[slot], lower_bounds[layer], batch, seq, tabs, None), d_w_out[slot]
        h = _outproj(og, w_out.astype(BF16), h, final_g, final=(layer == depth - 1))
    return h.reshape(batch, seq, d)
```

```python
import functools

import jax
import jax.numpy as jnp
import numpy as np
from jax import lax
from jax.experimental import pallas as pl
from jax.experimental.pallas import tpu as pltpu

F32 = jnp.float32
BF16 = jnp.bfloat16

D_MODEL = 1024
D_INNER = 2048
N_MIXERS = 4
NORM_EPS = 1e-6
ROPE_THETA = 10000.0
LANES = 128
NEG_BIG = -1e30
LOG2E = 1.4426950408889634

A_HEADS, A_HEAD_DIM = 16, 128
A_IDX_HEADS, A_IDX_DIM = 16, 64
A_TOPK_MAX = 256
A_Q_BLOCK = 256
A_K_BLOCK = 512
A_CNT_ROWS = 64
A_HEAD_STAGE = 16
B_HEADS, B_HEAD_DIM = 16, 128
B_CONFIGS = ((128, 1), (512, 4), (2048, 16))
B_DIL_RATIO = 4
B_BATCH_GROUP = 2
C_HEADS, C_QK_DIM, C_V_DIM, C_CHUNK = 8, 128, 256, 128
C_UNROLL = 4
D_HEADS, D_K_DIM, D_V_DIM = 16, 128, 128
D_CHUNK = 64
D_SUB = 2
D_HEAD_GROUP = 2
D_UNROLL = 4

PROJ_TILE_N = 2048
PROJ_TILE_N_RES = 1024
VMEM_LIMIT = 56 * 1024 * 1024


def _cparams(sem):
    return pltpu.CompilerParams(dimension_semantics=sem, vmem_limit_bytes=VMEM_LIMIT)


def _dot(a, b):
    return jnp.dot(a, b, preferred_element_type=F32)


def _dot_nt(a, b):
    return lax.dot_general(a, b, (((1,), (1,)), ((), ())), preferred_element_type=F32)


def _dot_tn(a, b):
    return lax.dot_general(a, b, (((0,), (0,)), ((), ())), preferred_element_type=F32)


def _rope_tables(seq, dh):
    freqs = ROPE_THETA ** (-jnp.arange(0, dh, 2, dtype=F32) / dh)
    ang = jnp.arange(seq, dtype=F32)[:, None] * freqs[None, :]
    cos, sin = jnp.cos(ang), jnp.sin(ang)
    cosf = jnp.concatenate([cos, cos], axis=-1)
    sinf = jnp.concatenate([-sin, sin], axis=-1)
    rep = LANES // dh
    return jnp.tile(cosf, (1, rep)), jnp.tile(sinf, (1, rep))


def _epilogue(kind, scale, x, tabs, aux):
    c128, s128, c64, s64 = tabs
    if kind == "rot128":
        y = x * c128 + pltpu.roll(x, 64, 1) * s128
    elif kind == "rot64":
        lane = lax.broadcasted_iota(jnp.int32, x.shape, 1)
        partner = jnp.where((lane % 64) < 32, pltpu.roll(x, 96, 1), pltpu.roll(x, 32, 1))
        y = x * c64 + partner * s64
    elif kind == "silu":
        y = x * jax.nn.sigmoid(x)
    elif kind == "hgrn_g":
        y = aux + (1.0 - aux) * jax.nn.sigmoid(x)
    else:
        y = x
    if scale is not None:
        y = y * scale
    return y


def _proj_body(h_ref, g_ref, w_ref, c128_ref, s128_ref, c64_ref, s64_ref, aux_ref, scale_ref, *rest,
               kinds, scaled, res_dils):
    o_ref = rest[0]
    res_refs = rest[1:1 + len(res_dils)]
    u_ref = rest[1 + len(res_dils)]
    tm = o_ref.shape[0]

    @pl.when(pl.program_id(1) == 0)
    def _norm():
        x = h_ref[...]
        ms = jnp.mean(x * x, axis=-1, keepdims=True)
        u_ref[...] = (x * lax.rsqrt(ms + NORM_EPS) * g_ref[...]).astype(BF16)

    tabs = (c128_ref[...] if "rot128" in kinds else None, s128_ref[...] if "rot128" in kinds else None,
            c64_ref[...] if "rot64" in kinds else None, s64_ref[...] if "rot64" in kinds else None)
    def pair_dot(p):
        return _dot(u_ref[...], w_ref[:, 2 * p * LANES:min(2 * p + 2, len(kinds)) * LANES])

    n_pairs = (len(kinds) + 1) // 2
    accs = {0: pair_dot(0)}
    for c, kind in enumerate(kinds):
        sl = slice(c * LANES, (c + 1) * LANES)
        aux = aux_ref[:, sl] if kind == "hgrn_g" else None
        if c % 2 == 0 and c // 2 + 1 < n_pairs:
            accs[c // 2 + 1] = pair_dot(c // 2 + 1)
        acc = accs[c // 2][:, (c % 2) * LANES:(c % 2 + 1) * LANES]
        y = _epilogue(kind, scale_ref[:, sl] if scaled else None, acc, tabs, aux)
        o_ref[:, sl] = y.astype(o_ref.dtype)
        if res_dils:
            y_ref = rest[-1]
            y_ref[c] = y
            for dil, r_ref in zip(res_dils, res_refs):
                for r in range(dil):
                    r_ref[r, :, sl] = y_ref[c, pl.ds(r, tm // dil, stride=dil), :].astype(r_ref.dtype)


def _proj(h, g, w, kinds, seq, tabs, *, scales=None, aux=None, out_dtype=None, res_dils=()):
    n, d = h.shape
    c = w.shape[1]
    tm = min(1024, seq)
    tn = min(PROJ_TILE_N_RES if res_dils else PROJ_TILE_N, c)
    if isinstance(kinds, str):
        kinds = (kinds,) * (tn // LANES)
    assert len(kinds) == tn // LANES and (c == tn or len(set(kinds)) == 1)
    if scales is None:
        scale_row = jnp.zeros((1, c), F32)
    else:
        scale_row = jnp.repeat(jnp.asarray(scales, F32), LANES).reshape(1, c)
    if aux is None:
        aux = jnp.zeros((1, c), F32)
    out_dtype = out_dtype or BF16
    pos_blocks = seq // tm
    tab_spec = pl.BlockSpec((tm, LANES), lambda i, j: (i % pos_blocks, 0))
    row_spec = pl.BlockSpec((1, tn), lambda i, j: (0, j))
    out_specs = [pl.BlockSpec((tm, tn), lambda i, j: (i, j))]
    out_shape = [jax.ShapeDtypeStruct((n, c), out_dtype)]
    scratch = [pltpu.VMEM((tm, d), BF16)]
    for dil in res_dils:
        out_specs.append(pl.BlockSpec((dil, tm // dil, tn), lambda i, j: (0, i, j)))
        out_shape.append(jax.ShapeDtypeStruct((dil, n // dil, c), out_dtype))
    if res_dils:
        scratch.append(pltpu.VMEM((tn // LANES, tm, LANES), F32))
    res = pl.pallas_call(
        functools.partial(_proj_body, kinds=tuple(kinds), scaled=scales is not None, res_dils=tuple(res_dils)),
        grid=(n // tm, c // tn),
        in_specs=[
            pl.BlockSpec((tm, d), lambda i, j: (i, 0)),
            pl.BlockSpec((1, d), lambda i, j: (0, 0)),
            pl.BlockSpec((d, tn), lambda i, j: (0, j)),
            tab_spec, tab_spec, tab_spec, tab_spec,
            row_spec, row_spec,
        ],
        out_specs=out_specs,
        out_shape=out_shape,
        scratch_shapes=scratch,
        compiler_params=_cparams(("parallel", "arbitrary")),
        name="norm_in_proj",
    )(h, g.reshape(1, d), w, *tabs, aux, scale_row)
    return res if res_dils else res[0]


def _outproj_body(og_ref, w_ref, h_ref, gf_ref, o_ref, *, final):
    y = h_ref[...] + _dot(og_ref[...], w_ref[...])
    if final:
        ms = jnp.mean(y * y, axis=-1, keepdims=True)
        y = y * lax.rsqrt(ms + NORM_EPS) * gf_ref[...]
    o_ref[...] = y


def _outproj(og, w, h, gf, final):
    n, di = og.shape
    d = w.shape[1]
    tm = min(1024, n)
    return pl.pallas_call(
        functools.partial(_outproj_body, final=final),
        grid=(n // tm,),
        in_specs=[
            pl.BlockSpec((tm, di), lambda i: (i, 0)),
            pl.BlockSpec((di, d), lambda i: (0, 0)),
            pl.BlockSpec((tm, d), lambda i: (i, 0)),
            pl.BlockSpec((1, d), lambda i: (0, 0)),
        ],
        out_specs=pl.BlockSpec((tm, d), lambda i: (i, 0)),
        out_shape=jax.ShapeDtypeStruct((n, d), F32),
        compiler_params=_cparams(("parallel",)),
        name="out_proj_residual",
    )(og, w, h, gf.reshape(1, d))


def _dsa_index_body(iq_ref, ik_ref, iw_ref, bias_ref, sc_ref, *, topk, tq, kc, seq):
    i = pl.program_id(1)
    n_chunks = ((i + 1) * tq + kc - 1) // kc
    int_min = jnp.int32(-2 ** 31)
    iw_t = iw_ref[...].astype(F32).T
    lane = lax.broadcasted_iota(jnp.int32, (kc, LANES), 1)
    s_loc = lax.broadcasted_iota(jnp.int32, (kc, tq), 0)
    t_pos = i * tq + lax.broadcasted_iota(jnp.int32, (kc, tq), 1)

    def score_chunk(c, carry):
        k0 = pl.multiple_of(c * kc, kc)
        ik2 = ik_ref[pl.ds(k0, kc), :]
        ik_lo = jnp.where(lane < A_IDX_DIM, ik2, jnp.zeros_like(ik2))
        ik_hi = jnp.where(lane >= A_IDX_DIM, ik2, jnp.zeros_like(ik2))
        score = jnp.zeros((kc, tq), F32)
        for p in range(A_IDX_HEADS // 2):
            iq2 = iq_ref[:, p * LANES:(p + 1) * LANES]
            score += jnp.maximum(_dot_nt(ik_lo, iq2), 0.0) * iw_t[2 * p:2 * p + 1, :]
            score += jnp.maximum(_dot_nt(ik_hi, iq2), 0.0) * iw_t[2 * p + 1:2 * p + 2, :]
        sc_ref[pl.ds(k0, kc), :] = jnp.where(k0 + s_loc <= t_pos, score, -jnp.inf)
        return carry

    lax.fori_loop(0, n_chunks, score_chunk, 0)

    def threshold(prefix):
        key = prefix ^ int_min
        return pltpu.bitcast(jnp.where(key < 0, key ^ jnp.int32(0x7FFFFFFF), key), F32)

    def bisect(it, prefix):
        cand = prefix | (jnp.int32(1) << (31 - it))
        thr = threshold(cand)

        def count(c, part):
            blk = sc_ref[pl.ds(pl.multiple_of(c * kc, kc), kc), :]
            hits = jnp.where(blk >= thr, 1.0, 0.0).reshape(kc // A_CNT_ROWS, A_CNT_ROWS, tq)
            return part + jnp.sum(hits, axis=0)

        part = lax.fori_loop(0, n_chunks, count, jnp.zeros((A_CNT_ROWS, tq), F32))
        cnt = jnp.sum(part, axis=0, keepdims=True)
        return jnp.where(cnt >= topk, cand, prefix)

    n_iter = jnp.where((i + 1) * tq <= topk, 0, 32)
    prefix = lax.fori_loop(0, n_iter, bisect, jnp.zeros((1, tq), jnp.int32))
    thr = threshold(prefix)
    take_all = prefix == 0

    def emit(c, carry):
        k0 = pl.multiple_of(c * kc, kc)
        sel = ((sc_ref[pl.ds(k0, kc), :] >= thr) | take_all) & (k0 + s_loc <= t_pos)
        bias_ref[pl.ds(k0, kc), :] = jnp.where(sel, 0.0, NEG_BIG).astype(bias_ref.dtype)
        return carry

    lax.fori_loop(0, n_chunks, emit, 0)

    def emit_masked(c, carry):
        bias_ref[pl.ds(pl.multiple_of(c * kc, kc), kc), :] = jnp.full((kc, tq), NEG_BIG, bias_ref.dtype)
        return carry

    lax.fori_loop(n_chunks, seq // kc, emit_masked, 0)


def _dsa_index(aux, batch, seq, topk):
    c_ik = A_IDX_HEADS * A_IDX_DIM // LANES + 2
    tq = min(A_Q_BLOCK, seq)
    kc = min(A_K_BLOCK, seq)
    nq = seq // tq
    return pl.pallas_call(
        functools.partial(_dsa_index_body, topk=topk, tq=tq, kc=kc, seq=seq),
        grid=(batch, nq),
        in_specs=[
            pl.BlockSpec((tq, A_IDX_HEADS * A_IDX_DIM), lambda b, i: (b * nq + i, 0)),
            pl.BlockSpec((seq, LANES), lambda b, i: (b, c_ik)),
            pl.BlockSpec((tq, LANES), lambda b, i: (b * nq + i, c_ik + 1)),
        ],
        out_specs=pl.BlockSpec((None, None, seq, tq), lambda b, i: (b, i, 0, 0)),
        out_shape=jax.ShapeDtypeStruct((batch, nq, seq, tq), BF16),
        scratch_shapes=[pltpu.VMEM((seq, tq), F32)],
        compiler_params=_cparams(("parallel", "parallel")),
        name="dsa_indexer_topk_mask",
    )(aux, aux, aux)


def _dsa_attn_body(qi_ref, kj_ref, q_ref, z_ref, k_ref, v_ref, bias_ref, o_ref, m_ref, l_ref, acc_ref, *, tq, kb):
    t = pl.program_id(1)
    i = qi_ref[t]
    j = kj_ref[t]
    nh = A_HEADS
    dv = A_HEAD_DIM

    @pl.when(j == 0)
    def _init():
        m_ref[...] = jnp.full(m_ref.shape, NEG_BIG, F32)
        l_ref[...] = jnp.zeros(l_ref.shape, F32)
        acc_ref[...] = jnp.zeros(acc_ref.shape, F32)

    k = k_ref[...]
    v_t = v_ref[...].astype(F32).T.astype(BF16)
    bias = bias_ref[...].astype(F32)
    for h0 in range(0, nh, A_HEAD_STAGE):
        hs = range(h0, h0 + A_HEAD_STAGE)
        s = [_dot_nt(k, q_ref[:, h * LANES:(h + 1) * LANES]) + bias for h in hs]
        ps, alphas = [], []
        for n, h in enumerate(hs):
            m_old = m_ref[h:h + 1, :]
            m_new = jnp.maximum(m_old, jnp.max(s[n], axis=0, keepdims=True))
            alpha = jnp.exp2(m_old - m_new)
            p = jnp.exp2(s[n] - m_new)
            l_ref[h:h + 1, :] = alpha * l_ref[h:h + 1, :] + jnp.sum(p, axis=0, keepdims=True)
            m_ref[h:h + 1, :] = m_new
            ps.append(p.astype(BF16))
            alphas.append(alpha)
        for n, h in enumerate(hs):
            rows = slice(h * dv, (h + 1) * dv)
            acc_ref[rows, :] = alphas[n] * acc_ref[rows, :] + _dot(v_t, ps[n])

    @pl.when(j == (i * tq + tq - 1) // kb)
    def _fin():
        for h in range(nh):
            sl = slice(h * LANES, (h + 1) * LANES)
            o_t = acc_ref[h * dv:(h + 1) * dv, :] * (1.0 / l_ref[h:h + 1, :])
            o_ref[:, sl] = (o_t.T * z_ref[:, sl].astype(F32)).astype(o_ref.dtype)


def _dsa_attn(q, z, aux, bias, batch, seq):
    n = q.shape[0]
    c_k = A_IDX_HEADS * A_IDX_DIM // LANES
    tq = bias.shape[-1]
    kb = min(A_K_BLOCK, seq)
    nq, nk = seq // tq, seq // kb
    pairs = [(i, j) for i in range(nq) for j in range((i * tq + tq - 1) // kb + 1)]
    qi = jnp.asarray([p[0] for p in pairs], jnp.int32)
    kj = jnp.asarray([p[1] for p in pairs], jnp.int32)
    grid_spec = pltpu.PrefetchScalarGridSpec(
        num_scalar_prefetch=2,
        grid=(batch, len(pairs)),
        in_specs=[
            pl.BlockSpec((tq, D_INNER), lambda b, t, qi, kj: (b * nq + qi[t], 0)),
            pl.BlockSpec((tq, D_INNER), lambda b, t, qi, kj: (b * nq + qi[t], 0)),
            pl.BlockSpec((kb, LANES), lambda b, t, qi, kj: (b * nk + kj[t], c_k)),
            pl.BlockSpec((kb, LANES), lambda b, t, qi, kj: (b * nk + kj[t], c_k + 1)),
            pl.BlockSpec((None, None, kb, tq), lambda b, t, qi, kj: (b, qi[t], kj[t], 0)),
        ],
        out_specs=pl.BlockSpec((tq, D_INNER), lambda b, t, qi, kj: (b * nq + qi[t], 0)),
        scratch_shapes=[
            pltpu.VMEM((A_HEADS, tq), F32),
            pltpu.VMEM((A_HEADS, tq), F32),
            pltpu.VMEM((A_HEADS * A_HEAD_DIM, tq), F32),
        ],
    )
    return pl.pallas_call(
        functools.partial(_dsa_attn_body, tq=tq, kb=kb),
        grid_spec=grid_spec,
        out_shape=jax.ShapeDtypeStruct((n, D_INNER), BF16),
        compiler_params=_cparams(("parallel", "arbitrary")),
        name="dsa_masked_attention",
    )(qi, kj, q, z, aux, aux, bias)


def _mixer_a(h, g, w_in, w_out, batch, seq, tabs, zero_aux):
    hd = A_HEADS * A_HEAD_DIM
    hi = A_IDX_HEADS * A_IDX_DIM
    o0 = np.cumsum([0, hd, A_HEAD_DIM, A_HEAD_DIM, hi, A_IDX_DIM, A_IDX_HEADS, D_INNER])
    wq, wk, wv, wiq, wik, wiw, wz = (w_in[:, o0[t]:o0[t + 1]] for t in range(7))
    pad = jnp.zeros((D_MODEL, LANES - A_IDX_HEADS), F32)
    w_aux = jnp.concatenate([wiq, wk, wv, wik, wik, wiw, pad], axis=1).astype(BF16)
    n_iq = hi // LANES
    q = _proj(h, g, wq.astype(BF16), "rot128", seq, tabs, scales=[A_HEAD_DIM ** -0.5 * LOG2E] * A_HEADS)
    z = _proj(h, g, wz.astype(BF16), "silu", seq, tabs)
    aux = _proj(h, g, w_aux, ["rot64"] * n_iq + ["rot128", "none", "rot64", "none"], seq, tabs,
                scales=[1.0] * (n_iq + 3) + [A_IDX_HEADS ** -0.5 * A_IDX_DIM ** -0.5])
    bias = _dsa_index(aux, batch, seq, min(A_TOPK_MAX, seq // 4))
    return _dsa_attn(q, z, aux, bias, batch, seq)


def _band_body(*refs, blk, band_prev, cfg_prev, final):
    refs = list(refs)
    q_ref, kc_ref, vc_ref = refs[:3]
    del refs[:3]
    if band_prev:
        kp_ref, vp_ref = refs[-2:]
        del refs[-2:]

        @pl.when(pl.program_id(2) == 0)
        def _no_prev():
            kp_ref[...] = jnp.zeros(kp_ref.shape, kp_ref.dtype)
            vp_ref[...] = jnp.zeros(vp_ref.shape, vp_ref.dtype)
    nb = q_ref.shape[0]
    if cfg_prev:
        op4_ref, lp4_ref = refs[:2]
        del refs[:2]
        op_ref, lp_ref = refs[-2:]
        del refs[-2:]
        for g in range(nb):
            for a in range(B_DIL_RATIO):
                rows = pl.ds(a, blk // B_DIL_RATIO, stride=B_DIL_RATIO)
                lp_ref[g, rows, :] = lp4_ref[a, g]
                for h in range(B_HEADS):
                    op_ref[g, h, rows, :] = op4_ref[a, g, :, h * LANES:(h + 1) * LANES].astype(F32)
    if final:
        z_ref, o_ref = refs
    else:
        o_ref, l_ref = refs
    ki = lax.broadcasted_iota(jnp.int32, (blk, blk), 0)
    qi = lax.broadcasted_iota(jnp.int32, (blk, blk), 1)
    mask_cur = ki <= qi
    eye = ki == qi
    if band_prev:
        mask_prev = (ki >= qi) & (pl.program_id(2) > 0)
    if cfg_prev:
        lse_prev_t = [_pad_rows(lp_ref[g].T, LANES) for g in range(nb)]
    items = [(g, slice(h * LANES, (h + 1) * LANES)) for g in range(nb) for h in range(B_HEADS)]
    s_c = [jnp.where(mask_cur, _dot_nt(kc_ref[g, :, sl], q_ref[g, :, sl]), NEG_BIG) for g, sl in items]
    if band_prev:
        s_p = [jnp.where(mask_prev, _dot_nt(kp_ref[g, :, sl], q_ref[g, :, sl]), NEG_BIG) for g, sl in items]
    lse_rows, p_c, p_p, d_old = [], [], [], []
    for n, (g, sl) in enumerate(items):
        m = jnp.max(s_c[n], axis=0, keepdims=True)
        if band_prev:
            m = jnp.maximum(m, jnp.max(s_p[n], axis=0, keepdims=True))
        e_c = jnp.exp2(s_c[n] - m)
        den = jnp.sum(e_c, axis=0, keepdims=True)
        if band_prev:
            e_p = jnp.exp2(s_p[n] - m)
            den += jnp.sum(e_p, axis=0, keepdims=True)
        lse = m + jnp.log2(den)
        scale = 1.0 / den
        if cfg_prev:
            h = n % B_HEADS
            lse_prev = lse_prev_t[g][h:h + 1, :]
            m2 = jnp.maximum(lse, lse_prev)
            w_new, w_old = jnp.exp2(lse - m2), jnp.exp2(lse_prev - m2)
            scale = scale * w_new / (w_new + w_old)
            d_old.append(jnp.where(eye, w_old / (w_new + w_old), 0.0).astype(BF16))
            lse = m2 + jnp.log2(w_new + w_old)
        p_c.append((e_c * scale).astype(BF16))
        if band_prev:
            p_p.append((e_p * scale).astype(BF16))
        lse_rows.append(lse)
    for n, (g, sl) in enumerate(items):
        o = _dot_tn(p_c[n], vc_ref[g, :, sl])
        if band_prev:
            o += _dot_tn(p_p[n], vp_ref[g, :, sl])
        if cfg_prev:
            o += _dot(d_old[n], op_ref[g, n % B_HEADS].astype(BF16))
        if final:
            o_ref[g, :, sl] = (o * z_ref[g, :, sl].astype(F32)).astype(o_ref.dtype)
        else:
            o_ref[g, :, sl] = o.astype(o_ref.dtype)
    if not final:
        for g in range(nb):
            lse_t = _pad_rows(jnp.concatenate(lse_rows[g * B_HEADS:(g + 1) * B_HEADS], axis=0), LANES)
            l_ref[g] = lse_t.T
    if band_prev:
        kp_ref[...] = kc_ref[...]
        vp_ref[...] = vc_ref[...]


def _pad_rows(x, rows):
    if x.shape[0] == rows:
        return x
    return jnp.concatenate([x, jnp.zeros((rows - x.shape[0], x.shape[1]), x.dtype)], axis=0)


def _band_attn(qk, v, gate, batch, seq, dil, prev):
    final = gate is not None
    length = seq // dil
    blk = min(128, length)
    nbl = length // blk
    band_prev = nbl > 1
    nb = B_BATCH_GROUP if batch % B_BATCH_GROUP == 0 else 1

    def per_batch(x, classes=dil):
        return x.reshape(classes, batch, x.shape[1] // batch, x.shape[2])

    def spec(col, width=D_INNER):
        return pl.BlockSpec((None, nb, blk, width), lambda b, r, i: (r, b, i, col))

    o_spec, l_spec = spec(0), spec(0, LANES)
    in_specs, args, scratch = [spec(0), spec(1), spec(0)], [per_batch(qk), per_batch(qk), per_batch(v)], []
    if prev is not None:
        coarse = B_DIL_RATIO * dil
        assert prev[0].shape[0] == coarse and blk % (16 * B_DIL_RATIO) == 0
        pblk = blk // B_DIL_RATIO
        in_specs += [pl.BlockSpec((B_DIL_RATIO, None, nb, pblk, D_INNER), lambda b, r, i: (0, r, b, i, 0)),
                     pl.BlockSpec((B_DIL_RATIO, None, nb, pblk, LANES), lambda b, r, i: (0, r, b, i, 0))]
        args += [x.reshape(B_DIL_RATIO, dil, batch, length // B_DIL_RATIO, x.shape[-1]) for x in prev]
        scratch = [pltpu.VMEM((nb, B_HEADS, blk, LANES), F32), pltpu.VMEM((nb, blk, LANES), F32)]
    if band_prev:
        scratch += [pltpu.VMEM((nb, blk, D_INNER), BF16), pltpu.VMEM((nb, blk, D_INNER), BF16)]
    o_shape = jax.ShapeDtypeStruct((dil, batch, length, D_INNER), BF16)
    if final:
        in_specs.append(spec(0))
        args.append(per_batch(gate))
        out_specs, out_shape = o_spec, o_shape
    else:
        out_specs = [o_spec, l_spec]
        out_shape = [o_shape, jax.ShapeDtypeStruct((dil, batch, length, LANES), F32)]
    res = pl.pallas_call(
        functools.partial(_band_body, blk=blk, band_prev=band_prev, cfg_prev=prev is not None, final=final),
        grid=(batch // nb, dil, nbl),
        in_specs=in_specs,
        out_specs=out_specs,
        out_shape=out_shape,
        scratch_shapes=scratch,
        compiler_params=_cparams(("parallel", "parallel", "arbitrary")),
        name="dilated_band_attention",
    )(*args)
    if final:
        return res.reshape(dil, batch * length, D_INNER)
    return [x.reshape(dil, batch * length, x.shape[-1]) for x in res]


def _mixer_b(h, g, w_in, w_out, batch, seq, tabs, zero_aux):
    configs = sorted(B_CONFIGS, key=lambda wd: -wd[1])
    dils = [dil for _, dil in configs]
    assert dils[-1] == 1 and all(a == B_DIL_RATIO * b for a, b in zip(dils, dils[1:]))
    assert all(window // dil == 128 and seq % dil == 0 for window, dil in configs)
    w = w_in.astype(BF16)
    qk = _proj(h, g, w[:, :2 * D_INNER], "rot128", seq, tabs, res_dils=dils[:-1],
               scales=[B_HEAD_DIM ** -0.5 * LOG2E] * B_HEADS + [1.0] * B_HEADS)
    v = _proj(h, g, w[:, 2 * D_INNER:3 * D_INNER], "none", seq, tabs, res_dils=dils[:-1])
    z = _proj(h, g, w[:, 3 * D_INNER:], "silu", seq, tabs)
    qks = list(qk[1:]) + [qk[0][None]]
    vs = list(v[1:]) + [v[0][None]]
    prev = None
    for qk_d, v_d, dil in zip(qks, vs, dils):
        prev = _band_attn(qk_d, v_d, z[None] if dil == 1 else None, batch, seq, dil, prev)
    return prev.reshape(-1, D_INNER)


def _retention_body(lg_ref, q_ref, k_ref, v_ref, z_ref, o_ref, st_ref, *, seq):
    c = C_CHUNK
    lg = lg_ref[pl.program_id(1)]
    ri = lax.broadcasted_iota(jnp.int32, (c, c), 0)
    ci = lax.broadcasted_iota(jnp.int32, (c, c), 1)
    rel = (ri - ci).astype(F32)
    decay = jnp.where(rel >= 0, jnp.exp(jnp.maximum(rel, 0.0) * lg), 0.0)
    idx = lax.broadcasted_iota(jnp.int32, (c, 1), 0).astype(F32)
    k_decay = jnp.exp((c - 1 - idx) * lg)
    q_decay = jnp.exp((idx + 1.0) * lg)
    chunk_decay = jnp.exp(jnp.full((1, C_QK_DIM), c, F32) * lg)
    st_ref[...] = jnp.zeros(st_ref.shape, F32)

    unroll = min(C_UNROLL, seq // c)

    def group(gidx, carry):
        rows0 = [pl.multiple_of((gidx * unroll + u) * c, c) for u in range(unroll)]
        qs = [q_ref[pl.ds(r0, c), :] for r0 in rows0]
        ks = [k_ref[pl.ds(r0, c), :] for r0 in rows0]
        vs = [v_ref[pl.ds(r0, c), :] for r0 in rows0]
        scores = [(_dot_nt(q, k) * decay).astype(BF16) for q, k in zip(qs, ks)]
        o_intra = [_dot(s, v) for s, v in zip(scores, vs)]
        q_dec = [(q.astype(F32) * q_decay).astype(BF16) for q in qs]
        k_dec = [(k.astype(F32) * k_decay).astype(BF16) for k in ks]
        outs = []
        for u in range(unroll):
            st = st_ref[...]
            outs.append(o_intra[u] + _dot_nt(q_dec[u], st.astype(BF16)))
            st_ref[...] = st * chunk_decay + _dot_tn(vs[u], k_dec[u])
        for u, r0 in enumerate(rows0):
            o = outs[u]
            o = o * lax.rsqrt(jnp.mean(o * o, axis=-1, keepdims=True) + NORM_EPS)
            o_ref[pl.ds(r0, c), :] = (o * z_ref[pl.ds(r0, c), :].astype(F32)).astype(o_ref.dtype)
        return carry

    lax.fori_loop(0, seq // (c * unroll), group, 0)


def _mixer_c(h, g, w_in, w_out, batch, seq, tabs, zero_aux):
    w = w_in.astype(BF16)
    nh = C_HEADS
    dqk = 2 * nh * C_QK_DIM
    qk = _proj(h, g, w[:, :dqk], "rot128", seq, tabs, scales=[1.0] * nh + [C_QK_DIM ** -0.5] * nh)
    v = _proj(h, g, w[:, dqk:dqk + D_INNER], "none", seq, tabs)
    z = _proj(h, g, w[:, dqk + D_INNER:], "silu", seq, tabs)
    n = qk.shape[0]
    log_gamma = jnp.log1p(-jnp.exp2(-5.0 - jnp.arange(nh, dtype=F32)))
    return pl.pallas_call(
        functools.partial(_retention_body, seq=seq),
        grid=(batch, nh),
        in_specs=[
            pl.BlockSpec(memory_space=pltpu.SMEM),
            pl.BlockSpec((seq, C_QK_DIM), lambda b, h: (b, h)),
            pl.BlockSpec((seq, C_QK_DIM), lambda b, h: (b, nh + h)),
            pl.BlockSpec((seq, C_V_DIM), lambda b, h: (b, h)),
            pl.BlockSpec((seq, C_V_DIM), lambda b, h: (b, h)),
        ],
        out_specs=pl.BlockSpec((seq, C_V_DIM), lambda b, h: (b, h)),
        out_shape=jax.ShapeDtypeStruct((n, D_INNER), BF16),
        scratch_shapes=[pltpu.VMEM((C_V_DIM, C_QK_DIM), F32)],
        compiler_params=_cparams(("parallel", "parallel")),
        name="retention",
    )(log_gamma, qk, qk, v, z)


def _hgrn_body(q_ref, g_ref, v_ref, z_ref, o_ref, st_ref, *, seq):
    c = D_CHUNK
    w = D_HEAD_GROUP * LANES
    halves = [D_SUB * 2 ** lv for lv in range(16) if D_SUB * 2 ** (lv + 1) <= c]
    ri = lax.broadcasted_iota(jnp.int32, (c, c), 0)
    ci = lax.broadcasted_iota(jnp.int32, (c, c), 1)
    mats = [ri >= ci]
    pair_masks = []
    for half in halves:
        block = 2 * half
        mid = (ri // block) * block + half - 1
        right = (ri % block) >= half
        mats.append((right & (ci > mid) & (ci <= ri)) | (jnp.logical_not(right) & (ci > ri) & (ci <= mid)))
        pair_masks.append(((ri // block) == (ci // block)) & right & ((ci % block) < half))
    diag_masks = []
    for j in range(D_SUB):
        key = (ri // D_SUB) * D_SUB + j
        mats.append((ci > key) & (ci <= ri))
        diag_masks.append((ci == key) & (ri >= key))
    m1 = jnp.concatenate([m.astype(BF16) for m in mats], axis=0)
    m3 = jnp.concatenate([m1, m1, m1], axis=1)
    rows = lax.broadcasted_iota(jnp.int32, (c, w), 0)
    right_rows = [(rows % (2 * half)) >= half for half in halves]
    n_lv = len(halves)
    st_ref[...] = jnp.zeros(st_ref.shape, F32)

    head_sl = [slice(hd * LANES, (hd + 1) * LANES) for hd in range(D_HEAD_GROUP)]
    unroll = min(D_UNROLL, seq // c)

    def group_rows(gidx):
        return [pl.multiple_of((gidx * unroll + u) * c, c) for u in range(unroll)]

    def prepare(gidx):
        rows0 = group_rows(gidx)
        gates = [g_ref[pl.ds(r0, c), :] for r0 in rows0]
        bds = []
        for gate in gates:
            lf = jnp.log2(gate)
            hi = lf.astype(BF16)
            r1 = lf - hi.astype(F32)
            mid_part = r1.astype(BF16)
            lo = (r1 - mid_part.astype(F32)).astype(BF16)
            bds.append(_dot(m3, jnp.concatenate([hi, mid_part, lo], axis=0)))
        prep = []
        for u, r0 in enumerate(rows0):
            bd, kk = bds[u], 1.0 - gates[u]
            b = bd[0:c]
            q = q_ref[pl.ds(r0, c), :].astype(F32)
            v = v_ref[pl.ds(r0, c), :]
            qe = (q * jnp.exp2(b)).astype(BF16)
            qts, kts = [], []
            for lv in range(len(halves)):
                e = jnp.exp2(bd[(lv + 1) * c:(lv + 2) * c])
                qts.append(jnp.where(right_rows[lv], q * e, 0.0).astype(BF16))
                kts.append(jnp.where(right_rows[lv], 0.0, kk * e).astype(BF16))
            q_diag = jnp.concatenate(
                [(q * jnp.exp2(bd[(1 + n_lv + j) * c:(2 + n_lv + j) * c])).astype(BF16) for j in range(D_SUB)], axis=0)
            b_last = b[c - 1:c, :]
            k_dec = (kk * jnp.exp2(b_last - b)).astype(BF16)
            prep.append((qe, qts, kts, v, (q_diag, kk.astype(BF16)), k_dec, jnp.exp2(b_last)))
        o_local = []
        for qe, qts, kts, v, (q_diag, kk16), k_dec, eb_last in prep:
            per_head = []
            for hd, sl in enumerate(head_sl):
                a = jnp.zeros((c, c), F32)
                for lv in range(n_lv):
                    a = a + jnp.where(pair_masks[lv], _dot_nt(qts[lv][:, sl], kts[lv][:, sl]), 0.0)
                g_diag = _dot_nt(q_diag[:, sl], kk16[:, sl])
                for j in range(D_SUB):
                    a = a + jnp.where(diag_masks[j], g_diag[j * c:(j + 1) * c], 0.0)
                per_head.append(_dot(a.astype(BF16), v[:, sl]))
            o_local.append(per_head)
        return [(o_local[u], p[0], p[3], p[5], p[6]) for u, p in enumerate(prep)]

    def finish(gidx, prepared):
        outs = []
        for o_loc, qe, v, k_dec, eb_last in prepared:
            per_head = []
            for hd, sl in enumerate(head_sl):
                st = st_ref[hd]
                per_head.append(o_loc[hd] + _dot_nt(qe[:, sl], st.astype(BF16)))
                st_ref[hd] = st * eb_last[:, sl] + _dot_tn(v[:, sl], k_dec[:, sl])
            outs.append(per_head)
        for u, r0 in enumerate(group_rows(gidx)):
            for hd, sl in enumerate(head_sl):
                o = outs[u][hd]
                o = o * lax.rsqrt(jnp.mean(o * o, axis=-1, keepdims=True) + NORM_EPS)
                o_ref[pl.ds(r0, c), sl] = (o * z_ref[pl.ds(r0, c), sl].astype(F32)).astype(o_ref.dtype)

    def trip(gidx, carry):
        finish(gidx, prepare(gidx))
        return carry

    lax.fori_loop(0, seq // (c * unroll), trip, 0)


def _mixer_d(h, g, w_in, w_out, lower_bound, batch, seq, tabs, zero_aux):
    wq, wf, wi, wz = (w_in[:, t * D_INNER:(t + 1) * D_INNER] for t in range(4))
    qz = _proj(h, g, jnp.concatenate([wq, wz], axis=1).astype(BF16), "silu", seq, tabs,
               scales=[D_K_DIM ** -0.5] * D_HEADS + [1.0] * D_HEADS)
    v = _proj(h, g, wi.astype(BF16), "none", seq, tabs)
    gates = _proj(h, g, wf.astype(BF16), "hgrn_g", seq, tabs, out_dtype=F32,
                  aux=lower_bound.reshape(1, D_INNER).astype(F32))
    n = qz.shape[0]
    ng = D_HEADS // D_HEAD_GROUP
    w = D_HEAD_GROUP * LANES
    return pl.pallas_call(
        functools.partial(_hgrn_body, seq=seq),
        grid=(batch, ng),
        in_specs=[
            pl.BlockSpec((seq, w), lambda b, h: (b, h)),
            pl.BlockSpec((seq, w), lambda b, h: (b, h)),
            pl.BlockSpec((seq, w), lambda b, h: (b, h)),
            pl.BlockSpec((seq, w), lambda b, h: (b, ng + h)),
        ],
        out_specs=pl.BlockSpec((seq, w), lambda b, h: (b, h)),
        out_shape=jax.ShapeDtypeStruct((n, D_INNER), BF16),
        scratch_shapes=[pltpu.VMEM((D_HEAD_GROUP, D_V_DIM, D_K_DIM), F32)],
        compiler_params=_cparams(("parallel", "parallel")),
        name="hgrn2_recurrence",
    )(qz, gates, v, qz)


def kernel(x, norm_g, final_g, a_w_in, a_w_out, b_w_in, b_w_out, c_w_in, c_w_out, d_w_in, d_w_out, hgrn_lb_logits):
    batch, seq, d = x.shape
    depth = norm_g.shape[0]
    lb_cum = jnp.cumsum(jax.nn.softmax(hgrn_lb_logits.astype(F32), axis=0), axis=0)
    lower_bounds = lb_cum - lb_cum[0]
    tabs = _rope_tables(seq, 128) + _rope_tables(seq, 64)
    h = x.reshape(batch * seq, d)
    for layer in range(depth):
        kind, slot = layer % N_MIXERS, layer // N_MIXERS
        g = norm_g[layer]
        if kind == 0:
            og, w_out = _mixer_a(h, g, a_w_in[slot], a_w_out[slot], batch, seq, tabs, None), a_w_out[slot]
        elif kind == 1:
            og, w_out = _mixer_b(h, g, b_w_in[slot], b_w_out[slot], batch, seq, tabs, None), b_w_out[slot]
        elif kind == 2:
            og, w_out = _mixer_c(h, g, c_w_in[slot], c_w_out[slot], batch, seq, tabs, None), c_w_out[slot]
        else:
            og, w_out = _mixer_d(h, g, d_w_in[slot], d_w_out[slot], lower_bounds[layer], batch, seq, tabs, None), d_w_out[slot]
        h = _outproj(og, w_out.astype(BF16), h, final_g, final=(layer == depth - 1))
    return h.reshape(batch, seq, d)
```

```python
import functools

import jax
import jax.numpy as jnp
import numpy as np
from jax import lax
from jax.experimental import pallas as pl
from jax.experimental.pallas import tpu as pltpu

F32 = jnp.float32
BF16 = jnp.bfloat16

D_MODEL = 1024
D_INNER = 2048
N_MIXERS = 4
NORM_EPS = 1e-6
ROPE_THETA = 10000.0
LANES = 128
NEG_BIG = -1e30
LOG2E = 1.4426950408889634

A_HEADS, A_HEAD_DIM = 16, 128
A_IDX_HEADS, A_IDX_DIM = 16, 64
A_TOPK_MAX = 256
A_Q_BLOCK = 256
A_K_BLOCK = 512
A_BATCH_GROUP = 2
A_CNT_ROWS = 64
A_HEAD_STAGE = 16
B_HEADS, B_HEAD_DIM = 16, 128
B_CONFIGS = ((128, 1), (512, 4), (2048, 16))
B_DIL_RATIO = 4
B_BATCH_GROUP = 4
C_HEADS, C_QK_DIM, C_V_DIM, C_CHUNK = 8, 128, 256, 128
C_UNROLL = 4
D_HEADS, D_K_DIM, D_V_DIM = 16, 128, 128
D_CHUNK = 64
D_SUB = 2
D_HEAD_GROUP = 2
D_UNROLL = 4

PROJ_TILE_N = 2048
PROJ_TILE_N_RES = 1024
VMEM_LIMIT = 56 * 1024 * 1024


def _cparams(sem):
    return pltpu.CompilerParams(dimension_semantics=sem, vmem_limit_bytes=VMEM_LIMIT)


def _dot(a, b):
    return jnp.dot(a, b, preferred_element_type=F32)


def _dot_nt(a, b):
    return lax.dot_general(a, b, (((1,), (1,)), ((), ())), preferred_element_type=F32)


def _dot_tn(a, b):
    return lax.dot_general(a, b, (((0,), (0,)), ((), ())), preferred_element_type=F32)


def _rope_tables(seq, dh):
    freqs = ROPE_THETA ** (-jnp.arange(0, dh, 2, dtype=F32) / dh)
    ang = jnp.arange(seq, dtype=F32)[:, None] * freqs[None, :]
    cos, sin = jnp.cos(ang), jnp.sin(ang)
    cosf = jnp.concatenate([cos, cos], axis=-1)
    sinf = jnp.concatenate([-sin, sin], axis=-1)
    rep = LANES // dh
    return jnp.tile(cosf, (1, rep)), jnp.tile(sinf, (1, rep))


def _epilogue(kind, scale, x, tabs, aux):
    c128, s128, c64, s64 = tabs
    if kind == "rot128":
        y = x * c128 + pltpu.roll(x, 64, 1) * s128
    elif kind == "rot64":
        lane = lax.broadcasted_iota(jnp.int32, x.shape, 1)
        partner = jnp.where((lane % 64) < 32, pltpu.roll(x, 96, 1), pltpu.roll(x, 32, 1))
        y = x * c64 + partner * s64
    elif kind == "silu":
        y = x * jax.nn.sigmoid(x)
    elif kind == "hgrn_g":
        y = aux + (1.0 - aux) * jax.nn.sigmoid(x)
    else:
        y = x
    if scale is not None:
        y = y * scale
    return y


def _proj_body(h_ref, g_ref, w_ref, c128_ref, s128_ref, c64_ref, s64_ref, aux_ref, scale_ref, *rest,
               kinds, scaled, res_dils):
    o_ref = rest[0]
    res_refs = rest[1:1 + len(res_dils)]
    u_ref = rest[1 + len(res_dils)]
    tm = o_ref.shape[0]

    @pl.when(pl.program_id(1) == 0)
    def _norm():
        x = h_ref[...]
        ms = jnp.mean(x * x, axis=-1, keepdims=True)
        u_ref[...] = (x * lax.rsqrt(ms + NORM_EPS) * g_ref[...]).astype(BF16)

    tabs = (c128_ref[...] if "rot128" in kinds else None, s128_ref[...] if "rot128" in kinds else None,
            c64_ref[...] if "rot64" in kinds else None, s64_ref[...] if "rot64" in kinds else None)
    def pair_dot(p):
        return _dot(u_ref[...], w_ref[:, 2 * p * LANES:min(2 * p + 2, len(kinds)) * LANES])

    n_pairs = (len(kinds) + 1) // 2
    accs = {0: pair_dot(0)}
    for c, kind in enumerate(kinds):
        sl = slice(c * LANES, (c + 1) * LANES)
        aux = aux_ref[:, sl] if kind == "hgrn_g" else None
        if c % 2 == 0 and c // 2 + 1 < n_pairs:
            accs[c // 2 + 1] = pair_dot(c // 2 + 1)
        acc = accs[c // 2][:, (c % 2) * LANES:(c % 2 + 1) * LANES]
        y = _epilogue(kind, scale_ref[:, sl] if scaled else None, acc, tabs, aux)
        o_ref[:, sl] = y.astype(o_ref.dtype)
        if res_dils:
            y_ref = rest[-1]
            y_ref[c] = y
            for dil, r_ref in zip(res_dils, res_refs):
                for r in range(dil):
                    r_ref[r, :, sl] = y_ref[c, pl.ds(r, tm // dil, stride=dil), :].astype(r_ref.dtype)


def _proj(h, g, w, kinds, seq, tabs, *, scales=None, aux=None, out_dtype=None, res_dils=()):
    n, d = h.shape
    c = w.shape[1]
    tm = min(1024, seq)
    tn = min(PROJ_TILE_N_RES if res_dils else PROJ_TILE_N, c)
    if isinstance(kinds, str):
        kinds = (kinds,) * (tn // LANES)
    assert len(kinds) == tn // LANES and (c == tn or len(set(kinds)) == 1)
    if scales is None:
        scale_row = jnp.zeros((1, c), F32)
    else:
        scale_row = jnp.repeat(jnp.asarray(scales, F32), LANES).reshape(1, c)
    if aux is None:
        aux = jnp.zeros((1, c), F32)
    out_dtype = out_dtype or BF16
    pos_blocks = seq // tm
    tab_spec = pl.BlockSpec((tm, LANES), lambda i, j: (i % pos_blocks, 0))
    row_spec = pl.BlockSpec((1, tn), lambda i, j: (0, j))
    out_specs = [pl.BlockSpec((tm, tn), lambda i, j: (i, j))]
    out_shape = [jax.ShapeDtypeStruct((n, c), out_dtype)]
    scratch = [pltpu.VMEM((tm, d), BF16)]
    for dil in res_dils:
        out_specs.append(pl.BlockSpec((dil, tm // dil, tn), lambda i, j: (0, i, j)))
        out_shape.append(jax.ShapeDtypeStruct((dil, n // dil, c), out_dtype))
    if res_dils:
        scratch.append(pltpu.VMEM((tn // LANES, tm, LANES), F32))
    res = pl.pallas_call(
        functools.partial(_proj_body, kinds=tuple(kinds), scaled=scales is not None, res_dils=tuple(res_dils)),
        grid=(n // tm, c // tn),
        in_specs=[
            pl.BlockSpec((tm, d), lambda i, j: (i, 0)),
            pl.BlockSpec((1, d), lambda i, j: (0, 0)),
            pl.BlockSpec((d, tn), lambda i, j: (0, j)),
            tab_spec, tab_spec, tab_spec, tab_spec,
            row_spec, row_spec,
        ],
        out_specs=out_specs,
        out_shape=out_shape,
        scratch_shapes=scratch,
        compiler_params=_cparams(("parallel", "arbitrary")),
        name="norm_in_proj",
    )(h, g.reshape(1, d), w, *tabs, aux, scale_row)
    return res if res_dils else res[0]


def _outproj_body(og_ref, w_ref, h_ref, gf_ref, o_ref, *, final):
    y = h_ref[...] + _dot(og_ref[...], w_ref[...])
    if final:
        ms = jnp.mean(y * y, axis=-1, keepdims=True)
        y = y * lax.rsqrt(ms + NORM_EPS) * gf_ref[...]
    o_ref[...] = y


def _outproj(og, w, h, gf, final):
    n, di = og.shape
    d = w.shape[1]
    tm = min(1024, n)
    return pl.pallas_call(
        functools.partial(_outproj_body, final=final),
        grid=(n // tm,),
        in_specs=[
            pl.BlockSpec((tm, di), lambda i: (i, 0)),
            pl.BlockSpec((di, d), lambda i: (0, 0)),
            pl.BlockSpec((tm, d), lambda i: (i, 0)),
            pl.BlockSpec((1, d), lambda i: (0, 0)),
        ],
        out_specs=pl.BlockSpec((tm, d), lambda i: (i, 0)),
        out_shape=jax.ShapeDtypeStruct((n, d), F32),
        compiler_params=_cparams(("parallel",)),
        name="out_proj_residual",
    )(og, w, h, gf.reshape(1, d))


def _dsa_index_body(iq_ref, ik_ref, iw_ref, bias_ref, sc_ref, *, topk, tq, kc, seq):
    i = pl.program_id(1)
    n_chunks = ((i + 1) * tq + kc - 1) // kc
    int_min = jnp.int32(-2 ** 31)
    iw_t = iw_ref[...].astype(F32).T
    lane = lax.broadcasted_iota(jnp.int32, (kc, LANES), 1)
    s_loc = lax.broadcasted_iota(jnp.int32, (kc, tq), 0)
    t_pos = i * tq + lax.broadcasted_iota(jnp.int32, (kc, tq), 1)

    def score_chunk(c, carry):
        k0 = pl.multiple_of(c * kc, kc)
        ik2 = ik_ref[pl.ds(k0, kc), :]
        ik_lo = jnp.where(lane < A_IDX_DIM, ik2, jnp.zeros_like(ik2))
        ik_hi = jnp.where(lane >= A_IDX_DIM, ik2, jnp.zeros_like(ik2))
        score = jnp.zeros((kc, tq), F32)
        for p in range(A_IDX_HEADS // 2):
            iq2 = iq_ref[:, p * LANES:(p + 1) * LANES]
            score += jnp.maximum(_dot_nt(ik_lo, iq2), 0.0) * iw_t[2 * p:2 * p + 1, :]
            score += jnp.maximum(_dot_nt(ik_hi, iq2), 0.0) * iw_t[2 * p + 1:2 * p + 2, :]
        sc_ref[pl.ds(k0, kc), :] = jnp.where(k0 + s_loc <= t_pos, score, -jnp.inf)
        return carry

    lax.fori_loop(0, n_chunks, score_chunk, 0)

    def threshold(prefix):
        key = prefix ^ int_min
        return pltpu.bitcast(jnp.where(key < 0, key ^ jnp.int32(0x7FFFFFFF), key), F32)

    def bisect(it, prefix):
        cand = prefix | (jnp.int32(1) << (31 - it))
        thr = threshold(cand)

        def count(c, part):
            blk = sc_ref[pl.ds(pl.multiple_of(c * kc, kc), kc), :]
            hits = jnp.where(blk >= thr, 1.0, 0.0).reshape(kc // A_CNT_ROWS, A_CNT_ROWS, tq)
            return part + jnp.sum(hits, axis=0)

        part = lax.fori_loop(0, n_chunks, count, jnp.zeros((A_CNT_ROWS, tq), F32))
        cnt = jnp.sum(part, axis=0, keepdims=True)
        return jnp.where(cnt >= topk, cand, prefix)

    n_iter = jnp.where((i + 1) * tq <= topk, 0, 32)
    prefix = lax.fori_loop(0, n_iter, bisect, jnp.zeros((1, tq), jnp.int32))
    thr = threshold(prefix)
    take_all = prefix == 0

    def emit(c, carry):
        k0 = pl.multiple_of(c * kc, kc)
        sel = ((sc_ref[pl.ds(k0, kc), :] >= thr) | take_all) & (k0 + s_loc <= t_pos)
        bias_ref[pl.ds(k0, kc), :] = jnp.where(sel, 0.0, NEG_BIG).astype(bias_ref.dtype)
        return carry

    lax.fori_loop(0, n_chunks, emit, 0)

    def emit_masked(c, carry):
        bias_ref[pl.ds(pl.multiple_of(c * kc, kc), kc), :] = jnp.full((kc, tq), NEG_BIG, bias_ref.dtype)
        return carry

    lax.fori_loop(n_chunks, seq // kc, emit_masked, 0)


def _dsa_index(aux, batch, seq, topk):
    c_ik = A_IDX_HEADS * A_IDX_DIM // LANES + 2
    tq = min(A_Q_BLOCK, seq)
    kc = min(A_K_BLOCK, seq)
    nq = seq // tq
    return pl.pallas_call(
        functools.partial(_dsa_index_body, topk=topk, tq=tq, kc=kc, seq=seq),
        grid=(batch, nq),
        in_specs=[
            pl.BlockSpec((tq, A_IDX_HEADS * A_IDX_DIM), lambda b, i: (b * nq + i, 0)),
            pl.BlockSpec((seq, LANES), lambda b, i: (b, c_ik)),
            pl.BlockSpec((tq, LANES), lambda b, i: (b * nq + i, c_ik + 1)),
        ],
        out_specs=pl.BlockSpec((None, None, seq, tq), lambda b, i: (b, i, 0, 0)),
        out_shape=jax.ShapeDtypeStruct((batch, nq, seq, tq), BF16),
        scratch_shapes=[pltpu.VMEM((seq, tq), F32)],
        compiler_params=_cparams(("parallel", "parallel")),
        name="dsa_indexer_topk_mask",
    )(aux, aux, aux)


def _dsa_attn_body(qi_ref, kj_ref, q_ref, z_ref, k_ref, v_ref, bias_ref, o_ref, m_ref, l_ref, acc_ref, *, tq, kb):
    t = pl.program_id(1)
    i = qi_ref[t]
    j = kj_ref[t]
    nh = A_HEADS
    dv = A_HEAD_DIM

    @pl.when(j == 0)
    def _init():
        m_ref[...] = jnp.full(m_ref.shape, NEG_BIG, F32)
        l_ref[...] = jnp.zeros(l_ref.shape, F32)
        acc_ref[...] = jnp.zeros(acc_ref.shape, F32)

    for g in range(q_ref.shape[0]):
        k = k_ref[g]
        v_t = v_ref[g].astype(F32).T.astype(BF16)
        bias = bias_ref[g].astype(F32)
        for h0 in range(0, nh, A_HEAD_STAGE):
            hs = range(h0, h0 + A_HEAD_STAGE)
            s = [_dot_nt(k, q_ref[g, :, h * LANES:(h + 1) * LANES]) + bias for h in hs]
            ps, alphas = [], []
            for n, h in enumerate(hs):
                m_old = m_ref[g, h:h + 1, :]
                m_new = jnp.maximum(m_old, jnp.max(s[n], axis=0, keepdims=True))
                alpha = jnp.exp2(m_old - m_new)
                p = jnp.exp2(s[n] - m_new)
                l_ref[g, h:h + 1, :] = alpha * l_ref[g, h:h + 1, :] + jnp.sum(p, axis=0, keepdims=True)
                m_ref[g, h:h + 1, :] = m_new
                ps.append(p.astype(BF16))
                alphas.append(alpha)
            for n, h in enumerate(hs):
                rows = slice(h * dv, (h + 1) * dv)
                acc_ref[g, rows, :] = alphas[n] * acc_ref[g, rows, :] + _dot(v_t, ps[n])

    @pl.when(j == (i * tq + tq - 1) // kb)
    def _fin():
        for g in range(q_ref.shape[0]):
            for h in range(nh):
                sl = slice(h * LANES, (h + 1) * LANES)
                o_t = acc_ref[g, h * dv:(h + 1) * dv, :] * (1.0 / l_ref[g, h:h + 1, :])
                o_ref[g, :, sl] = (o_t.T * z_ref[g, :, sl].astype(F32)).astype(o_ref.dtype)


def _dsa_attn(q, z, aux, bias, batch, seq):
    n = q.shape[0]
    c_k = A_IDX_HEADS * A_IDX_DIM // LANES
    tq = bias.shape[-1]
    kb = min(A_K_BLOCK, seq)
    nq, nk = seq // tq, seq // kb
    pairs = [(i, j) for i in range(nq) for j in range((i * tq + tq - 1) // kb + 1)]
    qi = jnp.asarray([p[0] for p in pairs], jnp.int32)
    kj = jnp.asarray([p[1] for p in pairs], jnp.int32)
    nb = A_BATCH_GROUP if batch % A_BATCH_GROUP == 0 else 1

    def per_batch(x):
        return x.reshape(batch, seq, x.shape[-1])

    grid_spec = pltpu.PrefetchScalarGridSpec(
        num_scalar_prefetch=2,
        grid=(batch // nb, len(pairs)),
        in_specs=[
            pl.BlockSpec((nb, tq, D_INNER), lambda b, t, qi, kj: (b, qi[t], 0)),
            pl.BlockSpec((nb, tq, D_INNER), lambda b, t, qi, kj: (b, qi[t], 0)),
            pl.BlockSpec((nb, kb, LANES), lambda b, t, qi, kj: (b, kj[t], c_k)),
            pl.BlockSpec((nb, kb, LANES), lambda b, t, qi, kj: (b, kj[t], c_k + 1)),
            pl.BlockSpec((nb, None, kb, tq), lambda b, t, qi, kj: (b, qi[t], kj[t], 0)),
        ],
        out_specs=pl.BlockSpec((nb, tq, D_INNER), lambda b, t, qi, kj: (b, qi[t], 0)),
        scratch_shapes=[
            pltpu.VMEM((nb, A_HEADS, tq), F32),
            pltpu.VMEM((nb, A_HEADS, tq), F32),
            pltpu.VMEM((nb, A_HEADS * A_HEAD_DIM, tq), F32),
        ],
    )
    return pl.pallas_call(
        functools.partial(_dsa_attn_body, tq=tq, kb=kb),
        grid_spec=grid_spec,
        out_shape=jax.ShapeDtypeStruct((batch, seq, D_INNER), BF16),
        compiler_params=_cparams(("parallel", "arbitrary")),
        name="dsa_masked_attention",
    )(qi, kj, per_batch(q), per_batch(z), per_batch(aux), per_batch(aux), bias).reshape(n, D_INNER)


def _mixer_a(h, g, w_in, w_out, batch, seq, tabs, zero_aux):
    hd = A_HEADS * A_HEAD_DIM
    hi = A_IDX_HEADS * A_IDX_DIM
    o0 = np.cumsum([0, hd, A_HEAD_DIM, A_HEAD_DIM, hi, A_IDX_DIM, A_IDX_HEADS, D_INNER])
    wq, wk, wv, wiq, wik, wiw, wz = (w_in[:, o0[t]:o0[t + 1]] for t in range(7))
    pad = jnp.zeros((D_MODEL, LANES - A_IDX_HEADS), F32)
    w_aux = jnp.concatenate([wiq, wk, wv, wik, wik, wiw, pad], axis=1).astype(BF16)
    n_iq = hi // LANES
    q = _proj(h, g, wq.astype(BF16), "rot128", seq, tabs, scales=[A_HEAD_DIM ** -0.5 * LOG2E] * A_HEADS)
    z = _proj(h, g, wz.astype(BF16), "silu", seq, tabs)
    aux = _proj(h, g, w_aux, ["rot64"] * n_iq + ["rot128", "none", "rot64", "none"], seq, tabs,
                scales=[1.0] * (n_iq + 3) + [A_IDX_HEADS ** -0.5 * A_IDX_DIM ** -0.5])
    bias = _dsa_index(aux, batch, seq, min(A_TOPK_MAX, seq // 4))
    return _dsa_attn(q, z, aux, bias, batch, seq)


def _band_body(*refs, blk, band_prev, cfg_prev, final):
    refs = list(refs)
    q_ref, kc_ref, vc_ref = refs[:3]
    del refs[:3]
    if band_prev:
        kp_ref, vp_ref = refs[-2:]
        del refs[-2:]

        @pl.when(pl.program_id(2) == 0)
        def _no_prev():
            kp_ref[...] = jnp.zeros(kp_ref.shape, kp_ref.dtype)
            vp_ref[...] = jnp.zeros(vp_ref.shape, vp_ref.dtype)
    nb = q_ref.shape[0]
    if cfg_prev:
        op4_ref, lp4_ref = refs[:2]
        del refs[:2]
        op_ref, lp_ref = refs[-2:]
        del refs[-2:]
        for g in range(nb):
            for a in range(B_DIL_RATIO):
                rows = pl.ds(a, blk // B_DIL_RATIO, stride=B_DIL_RATIO)
                lp_ref[g, rows, :] = lp4_ref[a, g]
                for h in range(B_HEADS):
                    op_ref[g, h, rows, :] = op4_ref[a, g, :, h * LANES:(h + 1) * LANES].astype(F32)
    if final:
        z_ref, o_ref = refs
    else:
        o_ref, l_ref = refs
    ki = lax.broadcasted_iota(jnp.int32, (blk, blk), 0)
    qi = lax.broadcasted_iota(jnp.int32, (blk, blk), 1)
    mask_cur = ki <= qi
    eye = ki == qi
    if band_prev:
        mask_prev = (ki >= qi) & (pl.program_id(2) > 0)
    if cfg_prev:
        lse_prev_t = [_pad_rows(lp_ref[g].T, LANES) for g in range(nb)]
    items = [(g, slice(h * LANES, (h + 1) * LANES)) for g in range(nb) for h in range(B_HEADS)]
    s_c = [jnp.where(mask_cur, _dot_nt(kc_ref[g, :, sl], q_ref[g, :, sl]), NEG_BIG) for g, sl in items]
    if band_prev:
        s_p = [jnp.where(mask_prev, _dot_nt(kp_ref[g, :, sl], q_ref[g, :, sl]), NEG_BIG) for g, sl in items]
    lse_rows, p_c, p_p, d_old = [], [], [], []
    for n, (g, sl) in enumerate(items):
        m = jnp.max(s_c[n], axis=0, keepdims=True)
        if band_prev:
            m = jnp.maximum(m, jnp.max(s_p[n], axis=0, keepdims=True))
        e_c = jnp.exp2(s_c[n] - m)
        den = jnp.sum(e_c, axis=0, keepdims=True)
        if band_prev:
            e_p = jnp.exp2(s_p[n] - m)
            den += jnp.sum(e_p, axis=0, keepdims=True)
        lse = m + jnp.log2(den)
        scale = 1.0 / den
        if cfg_prev:
            h = n % B_HEADS
            lse_prev = lse_prev_t[g][h:h + 1, :]
            m2 = jnp.maximum(lse, lse_prev)
            w_new, w_old = jnp.exp2(lse - m2), jnp.exp2(lse_prev - m2)
            scale = scale * w_new / (w_new + w_old)
            d_old.append(jnp.where(eye, w_old / (w_new + w_old), 0.0).astype(BF16))
            lse = m2 + jnp.log2(w_new + w_old)
        p_c.append((e_c * scale).astype(BF16))
        if band_prev:
            p_p.append((e_p * scale).astype(BF16))
        lse_rows.append(lse)
    for n, (g, sl) in enumerate(items):
        o = _dot_tn(p_c[n], vc_ref[g, :, sl])
        if band_prev:
            o += _dot_tn(p_p[n], vp_ref[g, :, sl])
        if cfg_prev:
            o += _dot(d_old[n], op_ref[g, n % B_HEADS].astype(BF16))
        if final:
            o_ref[g, :, sl] = (o * z_ref[g, :, sl].astype(F32)).astype(o_ref.dtype)
        else:
            o_ref[g, :, sl] = o.astype(o_ref.dtype)
    if not final:
        for g in range(nb):
            lse_t = _pad_rows(jnp.concatenate(lse_rows[g * B_HEADS:(g + 1) * B_HEADS], axis=0), LANES)
            l_ref[g] = lse_t.T
    if band_prev:
        kp_ref[...] = kc_ref[...]
        vp_ref[...] = vc_ref[...]


def _pad_rows(x, rows):
    if x.shape[0] == rows:
        return x
    return jnp.concatenate([x, jnp.zeros((rows - x.shape[0], x.shape[1]), x.dtype)], axis=0)


def _band_attn(qk, v, gate, batch, seq, dil, prev):
    final = gate is not None
    length = seq // dil
    blk = min(128, length)
    nbl = length // blk
    band_prev = nbl > 1
    nb = B_BATCH_GROUP if batch % B_BATCH_GROUP == 0 else 1

    def per_batch(x, classes=dil):
        return x.reshape(classes, batch, x.shape[1] // batch, x.shape[2])

    def spec(col, width=D_INNER):
        return pl.BlockSpec((None, nb, blk, width), lambda b, r, i: (r, b, i, col))

    o_spec, l_spec = spec(0), spec(0, LANES)
    in_specs, args, scratch = [spec(0), spec(1), spec(0)], [per_batch(qk), per_batch(qk), per_batch(v)], []
    if prev is not None:
        coarse = B_DIL_RATIO * dil
        assert prev[0].shape[0] == coarse and blk % (16 * B_DIL_RATIO) == 0
        pblk = blk // B_DIL_RATIO
        in_specs += [pl.BlockSpec((B_DIL_RATIO, None, nb, pblk, D_INNER), lambda b, r, i: (0, r, b, i, 0)),
                     pl.BlockSpec((B_DIL_RATIO, None, nb, pblk, LANES), lambda b, r, i: (0, r, b, i, 0))]
        args += [x.reshape(B_DIL_RATIO, dil, batch, length // B_DIL_RATIO, x.shape[-1]) for x in prev]
        scratch = [pltpu.VMEM((nb, B_HEADS, blk, LANES), F32), pltpu.VMEM((nb, blk, LANES), F32)]
    if band_prev:
        scratch += [pltpu.VMEM((nb, blk, D_INNER), BF16), pltpu.VMEM((nb, blk, D_INNER), BF16)]
    o_shape = jax.ShapeDtypeStruct((dil, batch, length, D_INNER), BF16)
    if final:
        in_specs.append(spec(0))
        args.append(per_batch(gate))
        out_specs, out_shape = o_spec, o_shape
    else:
        out_specs = [o_spec, l_spec]
        out_shape = [o_shape, jax.ShapeDtypeStruct((dil, batch, length, LANES), F32)]
    res = pl.pallas_call(
        functools.partial(_band_body, blk=blk, band_prev=band_prev, cfg_prev=prev is not None, final=final),
        grid=(batch // nb, dil, nbl),
        in_specs=in_specs,
        out_specs=out_specs,
        out_shape=out_shape,
        scratch_shapes=scratch,
        compiler_params=_cparams(("parallel", "parallel", "arbitrary")),
        name="dilated_band_attention",
    )(*args)
    if final:
        return res.reshape(dil, batch * length, D_INNER)
    return [x.reshape(dil, batch * length, x.shape[-1]) for x in res]


def _mixer_b(h, g, w_in, w_out, batch, seq, tabs, zero_aux):
    configs = sorted(B_CONFIGS, key=lambda wd: -wd[1])
    dils = [dil for _, dil in configs]
    assert dils[-1] == 1 and all(a == B_DIL_RATIO * b for a, b in zip(dils, dils[1:]))
    assert all(window // dil == 128 and seq % dil == 0 for window, dil in configs)
    w = w_in.astype(BF16)
    qk = _proj(h, g, w[:, :2 * D_INNER], "rot128", seq, tabs, res_dils=dils[:-1],
               scales=[B_HEAD_DIM ** -0.5 * LOG2E] * B_HEADS + [1.0] * B_HEADS)
    v = _proj(h, g, w[:, 2 * D_INNER:3 * D_INNER], "none", seq, tabs, res_dils=dils[:-1])
    z = _proj(h, g, w[:, 3 * D_INNER:], "silu", seq, tabs)
    qks = list(qk[1:]) + [qk[0][None]]
    vs = list(v[1:]) + [v[0][None]]
    prev = None
    for qk_d, v_d, dil in zip(qks, vs, dils):
        prev = _band_attn(qk_d, v_d, z[None] if dil == 1 else None, batch, seq, dil, prev)
    return prev.reshape(-1, D_INNER)


def _retention_body(lg_ref, q_ref, k_ref, v_ref, z_ref, o_ref, st_ref, *, seq):
    c = C_CHUNK
    lg = lg_ref[pl.program_id(1)]
    ri = lax.broadcasted_iota(jnp.int32, (c, c), 0)
    ci = lax.broadcasted_iota(jnp.int32, (c, c), 1)
    rel = (ri - ci).astype(F32)
    decay = jnp.where(rel >= 0, jnp.exp(jnp.maximum(rel, 0.0) * lg), 0.0)
    idx = lax.broadcasted_iota(jnp.int32, (c, 1), 0).astype(F32)
    k_decay = jnp.exp((c - 1 - idx) * lg)
    q_decay = jnp.exp((idx + 1.0) * lg)
    chunk_decay = jnp.exp(jnp.full((1, C_QK_DIM), c, F32) * lg)
    st_ref[...] = jnp.zeros(st_ref.shape, F32)

    unroll = min(C_UNROLL, seq // c)

    def group(gidx, carry):
        rows0 = [pl.multiple_of((gidx * unroll + u) * c, c) for u in range(unroll)]
        qs = [q_ref[pl.ds(r0, c), :] for r0 in rows0]
        ks = [k_ref[pl.ds(r0, c), :] for r0 in rows0]
        vs = [v_ref[pl.ds(r0, c), :] for r0 in rows0]
        scores = [(_dot_nt(q, k) * decay).astype(BF16) for q, k in zip(qs, ks)]
        o_intra = [_dot(s, v) for s, v in zip(scores, vs)]
        q_dec = [(q.astype(F32) * q_decay).astype(BF16) for q in qs]
        k_dec = [(k.astype(F32) * k_decay).astype(BF16) for k in ks]
        outs = []
        for u in range(unroll):
            st = st_ref[...]
            outs.append(o_intra[u] + _dot_nt(q_dec[u], st.astype(BF16)))
            st_ref[...] = st * chunk_decay + _dot_tn(vs[u], k_dec[u])
        for u, r0 in enumerate(rows0):
            o = outs[u]
            o = o * lax.rsqrt(jnp.mean(o * o, axis=-1, keepdims=True) + NORM_EPS)
            o_ref[pl.ds(r0, c), :] = (o * z_ref[pl.ds(r0, c), :].astype(F32)).astype(o_ref.dtype)
        return carry

    lax.fori_loop(0, seq // (c * unroll), group, 0)


def _mixer_c(h, g, w_in, w_out, batch, seq, tabs, zero_aux):
    w = w_in.astype(BF16)
    nh = C_HEADS
    dqk = 2 * nh * C_QK_DIM
    qk = _proj(h, g, w[:, :dqk], "rot128", seq, tabs, scales=[1.0] * nh + [C_QK_DIM ** -0.5] * nh)
    v = _proj(h, g, w[:, dqk:dqk + D_INNER], "none", seq, tabs)
    z = _proj(h, g, w[:, dqk + D_INNER:], "silu", seq, tabs)
    n = qk.shape[0]
    log_gamma = jnp.log1p(-jnp.exp2(-5.0 - jnp.arange(nh, dtype=F32)))
    return pl.pallas_call(
        functools.partial(_retention_body, seq=seq),
        grid=(batch, nh),
        in_specs=[
            pl.BlockSpec(memory_space=pltpu.SMEM),
            pl.BlockSpec((seq, C_QK_DIM), lambda b, h: (b, h)),
            pl.BlockSpec((seq, C_QK_DIM), lambda b, h: (b, nh + h)),
            pl.BlockSpec((seq, C_V_DIM), lambda b, h: (b, h)),
            pl.BlockSpec((seq, C_V_DIM), lambda b, h: (b, h)),
        ],
        out_specs=pl.BlockSpec((seq, C_V_DIM), lambda b, h: (b, h)),
        out_shape=jax.ShapeDtypeStruct((n, D_INNER), BF16),
        scratch_shapes=[pltpu.VMEM((C_V_DIM, C_QK_DIM), F32)],
        compiler_params=_cparams(("parallel", "parallel")),
        name="retention",
    )(log_gamma, qk, qk, v, z)


def _hgrn_body(q_ref, g_ref, v_ref, z_ref, o_ref, st_ref, *, seq):
    c = D_CHUNK
    w = D_HEAD_GROUP * LANES
    halves = [D_SUB * 2 ** lv for lv in range(16) if D_SUB * 2 ** (lv + 1) <= c]
    ri = lax.broadcasted_iota(jnp.int32, (c, c), 0)
    ci = lax.broadcasted_iota(jnp.int32, (c, c), 1)
    mats = [ri >= ci]
    pair_masks = []
    for half in halves:
        block = 2 * half
        mid = (ri // block) * block + half - 1
        right = (ri % block) >= half
        mats.append((right & (ci > mid) & (ci <= ri)) | (jnp.logical_not(right) & (ci > ri) & (ci <= mid)))
        pair_masks.append(((ri // block) == (ci // block)) & right & ((ci % block) < half))
    diag_masks = []
    for j in range(D_SUB):
        key = (ri // D_SUB) * D_SUB + j
        mats.append((ci > key) & (ci <= ri))
        diag_masks.append((ci == key) & (ri >= key))
    m1 = jnp.concatenate([m.astype(BF16) for m in mats], axis=0)
    m3 = jnp.concatenate([m1, m1, m1], axis=1)
    rows = lax.broadcasted_iota(jnp.int32, (c, w), 0)
    right_rows = [(rows % (2 * half)) >= half for half in halves]
    n_lv = len(halves)
    st_ref[...] = jnp.zeros(st_ref.shape, F32)

    head_sl = [slice(hd * LANES, (hd + 1) * LANES) for hd in range(D_HEAD_GROUP)]
    unroll = min(D_UNROLL, seq // c)

    def group_rows(gidx):
        return [pl.multiple_of((gidx * unroll + u) * c, c) for u in range(unroll)]

    def prepare(gidx):
        rows0 = group_rows(gidx)
        gates = [g_ref[pl.ds(r0, c), :] for r0 in rows0]
        bds = []
        for gate in gates:
            lf = jnp.log2(gate)
            hi = lf.astype(BF16)
            r1 = lf - hi.astype(F32)
            mid_part = r1.astype(BF16)
            lo = (r1 - mid_part.astype(F32)).astype(BF16)
            bds.append(_dot(m3, jnp.concatenate([hi, mid_part, lo], axis=0)))
        prep = []
        for u, r0 in enumerate(rows0):
            bd, kk = bds[u], 1.0 - gates[u]
            b = bd[0:c]
            q = q_ref[pl.ds(r0, c), :].astype(F32)
            v = v_ref[pl.ds(r0, c), :]
            qe = (q * jnp.exp2(b)).astype(BF16)
            qts, kts = [], []
            for lv in range(len(halves)):
                e = jnp.exp2(bd[(lv + 1) * c:(lv + 2) * c])
                qts.append(jnp.where(right_rows[lv], q * e, 0.0).astype(BF16))
                kts.append(jnp.where(right_rows[lv], 0.0, kk * e).astype(BF16))
            q_diag = jnp.concatenate(
                [(q * jnp.exp2(bd[(1 + n_lv + j) * c:(2 + n_lv + j) * c])).astype(BF16) for j in range(D_SUB)], axis=0)
            b_last = b[c - 1:c, :]
            k_dec = (kk * jnp.exp2(b_last - b)).astype(BF16)
            prep.append((qe, qts, kts, v, (q_diag, kk.astype(BF16)), k_dec, jnp.exp2(b_last)))
        o_local = []
        for qe, qts, kts, v, (q_diag, kk16), k_dec, eb_last in prep:
            per_head = []
            for hd, sl in enumerate(head_sl):
                a = jnp.zeros((c, c), F32)
                for lv in range(n_lv):
                    a = a + jnp.where(pair_masks[lv], _dot_nt(qts[lv][:, sl], kts[lv][:, sl]), 0.0)
                g_diag = _dot_nt(q_diag[:, sl], kk16[:, sl])
                for j in range(D_SUB):
                    a = a + jnp.where(diag_masks[j], g_diag[j * c:(j + 1) * c], 0.0)
                per_head.append(_dot(a.astype(BF16), v[:, sl]))
            o_local.append(per_head)
        return [(o_local[u], p[0], p[3], p[5], p[6]) for u, p in enumerate(prep)]

    def finish(gidx, prepared):
        outs = []
        for o_loc, qe, v, k_dec, eb_last in prepared:
            per_head = []
            for hd, sl in enumerate(head_sl):
                st = st_ref[hd]
                per_head.append(o_loc[hd] + _dot_nt(qe[:, sl], st.astype(BF16)))
                st_ref[hd] = st * eb_last[:, sl] + _dot_tn(v[:, sl], k_dec[:, sl])
            outs.append(per_head)
        for u, r0 in enumerate(group_rows(gidx)):
            for hd, sl in enumerate(head_sl):
                o = outs[u][hd]
                o = o * lax.rsqrt(jnp.mean(o * o, axis=-1, keepdims=True) + NORM_EPS)
                o_ref[pl.ds(r0, c), sl] = (o * z_ref[pl.ds(r0, c), sl].astype(F32)).astype(o_ref.dtype)

    def trip(gidx, carry):
        finish(gidx, prepare(gidx))
        return carry

    lax.fori_loop(0, seq // (c * unroll), trip, 0)


def _mixer_d(h, g, w_in, w_out, lower_bound, batch, seq, tabs, zero_aux):
    wq, wf, wi, wz = (w_in[:, t * D_INNER:(t + 1) * D_INNER] for t in range(4))
    qz = _proj(h, g, jnp.concatenate([wq, wz], axis=1).astype(BF16), "silu", seq, tabs,
               scales=[D_K_DIM ** -0.5] * D_HEADS + [1.0] * D_HEADS)
    v = _proj(h, g, wi.astype(BF16), "none", seq, tabs)
    gates = _proj(h, g, wf.astype(BF16), "hgrn_g", seq, tabs, out_dtype=F32,
                  aux=lower_bound.reshape(1, D_INNER).astype(F32))
    n = qz.shape[0]
    ng = D_HEADS // D_HEAD_GROUP
    w = D_HEAD_GROUP * LANES
    return pl.pallas_call(
        functools.partial(_hgrn_body, seq=seq),
        grid=(batch, ng),
        in_specs=[
            pl.BlockSpec((seq, w), lambda b, h: (b, h)),
            pl.BlockSpec((seq, w), lambda b, h: (b, h)),
            pl.BlockSpec((seq, w), lambda b, h: (b, h)),
            pl.BlockSpec((seq, w), lambda b, h: (b, ng + h)),
        ],
        out_specs=pl.BlockSpec((seq, w), lambda b, h: (b, h)),
        out_shape=jax.ShapeDtypeStruct((n, D_INNER), BF16),
        scratch_shapes=[pltpu.VMEM((D_HEAD_GROUP, D_V_DIM, D_K_DIM), F32)],
        compiler_params=_cparams(("parallel", "parallel")),
        name="hgrn2_recurrence",
    )(qz, gates, v, qz)


def kernel(x, norm_g, final_g, a_w_in, a_w_out, b_w_in, b_w_out, c_w_in, c_w_out, d_w_in, d_w_out, hgrn_lb_logits):
    batch, seq, d = x.shape
    depth = norm_g.shape[0]
    lb_cum = jnp.cumsum(jax.nn.softmax(hgrn_lb_logits.astype(F32), axis=0), axis=0)
    lower_bounds = lb_cum - lb_cum[0]
    tabs = _rope_tables(seq, 128) + _rope_tables(seq, 64)
    h = x.reshape(batch * seq, d)
    for layer in range(depth):
        kind, slot = layer % N_MIXERS, layer // N_MIXERS
        g = norm_g[layer]
        if kind == 0:
            og, w_out = _mixer_a(h, g, a_w_in[slot], a_w_out[slot], batch, seq, tabs, None), a_w_out[slot]
        elif kind == 1:
            og, w_out = _mixer_b(h, g, b_w_in[slot], b_w_out[slot], batch, seq, tabs, None), b_w_out[slot]
        elif kind == 2:
            og, w_out = _mixer_c(h, g, c_w_in[slot], c_w_out[slot], batch, seq, tabs, None), c_w_out[slot]
        else:
            og, w_out = _mixer_d(h, g, d_w_in[slot], d_w_out[slot], lower_bounds[layer], batch, seq, tabs, None), d_w_out[slot]
        h = _outproj(og, w_out.astype(BF16), h, final_g, final=(layer == depth - 1))
    return h.reshape(batch, seq, d)
```

```python
import functools

import jax
import jax.numpy as jnp
import numpy as np
from jax import lax
from jax.experimental import pallas as pl
from jax.experimental.pallas import tpu as pltpu

F32 = jnp.float32
BF16 = jnp.bfloat16

D_MODEL = 1024
D_INNER = 2048
N_MIXERS = 4
NORM_EPS = 1e-6
ROPE_THETA = 10000.0
LANES = 128
NEG_BIG = -1e30
LOG2E = 1.4426950408889634

A_HEADS, A_HEAD_DIM = 16, 128
A_IDX_HEADS, A_IDX_DIM = 16, 64
A_TOPK_MAX = 256
A_Q_BLOCK = 256
A_K_BLOCK = 512
A_BATCH_GROUP = 2
A_CNT_ROWS = 64
A_HEAD_STAGE = 16
B_HEADS, B_HEAD_DIM = 16, 128
B_CONFIGS = ((128, 1), (512, 4), (2048, 16))
B_DIL_RATIO = 4
B_BATCH_GROUP = 4
C_HEADS, C_QK_DIM, C_V_DIM, C_CHUNK = 8, 128, 256, 128
C_UNROLL = 8
D_HEADS, D_K_DIM, D_V_DIM = 16, 128, 128
D_CHUNK = 64
D_SUB = 2
D_HEAD_GROUP = 2
D_UNROLL = 4

PROJ_TILE_N = 2048
PROJ_TILE_N_RES = 1024
VMEM_LIMIT = 56 * 1024 * 1024


def _cparams(sem):
    return pltpu.CompilerParams(dimension_semantics=sem, vmem_limit_bytes=VMEM_LIMIT)


def _dot(a, b):
    return jnp.dot(a, b, preferred_element_type=F32)


def _dot_nt(a, b):
    return lax.dot_general(a, b, (((1,), (1,)), ((), ())), preferred_element_type=F32)


def _dot_tn(a, b):
    return lax.dot_general(a, b, (((0,), (0,)), ((), ())), preferred_element_type=F32)


def _rope_tables(seq, dh):
    freqs = ROPE_THETA ** (-jnp.arange(0, dh, 2, dtype=F32) / dh)
    ang = jnp.arange(seq, dtype=F32)[:, None] * freqs[None, :]
    cos, sin = jnp.cos(ang), jnp.sin(ang)
    cosf = jnp.concatenate([cos, cos], axis=-1)
    sinf = jnp.concatenate([-sin, sin], axis=-1)
    rep = LANES // dh
    return jnp.tile(cosf, (1, rep)), jnp.tile(sinf, (1, rep))


def _epilogue(kind, scale, x, tabs, aux):
    c128, s128, c64, s64 = tabs
    if kind == "rot128":
        y = x * c128 + pltpu.roll(x, 64, 1) * s128
    elif kind == "rot64":
        lane = lax.broadcasted_iota(jnp.int32, x.shape, 1)
        partner = jnp.where((lane % 64) < 32, pltpu.roll(x, 96, 1), pltpu.roll(x, 32, 1))
        y = x * c64 + partner * s64
    elif kind == "silu":
        y = x * jax.nn.sigmoid(x)
    elif kind == "hgrn_g":
        y = aux + (1.0 - aux) * jax.nn.sigmoid(x)
    else:
        y = x
    if scale is not None:
        y = y * scale
    return y


def _proj_body(h_ref, g_ref, w_ref, c128_ref, s128_ref, c64_ref, s64_ref, aux_ref, scale_ref, *rest,
               kinds, scaled, res_dils):
    o_ref = rest[0]
    res_refs = rest[1:1 + len(res_dils)]
    u_ref = rest[1 + len(res_dils)]
    tm = o_ref.shape[0]

    @pl.when(pl.program_id(1) == 0)
    def _norm():
        x = h_ref[...]
        ms = jnp.mean(x * x, axis=-1, keepdims=True)
        u_ref[...] = (x * lax.rsqrt(ms + NORM_EPS) * g_ref[...]).astype(BF16)

    tabs = (c128_ref[...] if "rot128" in kinds else None, s128_ref[...] if "rot128" in kinds else None,
            c64_ref[...] if "rot64" in kinds else None, s64_ref[...] if "rot64" in kinds else None)
    def pair_dot(p):
        return _dot(u_ref[...], w_ref[:, 2 * p * LANES:min(2 * p + 2, len(kinds)) * LANES])

    n_pairs = (len(kinds) + 1) // 2
    accs = {0: pair_dot(0)}
    for c, kind in enumerate(kinds):
        sl = slice(c * LANES, (c + 1) * LANES)
        aux = aux_ref[:, sl] if kind == "hgrn_g" else None
        if c % 2 == 0 and c // 2 + 1 < n_pairs:
            accs[c // 2 + 1] = pair_dot(c // 2 + 1)
        acc = accs[c // 2][:, (c % 2) * LANES:(c % 2 + 1) * LANES]
        y = _epilogue(kind, scale_ref[:, sl] if scaled else None, acc, tabs, aux)
        o_ref[:, sl] = y.astype(o_ref.dtype)
        if res_dils:
            y_ref = rest[-1]
            y_ref[c] = y
            for dil, r_ref in zip(res_dils, res_refs):
                for r in range(dil):
                    r_ref[r, :, sl] = y_ref[c, pl.ds(r, tm // dil, stride=dil), :].astype(r_ref.dtype)


def _proj(h, g, w, kinds, seq, tabs, *, scales=None, aux=None, out_dtype=None, res_dils=()):
    n, d = h.shape
    c = w.shape[1]
    tm = min(1024, seq)
    tn = min(PROJ_TILE_N_RES if res_dils else PROJ_TILE_N, c)
    if isinstance(kinds, str):
        kinds = (kinds,) * (tn // LANES)
    assert len(kinds) == tn // LANES and (c == tn or len(set(kinds)) == 1)
    if scales is None:
        scale_row = jnp.zeros((1, c), F32)
    else:
        scale_row = jnp.repeat(jnp.asarray(scales, F32), LANES).reshape(1, c)
    if aux is None:
        aux = jnp.zeros((1, c), F32)
    out_dtype = out_dtype or BF16
    pos_blocks = seq // tm
    tab_spec = pl.BlockSpec((tm, LANES), lambda i, j: (i % pos_blocks, 0))
    row_spec = pl.BlockSpec((1, tn), lambda i, j: (0, j))
    out_specs = [pl.BlockSpec((tm, tn), lambda i, j: (i, j))]
    out_shape = [jax.ShapeDtypeStruct((n, c), out_dtype)]
    scratch = [pltpu.VMEM((tm, d), BF16)]
    for dil in res_dils:
        out_specs.append(pl.BlockSpec((dil, tm // dil, tn), lambda i, j: (0, i, j)))
        out_shape.append(jax.ShapeDtypeStruct((dil, n // dil, c), out_dtype))
    if res_dils:
        scratch.append(pltpu.VMEM((tn // LANES, tm, LANES), F32))
    res = pl.pallas_call(
        functools.partial(_proj_body, kinds=tuple(kinds), scaled=scales is not None, res_dils=tuple(res_dils)),
        grid=(n // tm, c // tn),
        in_specs=[
            pl.BlockSpec((tm, d), lambda i, j: (i, 0)),
            pl.BlockSpec((1, d), lambda i, j: (0, 0)),
            pl.BlockSpec((d, tn), lambda i, j: (0, j)),
            tab_spec, tab_spec, tab_spec, tab_spec,
            row_spec, row_spec,
        ],
        out_specs=out_specs,
        out_shape=out_shape,
        scratch_shapes=scratch,
        compiler_params=_cparams(("parallel", "arbitrary")),
        name="norm_in_proj",
    )(h, g.reshape(1, d), w, *tabs, aux, scale_row)
    return res if res_dils else res[0]


def _outproj_body(og_ref, w_ref, h_ref, gf_ref, o_ref, *, final):
    y = h_ref[...] + _dot(og_ref[...], w_ref[...])
    if final:
        ms = jnp.mean(y * y, axis=-1, keepdims=True)
        y = y * lax.rsqrt(ms + NORM_EPS) * gf_ref[...]
    o_ref[...] = y


def _outproj(og, w, h, gf, final):
    n, di = og.shape
    d = w.shape[1]
    tm = min(1024, n)
    return pl.pallas_call(
        functools.partial(_outproj_body, final=final),
        grid=(n // tm,),
        in_specs=[
            pl.BlockSpec((tm, di), lambda i: (i, 0)),
            pl.BlockSpec((di, d), lambda i: (0, 0)),
            pl.BlockSpec((tm, d), lambda i: (i, 0)),
            pl.BlockSpec((1, d), lambda i: (0, 0)),
        ],
        out_specs=pl.BlockSpec((tm, d), lambda i: (i, 0)),
        out_shape=jax.ShapeDtypeStruct((n, d), F32),
        compiler_params=_cparams(("parallel",)),
        name="out_proj_residual",
    )(og, w, h, gf.reshape(1, d))


def _dsa_index_body(iq_ref, ik_ref, iw_ref, bias_ref, sc_ref, *, topk, tq, kc, seq):
    i = pl.program_id(1)
    n_chunks = ((i + 1) * tq + kc - 1) // kc
    int_min = jnp.int32(-2 ** 31)
    iw_t = iw_ref[...].astype(F32).T
    lane = lax.broadcasted_iota(jnp.int32, (kc, LANES), 1)
    s_loc = lax.broadcasted_iota(jnp.int32, (kc, tq), 0)
    t_pos = i * tq + lax.broadcasted_iota(jnp.int32, (kc, tq), 1)

    def score_chunk(c, carry):
        k0 = pl.multiple_of(c * kc, kc)
        ik2 = ik_ref[pl.ds(k0, kc), :]
        ik_lo = jnp.where(lane < A_IDX_DIM, ik2, jnp.zeros_like(ik2))
        ik_hi = jnp.where(lane >= A_IDX_DIM, ik2, jnp.zeros_like(ik2))
        score = jnp.zeros((kc, tq), F32)
        for p in range(A_IDX_HEADS // 2):
            iq2 = iq_ref[:, p * LANES:(p + 1) * LANES]
            score += jnp.maximum(_dot_nt(ik_lo, iq2), 0.0) * iw_t[2 * p:2 * p + 1, :]
            score += jnp.maximum(_dot_nt(ik_hi, iq2), 0.0) * iw_t[2 * p + 1:2 * p + 2, :]
        sc_ref[pl.ds(k0, kc), :] = jnp.where(k0 + s_loc <= t_pos, score, -jnp.inf)
        return carry

    lax.fori_loop(0, n_chunks, score_chunk, 0)

    def threshold(prefix):
        key = prefix ^ int_min
        return pltpu.bitcast(jnp.where(key < 0, key ^ jnp.int32(0x7FFFFFFF), key), F32)

    def bisect(it, prefix):
        cand = prefix | (jnp.int32(1) << (31 - it))
        thr = threshold(cand)

        def count(c, part):
            blk = sc_ref[pl.ds(pl.multiple_of(c * kc, kc), kc), :]
            hits = jnp.where(blk >= thr, 1.0, 0.0).reshape(kc // A_CNT_ROWS, A_CNT_ROWS, tq)
            return part + jnp.sum(hits, axis=0)

        part = lax.fori_loop(0, n_chunks, count, jnp.zeros((A_CNT_ROWS, tq), F32))
        cnt = jnp.sum(part, axis=0, keepdims=True)
        return jnp.where(cnt >= topk, cand, prefix)

    n_iter = jnp.where((i + 1) * tq <= topk, 0, 32)
    prefix = lax.fori_loop(0, n_iter, bisect, jnp.zeros((1, tq), jnp.int32))
    take_all = prefix == 0
    thr = threshold(jnp.where(take_all, jnp.int32(0x00800000), prefix))

    def emit(c, carry):
        k0 = pl.multiple_of(c * kc, kc)
        sel = ((sc_ref[pl.ds(k0, kc), :] >= thr) | take_all) & (k0 + s_loc <= t_pos)
        bias_ref[pl.ds(k0, kc), :] = jnp.where(sel, 0.0, NEG_BIG).astype(bias_ref.dtype)
        return carry

    lax.fori_loop(0, n_chunks, emit, 0)

    def emit_masked(c, carry):
        bias_ref[pl.ds(pl.multiple_of(c * kc, kc), kc), :] = jnp.full((kc, tq), NEG_BIG, bias_ref.dtype)
        return carry

    lax.fori_loop(n_chunks, seq // kc, emit_masked, 0)


def _dsa_index(aux, batch, seq, topk):
    c_ik = A_IDX_HEADS * A_IDX_DIM // LANES + 2
    tq = min(A_Q_BLOCK, seq)
    kc = min(A_K_BLOCK, seq)
    nq = seq // tq
    return pl.pallas_call(
        functools.partial(_dsa_index_body, topk=topk, tq=tq, kc=kc, seq=seq),
        grid=(batch, nq),
        in_specs=[
            pl.BlockSpec((tq, A_IDX_HEADS * A_IDX_DIM), lambda b, i: (b * nq + i, 0)),
            pl.BlockSpec((seq, LANES), lambda b, i: (b, c_ik)),
            pl.BlockSpec((tq, LANES), lambda b, i: (b * nq + i, c_ik + 1)),
        ],
        out_specs=pl.BlockSpec((None, None, seq, tq), lambda b, i: (b, i, 0, 0)),
        out_shape=jax.ShapeDtypeStruct((batch, nq, seq, tq), BF16),
        scratch_shapes=[pltpu.VMEM((seq, tq), F32)],
        compiler_params=_cparams(("parallel", "parallel")),
        name="dsa_indexer_topk_mask",
    )(aux, aux, aux)


def _dsa_attn_body(qi_ref, kj_ref, q_ref, z_ref, k_ref, v_ref, bias_ref, o_ref, m_ref, l_ref, acc_ref, *, tq, kb):
    t = pl.program_id(1)
    i = qi_ref[t]
    j = kj_ref[t]
    nh = A_HEADS
    dv = A_HEAD_DIM

    @pl.when(j == 0)
    def _init():
        m_ref[...] = jnp.full(m_ref.shape, NEG_BIG, F32)
        l_ref[...] = jnp.zeros(l_ref.shape, F32)
        acc_ref[...] = jnp.zeros(acc_ref.shape, F32)

    for g in range(q_ref.shape[0]):
        k = k_ref[g]
        v_t = v_ref[g].astype(F32).T.astype(BF16)
        bias = bias_ref[g].astype(F32)
        for h0 in range(0, nh, A_HEAD_STAGE):
            hs = range(h0, h0 + A_HEAD_STAGE)
            s = [_dot_nt(k, q_ref[g, :, h * LANES:(h + 1) * LANES]) + bias for h in hs]
            ps, alphas = [], []
            for n, h in enumerate(hs):
                m_old = m_ref[g, h:h + 1, :]
                m_new = jnp.maximum(m_old, jnp.max(s[n], axis=0, keepdims=True))
                alpha = jnp.exp2(m_old - m_new)
                p = jnp.exp2(s[n] - m_new)
                l_ref[g, h:h + 1, :] = alpha * l_ref[g, h:h + 1, :] + jnp.sum(p, axis=0, keepdims=True)
                m_ref[g, h:h + 1, :] = m_new
                ps.append(p.astype(BF16))
                alphas.append(alpha)
            for n, h in enumerate(hs):
                rows = slice(h * dv, (h + 1) * dv)
                acc_ref[g, rows, :] = alphas[n] * acc_ref[g, rows, :] + _dot(v_t, ps[n])

    @pl.when(j == (i * tq + tq - 1) // kb)
    def _fin():
        for g in range(q_ref.shape[0]):
            for h in range(nh):
                sl = slice(h * LANES, (h + 1) * LANES)
                o_t = acc_ref[g, h * dv:(h + 1) * dv, :] * (1.0 / l_ref[g, h:h + 1, :])
                o_ref[g, :, sl] = (o_t.T * z_ref[g, :, sl].astype(F32)).astype(o_ref.dtype)


def _dsa_attn(q, z, aux, bias, batch, seq):
    n = q.shape[0]
    c_k = A_IDX_HEADS * A_IDX_DIM // LANES
    tq = bias.shape[-1]
    kb = min(A_K_BLOCK, seq)
    nq, nk = seq // tq, seq // kb
    pairs = [(i, j) for i in range(nq) for j in range((i * tq + tq - 1) // kb + 1)]
    qi = jnp.asarray([p[0] for p in pairs], jnp.int32)
    kj = jnp.asarray([p[1] for p in pairs], jnp.int32)
    nb = A_BATCH_GROUP if batch % A_BATCH_GROUP == 0 else 1

    def per_batch(x):
        return x.reshape(batch, seq, x.shape[-1])

    grid_spec = pltpu.PrefetchScalarGridSpec(
        num_scalar_prefetch=2,
        grid=(batch // nb, len(pairs)),
        in_specs=[
            pl.BlockSpec((nb, tq, D_INNER), lambda b, t, qi, kj: (b, qi[t], 0)),
            pl.BlockSpec((nb, tq, D_INNER), lambda b, t, qi, kj: (b, qi[t], 0)),
            pl.BlockSpec((nb, kb, LANES), lambda b, t, qi, kj: (b, kj[t], c_k)),
            pl.BlockSpec((nb, kb, LANES), lambda b, t, qi, kj: (b, kj[t], c_k + 1)),
            pl.BlockSpec((nb, None, kb, tq), lambda b, t, qi, kj: (b, qi[t], kj[t], 0)),
        ],
        out_specs=pl.BlockSpec((nb, tq, D_INNER), lambda b, t, qi, kj: (b, qi[t], 0)),
        scratch_shapes=[
            pltpu.VMEM((nb, A_HEADS, tq), F32),
            pltpu.VMEM((nb, A_HEADS, tq), F32),
            pltpu.VMEM((nb, A_HEADS * A_HEAD_DIM, tq), F32),
        ],
    )
    return pl.pallas_call(
        functools.partial(_dsa_attn_body, tq=tq, kb=kb),
        grid_spec=grid_spec,
        out_shape=jax.ShapeDtypeStruct((batch, seq, D_INNER), BF16),
        compiler_params=_cparams(("parallel", "arbitrary")),
        name="dsa_masked_attention",
    )(qi, kj, per_batch(q), per_batch(z), per_batch(aux), per_batch(aux), bias).reshape(n, D_INNER)


def _mixer_a(h, g, w_in, w_out, batch, seq, tabs, zero_aux):
    hd = A_HEADS * A_HEAD_DIM
    hi = A_IDX_HEADS * A_IDX_DIM
    o0 = np.cumsum([0, hd, A_HEAD_DIM, A_HEAD_DIM, hi, A_IDX_DIM, A_IDX_HEADS, D_INNER])
    wq, wk, wv, wiq, wik, wiw, wz = (w_in[:, o0[t]:o0[t + 1]] for t in range(7))
    pad = jnp.zeros((D_MODEL, LANES - A_IDX_HEADS), F32)
    w_aux = jnp.concatenate([wiq, wk, wv, wik, wik, wiw, pad], axis=1).astype(BF16)
    n_iq = hi // LANES
    q = _proj(h, g, wq.astype(BF16), "rot128", seq, tabs, scales=[A_HEAD_DIM ** -0.5 * LOG2E] * A_HEADS)
    z = _proj(h, g, wz.astype(BF16), "silu", seq, tabs)
    aux = _proj(h, g, w_aux, ["rot64"] * n_iq + ["rot128", "none", "rot64", "none"], seq, tabs,
                scales=[1.0] * (n_iq + 3) + [A_IDX_HEADS ** -0.5 * A_IDX_DIM ** -0.5])
    bias = _dsa_index(aux, batch, seq, min(A_TOPK_MAX, seq // 4))
    return _dsa_attn(q, z, aux, bias, batch, seq)


def _band_body(*refs, blk, band_prev, cfg_prev, final):
    refs = list(refs)
    q_ref, kc_ref, vc_ref = refs[:3]
    del refs[:3]
    if band_prev:
        kp_ref, vp_ref = refs[-2:]
        del refs[-2:]

        @pl.when(pl.program_id(2) == 0)
        def _no_prev():
            kp_ref[...] = jnp.zeros(kp_ref.shape, kp_ref.dtype)
            vp_ref[...] = jnp.zeros(vp_ref.shape, vp_ref.dtype)
    nb = q_ref.shape[0]
    if cfg_prev:
        op4_ref, lp4_ref = refs[:2]
        del refs[:2]
        op_ref, lp_ref = refs[-2:]
        del refs[-2:]
        for g in range(nb):
            for a in range(B_DIL_RATIO):
                rows = pl.ds(a, blk // B_DIL_RATIO, stride=B_DIL_RATIO)
                lp_ref[g, rows, :] = lp4_ref[a, g]
                for h in range(B_HEADS):
                    op_ref[g, h, rows, :] = op4_ref[a, g, :, h * LANES:(h + 1) * LANES].astype(F32)
    if final:
        z_ref, o_ref = refs
    else:
        o_ref, l_ref = refs
    ki = lax.broadcasted_iota(jnp.int32, (blk, blk), 0)
    qi = lax.broadcasted_iota(jnp.int32, (blk, blk), 1)
    mask_cur = ki <= qi
    eye = ki == qi
    if band_prev:
        mask_prev = (ki >= qi) & (pl.program_id(2) > 0)
    if cfg_prev:
        lse_prev_t = [_pad_rows(lp_ref[g].T, LANES) for g in range(nb)]
    items = [(g, slice(h * LANES, (h + 1) * LANES)) for g in range(nb) for h in range(B_HEADS)]
    s_c = [jnp.where(mask_cur, _dot_nt(kc_ref[g, :, sl], q_ref[g, :, sl]), NEG_BIG) for g, sl in items]
    if band_prev:
        s_p = [jnp.where(mask_prev, _dot_nt(kp_ref[g, :, sl], q_ref[g, :, sl]), NEG_BIG) for g, sl in items]
    lse_rows, p_c, p_p, d_old = [], [], [], []
    for n, (g, sl) in enumerate(items):
        m = jnp.max(s_c[n], axis=0, keepdims=True)
        if band_prev:
            m = jnp.maximum(m, jnp.max(s_p[n], axis=0, keepdims=True))
        e_c = jnp.exp2(s_c[n] - m)
        den = jnp.sum(e_c, axis=0, keepdims=True)
        if band_prev:
            e_p = jnp.exp2(s_p[n] - m)
            den += jnp.sum(e_p, axis=0, keepdims=True)
        lse = m + jnp.log2(den)
        scale = 1.0 / den
        if cfg_prev:
            h = n % B_HEADS
            lse_prev = lse_prev_t[g][h:h + 1, :]
            m2 = jnp.maximum(lse, lse_prev)
            w_new, w_old = jnp.exp2(lse - m2), jnp.exp2(lse_prev - m2)
            scale = scale * w_new / (w_new + w_old)
            d_old.append(jnp.where(eye, w_old / (w_new + w_old), 0.0).astype(BF16))
            lse = m2 + jnp.log2(w_new + w_old)
        p_c.append((e_c * scale).astype(BF16))
        if band_prev:
            p_p.append((e_p * scale).astype(BF16))
        lse_rows.append(lse)
    for n, (g, sl) in enumerate(items):
        o = _dot_tn(p_c[n], vc_ref[g, :, sl])
        if band_prev:
            o += _dot_tn(p_p[n], vp_ref[g, :, sl])
        if cfg_prev:
            o += _dot(d_old[n], op_ref[g, n % B_HEADS].astype(BF16))
        if final:
            o_ref[g, :, sl] = (o * z_ref[g, :, sl].astype(F32)).astype(o_ref.dtype)
        else:
            o_ref[g, :, sl] = o.astype(o_ref.dtype)
    if not final:
        for g in range(nb):
            lse_t = _pad_rows(jnp.concatenate(lse_rows[g * B_HEADS:(g + 1) * B_HEADS], axis=0), LANES)
            l_ref[g] = lse_t.T
    if band_prev:
        kp_ref[...] = kc_ref[...]
        vp_ref[...] = vc_ref[...]


def _pad_rows(x, rows):
    if x.shape[0] == rows:
        return x
    return jnp.concatenate([x, jnp.zeros((rows - x.shape[0], x.shape[1]), x.dtype)], axis=0)


def _band_attn(qk, v, gate, batch, seq, dil, prev):
    final = gate is not None
    length = seq // dil
    blk = min(128, length)
    nbl = length // blk
    band_prev = nbl > 1
    nb = B_BATCH_GROUP if batch % B_BATCH_GROUP == 0 else 1

    def per_batch(x, classes=dil):
        return x.reshape(classes, batch, x.shape[1] // batch, x.shape[2])

    def spec(col, width=D_INNER):
        return pl.BlockSpec((None, nb, blk, width), lambda b, r, i: (r, b, i, col))

    o_spec, l_spec = spec(0), spec(0, LANES)
    in_specs, args, scratch = [spec(0), spec(1), spec(0)], [per_batch(qk), per_batch(qk), per_batch(v)], []
    if prev is not None:
        coarse = B_DIL_RATIO * dil
        assert prev[0].shape[0] == coarse and blk % (16 * B_DIL_RATIO) == 0
        pblk = blk // B_DIL_RATIO
        in_specs += [pl.BlockSpec((B_DIL_RATIO, None, nb, pblk, D_INNER), lambda b, r, i: (0, r, b, i, 0)),
                     pl.BlockSpec((B_DIL_RATIO, None, nb, pblk, LANES), lambda b, r, i: (0, r, b, i, 0))]
        args += [x.reshape(B_DIL_RATIO, dil, batch, length // B_DIL_RATIO, x.shape[-1]) for x in prev]
        scratch = [pltpu.VMEM((nb, B_HEADS, blk, LANES), F32), pltpu.VMEM((nb, blk, LANES), F32)]
    if band_prev:
        scratch += [pltpu.VMEM((nb, blk, D_INNER), BF16), pltpu.VMEM((nb, blk, D_INNER), BF16)]
    o_shape = jax.ShapeDtypeStruct((dil, batch, length, D_INNER), BF16)
    if final:
        in_specs.append(spec(0))
        args.append(per_batch(gate))
        out_specs, out_shape = o_spec, o_shape
    else:
        out_specs = [o_spec, l_spec]
        out_shape = [o_shape, jax.ShapeDtypeStruct((dil, batch, length, LANES), F32)]
    res = pl.pallas_call(
        functools.partial(_band_body, blk=blk, band_prev=band_prev, cfg_prev=prev is not None, final=final),
        grid=(batch // nb, dil, nbl),
        in_specs=in_specs,
        out_specs=out_specs,
        out_shape=out_shape,
        scratch_shapes=scratch,
        compiler_params=_cparams(("parallel", "parallel", "arbitrary")),
        name="dilated_band_attention",
    )(*args)
    if final:
        return res.reshape(dil, batch * length, D_INNER)
    return [x.reshape(dil, batch * length, x.shape[-1]) for x in res]


def _mixer_b(h, g, w_in, w_out, batch, seq, tabs, zero_aux):
    configs = sorted(B_CONFIGS, key=lambda wd: -wd[1])
    dils = [dil for _, dil in configs]
    assert dils[-1] == 1 and all(a == B_DIL_RATIO * b for a, b in zip(dils, dils[1:]))
    assert all(window // dil == 128 and seq % dil == 0 for window, dil in configs)
    w = w_in.astype(BF16)
    qk = _proj(h, g, w[:, :2 * D_INNER], "rot128", seq, tabs, res_dils=dils[:-1],
               scales=[B_HEAD_DIM ** -0.5 * LOG2E] * B_HEADS + [1.0] * B_HEADS)
    v = _proj(h, g, w[:, 2 * D_INNER:3 * D_INNER], "none", seq, tabs, res_dils=dils[:-1])
    z = _proj(h, g, w[:, 3 * D_INNER:], "silu", seq, tabs)
    qks = list(qk[1:]) + [qk[0][None]]
    vs = list(v[1:]) + [v[0][None]]
    prev = None
    for qk_d, v_d, dil in zip(qks, vs, dils):
        prev = _band_attn(qk_d, v_d, z[None] if dil == 1 else None, batch, seq, dil, prev)
    return prev.reshape(-1, D_INNER)


def _retention_body(lg_ref, q_ref, k_ref, v_ref, z_ref, o_ref, st_ref, *, seq):
    c = C_CHUNK
    lg = lg_ref[pl.program_id(1)]
    ri = lax.broadcasted_iota(jnp.int32, (c, c), 0)
    ci = lax.broadcasted_iota(jnp.int32, (c, c), 1)
    rel = (ri - ci).astype(F32)
    decay = jnp.where(rel >= 0, jnp.exp(jnp.maximum(rel, 0.0) * lg), 0.0)
    idx = lax.broadcasted_iota(jnp.int32, (c, 1), 0).astype(F32)
    k_decay = jnp.exp((c - 1 - idx) * lg)
    q_decay = jnp.exp((idx + 1.0) * lg)
    chunk_decay = jnp.exp(jnp.full((1, C_QK_DIM), c, F32) * lg)
    st_ref[...] = jnp.zeros(st_ref.shape, F32)

    unroll = min(C_UNROLL, seq // c)

    def group(gidx, carry):
        rows0 = [pl.multiple_of((gidx * unroll + u) * c, c) for u in range(unroll)]
        qs = [q_ref[pl.ds(r0, c), :] for r0 in rows0]
        ks = [k_ref[pl.ds(r0, c), :] for r0 in rows0]
        vs = [v_ref[pl.ds(r0, c), :] for r0 in rows0]
        scores = [(_dot_nt(q, k) * decay).astype(BF16) for q, k in zip(qs, ks)]
        o_intra = [_dot(s, v) for s, v in zip(scores, vs)]
        q_dec = [(q.astype(F32) * q_decay).astype(BF16) for q in qs]
        k_dec = [(k.astype(F32) * k_decay).astype(BF16) for k in ks]
        outs = []
        for u in range(unroll):
            st = st_ref[...]
            outs.append(o_intra[u] + _dot_nt(q_dec[u], st.astype(BF16)))
            st_ref[...] = st * chunk_decay + _dot_tn(vs[u], k_dec[u])
        for u, r0 in enumerate(rows0):
            o = outs[u]
            o = o * lax.rsqrt(jnp.mean(o * o, axis=-1, keepdims=True) + NORM_EPS)
            o_ref[pl.ds(r0, c), :] = (o * z_ref[pl.ds(r0, c), :].astype(F32)).astype(o_ref.dtype)
        return carry

    lax.fori_loop(0, seq // (c * unroll), group, 0)


def _mixer_c(h, g, w_in, w_out, batch, seq, tabs, zero_aux):
    w = w_in.astype(BF16)
    nh = C_HEADS
    dqk = 2 * nh * C_QK_DIM
    qk = _proj(h, g, w[:, :dqk], "rot128", seq, tabs, scales=[1.0] * nh + [C_QK_DIM ** -0.5] * nh)
    v = _proj(h, g, w[:, dqk:dqk + D_INNER], "none", seq, tabs)
    z = _proj(h, g, w[:, dqk + D_INNER:], "silu", seq, tabs)
    n = qk.shape[0]
    log_gamma = jnp.log1p(-jnp.exp2(-5.0 - jnp.arange(nh, dtype=F32)))
    return pl.pallas_call(
        functools.partial(_retention_body, seq=seq),
        grid=(batch, nh),
        in_specs=[
            pl.BlockSpec(memory_space=pltpu.SMEM),
            pl.BlockSpec((seq, C_QK_DIM), lambda b, h: (b, h)),
            pl.BlockSpec((seq, C_QK_DIM), lambda b, h: (b, nh + h)),
            pl.BlockSpec((seq, C_V_DIM), lambda b, h: (b, h)),
            pl.BlockSpec((seq, C_V_DIM), lambda b, h: (b, h)),
        ],
        out_specs=pl.BlockSpec((seq, C_V_DIM), lambda b, h: (b, h)),
        out_shape=jax.ShapeDtypeStruct((n, D_INNER), BF16),
        scratch_shapes=[pltpu.VMEM((C_V_DIM, C_QK_DIM), F32)],
        compiler_params=_cparams(("parallel", "parallel")),
        name="retention",
    )(log_gamma, qk, qk, v, z)


def _hgrn_body(q_ref, g_ref, v_ref, z_ref, o_ref, st_ref, *, seq):
    c = D_CHUNK
    w = D_HEAD_GROUP * LANES
    halves = [D_SUB * 2 ** lv for lv in range(16) if D_SUB * 2 ** (lv + 1) <= c]
    ri = lax.broadcasted_iota(jnp.int32, (c, c), 0)
    ci = lax.broadcasted_iota(jnp.int32, (c, c), 1)
    mats = [ri >= ci]
    pair_masks = []
    for half in halves:
        block = 2 * half
        mid = (ri // block) * block + half - 1
        right = (ri % block) >= half
        mats.append((right & (ci > mid) & (ci <= ri)) | (jnp.logical_not(right) & (ci > ri) & (ci <= mid)))
        pair_masks.append(((ri // block) == (ci // block)) & right & ((ci % block) < half))
    diag_masks = []
    for j in range(D_SUB):
        key = (ri // D_SUB) * D_SUB + j
        mats.append((ci > key) & (ci <= ri))
        diag_masks.append((ci == key) & (ri >= key))
    m1 = jnp.concatenate([m.astype(BF16) for m in mats], axis=0)
    m3 = jnp.concatenate([m1, m1, m1], axis=1)
    rows = lax.broadcasted_iota(jnp.int32, (c, w), 0)
    right_rows = [(rows % (2 * half)) >= half for half in halves]
    n_lv = len(halves)
    st_ref[...] = jnp.zeros(st_ref.shape, F32)

    head_sl = [slice(hd * LANES, (hd + 1) * LANES) for hd in range(D_HEAD_GROUP)]
    unroll = min(D_UNROLL, seq // c)

    def group_rows(gidx):
        return [pl.multiple_of((gidx * unroll + u) * c, c) for u in range(unroll)]

    def prepare(gidx):
        rows0 = group_rows(gidx)
        gates = [g_ref[pl.ds(r0, c), :] for r0 in rows0]
        bds = []
        for gate in gates:
            lf = jnp.log2(gate)
            hi = lf.astype(BF16)
            r1 = lf - hi.astype(F32)
            mid_part = r1.astype(BF16)
            lo = (r1 - mid_part.astype(F32)).astype(BF16)
            bds.append(_dot(m3, jnp.concatenate([hi, mid_part, lo], axis=0)))
        prep = []
        for u, r0 in enumerate(rows0):
            bd, kk = bds[u], 1.0 - gates[u]
            b = bd[0:c]
            q = q_ref[pl.ds(r0, c), :].astype(F32)
            v = v_ref[pl.ds(r0, c), :]
            qe = (q * jnp.exp2(b)).astype(BF16)
            qts, kts = [], []
            for lv in range(len(halves)):
                e = jnp.exp2(bd[(lv + 1) * c:(lv + 2) * c])
                qts.append(jnp.where(right_rows[lv], q * e, 0.0).astype(BF16))
                kts.append(jnp.where(right_rows[lv], 0.0, kk * e).astype(BF16))
            q_diag = jnp.concatenate(
                [(q * jnp.exp2(bd[(1 + n_lv + j) * c:(2 + n_lv + j) * c])).astype(BF16) for j in range(D_SUB)], axis=0)
            b_last = b[c - 1:c, :]
            k_dec = (kk * jnp.exp2(b_last - b)).astype(BF16)
            prep.append((qe, qts, kts, v, (q_diag, kk.astype(BF16)), k_dec, jnp.exp2(b_last)))
        o_local = []
        for qe, qts, kts, v, (q_diag, kk16), k_dec, eb_last in prep:
            per_head = []
            for hd, sl in enumerate(head_sl):
                a = jnp.zeros((c, c), F32)
                for lv in range(n_lv):
                    a = a + jnp.where(pair_masks[lv], _dot_nt(qts[lv][:, sl], kts[lv][:, sl]), 0.0)
                g_diag = _dot_nt(q_diag[:, sl], kk16[:, sl])
                for j in range(D_SUB):
                    a = a + jnp.where(diag_masks[j], g_diag[j * c:(j + 1) * c], 0.0)
                per_head.append(_dot(a.astype(BF16), v[:, sl]))
            o_local.append(per_head)
        return [(o_local[u], p[0], p[3], p[5], p[6]) for u, p in enumerate(prep)]

    def finish(gidx, prepared):
        outs = []
        for o_loc, qe, v, k_dec, eb_last in prepared:
            per_head = []
            for hd, sl in enumerate(head_sl):
                st = st_ref[hd]
                per_head.append(o_loc[hd] + _dot_nt(qe[:, sl], st.astype(BF16)))
                st_ref[hd] = st * eb_last[:, sl] + _dot_tn(v[:, sl], k_dec[:, sl])
            outs.append(per_head)
        for u, r0 in enumerate(group_rows(gidx)):
            for hd, sl in enumerate(head_sl):
                o = outs[u][hd]
                o = o * lax.rsqrt(jnp.mean(o * o, axis=-1, keepdims=True) + NORM_EPS)
                o_ref[pl.ds(r0, c), sl] = (o * z_ref[pl.ds(r0, c), sl].astype(F32)).astype(o_ref.dtype)

    def trip(gidx, carry):
        finish(gidx, prepare(gidx))
        return carry

    lax.fori_loop(0, seq // (c * unroll), trip, 0)


def _mixer_d(h, g, w_in, w_out, lower_bound, batch, seq, tabs, zero_aux):
    wq, wf, wi, wz = (w_in[:, t * D_INNER:(t + 1) * D_INNER] for t in range(4))
    qz = _proj(h, g, jnp.concatenate([wq, wz], axis=1).astype(BF16), "silu", seq, tabs,
               scales=[D_K_DIM ** -0.5] * D_HEADS + [1.0] * D_HEADS)
    v = _proj(h, g, wi.astype(BF16), "none", seq, tabs)
    gates = _proj(h, g, wf.astype(BF16), "hgrn_g", seq, tabs, out_dtype=F32,
                  aux=lower_bound.reshape(1, D_INNER).astype(F32))
    n = qz.shape[0]
    ng = D_HEADS // D_HEAD_GROUP
    w = D_HEAD_GROUP * LANES
    return pl.pallas_call(
        functools.partial(_hgrn_body, seq=seq),
        grid=(batch, ng),
        in_specs=[
            pl.BlockSpec((seq, w), lambda b, h: (b, h)),
            pl.BlockSpec((seq, w), lambda b, h: (b, h)),
            pl.BlockSpec((seq, w), lambda b, h: (b, h)),
            pl.BlockSpec((seq, w), lambda b, h: (b, ng + h)),
        ],
        out_specs=pl.BlockSpec((seq, w), lambda b, h: (b, h)),
        out_shape=jax.ShapeDtypeStruct((n, D_INNER), BF16),
        scratch_shapes=[pltpu.VMEM((D_HEAD_GROUP, D_V_DIM, D_K_DIM), F32)],
        compiler_params=_cparams(("parallel", "parallel")),
        name="hgrn2_recurrence",
    )(qz, gates, v, qz)


def kernel(x, norm_g, final_g, a_w_in, a_w_out, b_w_in, b_w_out, c_w_in, c_w_out, d_w_in, d_w_out, hgrn_lb_logits):
    batch, seq, d = x.shape
    depth = norm_g.shape[0]
    lb_cum = jnp.cumsum(jax.nn.softmax(hgrn_lb_logits.astype(F32), axis=0), axis=0)
    lower_bounds = lb_cum - lb_cum[0]
    tabs = _rope_tables(seq, 128) + _rope_tables(seq, 64)
    h = x.reshape(batch * seq, d)
    for layer in range(depth):
        kind, slot = layer % N_MIXERS, layer // N_MIXERS
        g = norm_g[layer]
        if kind == 0:
            og, w_out = _mixer_a(h, g, a_w_in[slot], a_w_out[slot], batch, seq, tabs, None), a_w_out[slot]
        elif kind == 1:
            og, w_out = _mixer_b(h, g, b_w_in[slot], b_w_out[slot], batch, seq, tabs, None), b_w_out[slot]
        elif kind == 2:
            og, w_out = _mixer_c(h, g, c_w_in[slot], c_w_out[slot], batch, seq, tabs, None), c_w_out[slot]
        else:
            og, w_out = _mixer_d(h, g, d_w_in[slot], d_w_out[slot], lower_bounds[layer], batch, seq, tabs, None), d_w_out[slot]
        h = _outproj(og, w_out.astype(BF16), h, final_g, final=(layer == depth - 1))
    return h.reshape(batch, seq, d)
```

```python
import functools

import jax
import jax.numpy as jnp
import numpy as np
from jax import lax
from jax.experimental import pallas as pl
from jax.experimental.pallas import tpu as pltpu

F32 = jnp.float32
BF16 = jnp.bfloat16

D_MODEL = 1024
D_INNER = 2048
N_MIXERS = 4
NORM_EPS = 1e-6
ROPE_THETA = 10000.0
LANES = 128
NEG_BIG = -1e30
LOG2E = 1.4426950408889634

A_HEADS, A_HEAD_DIM = 16, 128
A_IDX_HEADS, A_IDX_DIM = 16, 64
A_TOPK_MAX = 256
A_Q_BLOCK = 256
A_K_BLOCK = 512
A_BATCH_GROUP = 4
A_CNT_ROWS = 64
A_HEAD_STAGE = 16
B_HEADS, B_HEAD_DIM = 16, 128
B_CONFIGS = ((128, 1), (512, 4), (2048, 16))
B_DIL_RATIO = 4
B_BATCH_GROUP = 4
C_HEADS, C_QK_DIM, C_V_DIM, C_CHUNK = 8, 128, 256, 128
C_UNROLL = 8
D_HEADS, D_K_DIM, D_V_DIM = 16, 128, 128
D_CHUNK = 64
D_SUB = 2
D_HEAD_GROUP = 2
D_UNROLL = 8

PROJ_TILE_N = 2048
PROJ_TILE_N_RES = 1024
VMEM_LIMIT = 56 * 1024 * 1024


def _cparams(sem):
    return pltpu.CompilerParams(dimension_semantics=sem, vmem_limit_bytes=VMEM_LIMIT)


def _dot(a, b):
    return jnp.dot(a, b, preferred_element_type=F32)


def _dot_nt(a, b):
    return lax.dot_general(a, b, (((1,), (1,)), ((), ())), preferred_element_type=F32)


def _dot_tn(a, b):
    return lax.dot_general(a, b, (((0,), (0,)), ((), ())), preferred_element_type=F32)


def _rope_tables(seq, dh):
    freqs = ROPE_THETA ** (-jnp.arange(0, dh, 2, dtype=F32) / dh)
    ang = jnp.arange(seq, dtype=F32)[:, None] * freqs[None, :]
    cos, sin = jnp.cos(ang), jnp.sin(ang)
    cosf = jnp.concatenate([cos, cos], axis=-1)
    sinf = jnp.concatenate([-sin, sin], axis=-1)
    rep = LANES // dh
    return jnp.tile(cosf, (1, rep)), jnp.tile(sinf, (1, rep))


def _epilogue(kind, scale, x, tabs, aux):
    c128, s128, c64, s64 = tabs
    if kind == "rot128":
        y = x * c128 + pltpu.roll(x, 64, 1) * s128
    elif kind == "rot64":
        lane = lax.broadcasted_iota(jnp.int32, x.shape, 1)
        partner = jnp.where((lane % 64) < 32, pltpu.roll(x, 96, 1), pltpu.roll(x, 32, 1))
        y = x * c64 + partner * s64
    elif kind == "silu":
        y = x * jax.nn.sigmoid(x)
    elif kind == "hgrn_g":
        y = aux + (1.0 - aux) * jax.nn.sigmoid(x)
    else:
        y = x
    if scale is not None:
        y = y * scale
    return y


def _proj_body(h_ref, g_ref, w_ref, c128_ref, s128_ref, c64_ref, s64_ref, aux_ref, scale_ref, *rest,
               kinds, scaled, res_dils):
    o_ref = rest[0]
    res_refs = rest[1:1 + len(res_dils)]
    u_ref = rest[1 + len(res_dils)]
    tm = o_ref.shape[0]

    @pl.when(pl.program_id(1) == 0)
    def _norm():
        x = h_ref[...]
        ms = jnp.mean(x * x, axis=-1, keepdims=True)
        u_ref[...] = (x * lax.rsqrt(ms + NORM_EPS) * g_ref[...]).astype(BF16)

    tabs = (c128_ref[...] if "rot128" in kinds else None, s128_ref[...] if "rot128" in kinds else None,
            c64_ref[...] if "rot64" in kinds else None, s64_ref[...] if "rot64" in kinds else None)
    def pair_dot(p):
        return _dot(u_ref[...], w_ref[:, 2 * p * LANES:min(2 * p + 2, len(kinds)) * LANES])

    n_pairs = (len(kinds) + 1) // 2
    accs = {0: pair_dot(0)}
    for c, kind in enumerate(kinds):
        sl = slice(c * LANES, (c + 1) * LANES)
        aux = aux_ref[:, sl] if kind == "hgrn_g" else None
        if c % 2 == 0 and c // 2 + 1 < n_pairs:
            accs[c // 2 + 1] = pair_dot(c // 2 + 1)
        acc = accs[c // 2][:, (c % 2) * LANES:(c % 2 + 1) * LANES]
        y = _epilogue(kind, scale_ref[:, sl] if scaled else None, acc, tabs, aux)
        o_ref[:, sl] = y.astype(o_ref.dtype)
        if res_dils:
            y_ref = rest[-1]
            y_ref[c] = y
            for dil, r_ref in zip(res_dils, res_refs):
                for r in range(dil):
                    r_ref[r, :, sl] = y_ref[c, pl.ds(r, tm // dil, stride=dil), :].astype(r_ref.dtype)


def _proj(h, g, w, kinds, seq, tabs, *, scales=None, aux=None, out_dtype=None, res_dils=()):
    n, d = h.shape
    c = w.shape[1]
    tm = min(1024, seq)
    tn = min(PROJ_TILE_N_RES if res_dils else PROJ_TILE_N, c)
    if isinstance(kinds, str):
        kinds = (kinds,) * (tn // LANES)
    assert len(kinds) == tn // LANES and (c == tn or len(set(kinds)) == 1)
    if scales is None:
        scale_row = jnp.zeros((1, c), F32)
    else:
        scale_row = jnp.repeat(jnp.asarray(scales, F32), LANES).reshape(1, c)
    if aux is None:
        aux = jnp.zeros((1, c), F32)
    out_dtype = out_dtype or BF16
    pos_blocks = seq // tm
    tab_spec = pl.BlockSpec((tm, LANES), lambda i, j: (i % pos_blocks, 0))
    row_spec = pl.BlockSpec((1, tn), lambda i, j: (0, j))
    out_specs = [pl.BlockSpec((tm, tn), lambda i, j: (i, j))]
    out_shape = [jax.ShapeDtypeStruct((n, c), out_dtype)]
    scratch = [pltpu.VMEM((tm, d), BF16)]
    for dil in res_dils:
        out_specs.append(pl.BlockSpec((dil, tm // dil, tn), lambda i, j: (0, i, j)))
        out_shape.append(jax.ShapeDtypeStruct((dil, n // dil, c), out_dtype))
    if res_dils:
        scratch.append(pltpu.VMEM((tn // LANES, tm, LANES), F32))
    res = pl.pallas_call(
        functools.partial(_proj_body, kinds=tuple(kinds), scaled=scales is not None, res_dils=tuple(res_dils)),
        grid=(n // tm, c // tn),
        in_specs=[
            pl.BlockSpec((tm, d), lambda i, j: (i, 0)),
            pl.BlockSpec((1, d), lambda i, j: (0, 0)),
            pl.BlockSpec((d, tn), lambda i, j: (0, j)),
            tab_spec, tab_spec, tab_spec, tab_spec,
            row_spec, row_spec,
        ],
        out_specs=out_specs,
        out_shape=out_shape,
        scratch_shapes=scratch,
        compiler_params=_cparams(("parallel", "arbitrary")),
        name="norm_in_proj",
    )(h, g.reshape(1, d), w, *tabs, aux, scale_row)
    return res if res_dils else res[0]


def _outproj_body(og_ref, w_ref, h_ref, gf_ref, o_ref, *, final):
    y = h_ref[...] + _dot(og_ref[...], w_ref[...])
    if final:
        ms = jnp.mean(y * y, axis=-1, keepdims=True)
        y = y * lax.rsqrt(ms + NORM_EPS) * gf_ref[...]
    o_ref[...] = y


def _outproj(og, w, h, gf, final):
    n, di = og.shape
    d = w.shape[1]
    tm = min(1024, n)
    return pl.pallas_call(
        functools.partial(_outproj_body, final=final),
        grid=(n // tm,),
        in_specs=[
            pl.BlockSpec((tm, di), lambda i: (i, 0)),
            pl.BlockSpec((di, d), lambda i: (0, 0)),
            pl.BlockSpec((tm, d), lambda i: (i, 0)),
            pl.BlockSpec((1, d), lambda i: (0, 0)),
        ],
        out_specs=pl.BlockSpec((tm, d), lambda i: (i, 0)),
        out_shape=jax.ShapeDtypeStruct((n, d), F32),
        compiler_params=_cparams(("parallel",)),
        name="out_proj_residual",
    )(og, w, h, gf.reshape(1, d))


def _dsa_index_body(iq_ref, ik_ref, iw_ref, bias_ref, sc_ref, *, topk, tq, kc, seq):
    i = pl.program_id(1)
    n_chunks = ((i + 1) * tq + kc - 1) // kc
    int_min = jnp.int32(-2 ** 31)
    iw_t = iw_ref[...].astype(F32).T
    lane = lax.broadcasted_iota(jnp.int32, (kc, LANES), 1)
    s_loc = lax.broadcasted_iota(jnp.int32, (kc, tq), 0)
    t_pos = i * tq + lax.broadcasted_iota(jnp.int32, (kc, tq), 1)

    def score_chunk(c, carry):
        k0 = pl.multiple_of(c * kc, kc)
        ik2 = ik_ref[pl.ds(k0, kc), :]
        ik_lo = jnp.where(lane < A_IDX_DIM, ik2, jnp.zeros_like(ik2))
        ik_hi = jnp.where(lane >= A_IDX_DIM, ik2, jnp.zeros_like(ik2))
        score = jnp.zeros((kc, tq), F32)
        for p in range(A_IDX_HEADS // 2):
            iq2 = iq_ref[:, p * LANES:(p + 1) * LANES]
            score += jnp.maximum(_dot_nt(ik_lo, iq2), 0.0) * iw_t[2 * p:2 * p + 1, :]
            score += jnp.maximum(_dot_nt(ik_hi, iq2), 0.0) * iw_t[2 * p + 1:2 * p + 2, :]
        sc_ref[pl.ds(k0, kc), :] = jnp.where(k0 + s_loc <= t_pos, score, -jnp.inf)
        return carry

    lax.fori_loop(0, n_chunks, score_chunk, 0)

    def threshold(prefix):
        key = prefix ^ int_min
        return pltpu.bitcast(jnp.where(key < 0, key ^ jnp.int32(0x7FFFFFFF), key), F32)

    def bisect(it, prefix):
        cand = prefix | (jnp.int32(1) << (31 - it))
        thr = threshold(cand)

        def count(c, part):
            blk = sc_ref[pl.ds(pl.multiple_of(c * kc, kc), kc), :]
            hits = jnp.where(blk >= thr, 1.0, 0.0).reshape(kc // A_CNT_ROWS, A_CNT_ROWS, tq)
            return part + jnp.sum(hits, axis=0)

        part = lax.fori_loop(0, n_chunks, count, jnp.zeros((A_CNT_ROWS, tq), F32))
        cnt = jnp.sum(part, axis=0, keepdims=True)
        return jnp.where(cnt >= topk, cand, prefix)

    n_iter = jnp.where((i + 1) * tq <= topk, 0, 32)
    prefix = lax.fori_loop(0, n_iter, bisect, jnp.zeros((1, tq), jnp.int32))
    take_all = prefix == 0
    thr = threshold(jnp.where(take_all, jnp.int32(0x00800000), prefix))

    def emit(c, carry):
        k0 = pl.multiple_of(c * kc, kc)
        sel = ((sc_ref[pl.ds(k0, kc), :] >= thr) | take_all) & (k0 + s_loc <= t_pos)
        bias_ref[pl.ds(k0, kc), :] = jnp.where(sel, 0.0, NEG_BIG).astype(bias_ref.dtype)
        return carry

    lax.fori_loop(0, n_chunks, emit, 0)

    def emit_masked(c, carry):
        bias_ref[pl.ds(pl.multiple_of(c * kc, kc), kc), :] = jnp.full((kc, tq), NEG_BIG, bias_ref.dtype)
        return carry

    lax.fori_loop(n_chunks, seq // kc, emit_masked, 0)


def _dsa_index(aux, batch, seq, topk):
    c_ik = A_IDX_HEADS * A_IDX_DIM // LANES + 2
    tq = min(A_Q_BLOCK, seq)
    kc = min(A_K_BLOCK, seq)
    nq = seq // tq
    return pl.pallas_call(
        functools.partial(_dsa_index_body, topk=topk, tq=tq, kc=kc, seq=seq),
        grid=(batch, nq),
        in_specs=[
            pl.BlockSpec((tq, A_IDX_HEADS * A_IDX_DIM), lambda b, i: (b * nq + i, 0)),
            pl.BlockSpec((seq, LANES), lambda b, i: (b, c_ik)),
            pl.BlockSpec((tq, LANES), lambda b, i: (b * nq + i, c_ik + 1)),
        ],
        out_specs=pl.BlockSpec((None, None, seq, tq), lambda b, i: (b, i, 0, 0)),
        out_shape=jax.ShapeDtypeStruct((batch, nq, seq, tq), BF16),
        scratch_shapes=[pltpu.VMEM((seq, tq), F32)],
        compiler_params=_cparams(("parallel", "parallel")),
        name="dsa_indexer_topk_mask",
    )(aux, aux, aux)


def _dsa_attn_body(qi_ref, kj_ref, q_ref, z_ref, k_ref, v_ref, bias_ref, o_ref, m_ref, l_ref, acc_ref, *, tq, kb):
    t = pl.program_id(1)
    i = qi_ref[t]
    j = kj_ref[t]
    nh = A_HEADS
    dv = A_HEAD_DIM

    @pl.when(j == 0)
    def _init():
        m_ref[...] = jnp.full(m_ref.shape, NEG_BIG, F32)
        l_ref[...] = jnp.zeros(l_ref.shape, F32)
        acc_ref[...] = jnp.zeros(acc_ref.shape, F32)

    for g in range(q_ref.shape[0]):
        k = k_ref[g]
        v_t = v_ref[g].astype(F32).T.astype(BF16)
        bias = bias_ref[g].astype(F32)
        for h0 in range(0, nh, A_HEAD_STAGE):
            hs = range(h0, h0 + A_HEAD_STAGE)
            s = [_dot_nt(k, q_ref[g, :, h * LANES:(h + 1) * LANES]) + bias for h in hs]
            ps, alphas = [], []
            for n, h in enumerate(hs):
                m_old = m_ref[g, h:h + 1, :]
                m_new = jnp.maximum(m_old, jnp.max(s[n], axis=0, keepdims=True))
                alpha = jnp.exp2(m_old - m_new)
                p = jnp.exp2(s[n] - m_new)
                l_ref[g, h:h + 1, :] = alpha * l_ref[g, h:h + 1, :] + jnp.sum(p, axis=0, keepdims=True)
                m_ref[g, h:h + 1, :] = m_new
                ps.append(p.astype(BF16))
                alphas.append(alpha)
            for n, h in enumerate(hs):
                rows = slice(h * dv, (h + 1) * dv)
                acc_ref[g, rows, :] = alphas[n] * acc_ref[g, rows, :] + _dot(v_t, ps[n])

    @pl.when(j == (i * tq + tq - 1) // kb)
    def _fin():
        for g in range(q_ref.shape[0]):
            for h in range(nh):
                sl = slice(h * LANES, (h + 1) * LANES)
                o_t = acc_ref[g, h * dv:(h + 1) * dv, :] * (1.0 / l_ref[g, h:h + 1, :])
                o_ref[g, :, sl] = (o_t.T * z_ref[g, :, sl].astype(F32)).astype(o_ref.dtype)


def _dsa_attn(q, z, aux, bias, batch, seq):
    n = q.shape[0]
    c_k = A_IDX_HEADS * A_IDX_DIM // LANES
    tq = bias.shape[-1]
    kb = min(A_K_BLOCK, seq)
    nq, nk = seq // tq, seq // kb
    pairs = [(i, j) for i in range(nq) for j in range((i * tq + tq - 1) // kb + 1)]
    qi = jnp.asarray([p[0] for p in pairs], jnp.int32)
    kj = jnp.asarray([p[1] for p in pairs], jnp.int32)
    nb = A_BATCH_GROUP if batch % A_BATCH_GROUP == 0 else 1

    def per_batch(x):
        return x.reshape(batch, seq, x.shape[-1])

    grid_spec = pltpu.PrefetchScalarGridSpec(
        num_scalar_prefetch=2,
        grid=(batch // nb, len(pairs)),
        in_specs=[
            pl.BlockSpec((nb, tq, D_INNER), lambda b, t, qi, kj: (b, qi[t], 0)),
            pl.BlockSpec((nb, tq, D_INNER), lambda b, t, qi, kj: (b, qi[t], 0)),
            pl.BlockSpec((nb, kb, LANES), lambda b, t, qi, kj: (b, kj[t], c_k)),
            pl.BlockSpec((nb, kb, LANES), lambda b, t, qi, kj: (b, kj[t], c_k + 1)),
            pl.BlockSpec((nb, None, kb, tq), lambda b, t, qi, kj: (b, qi[t], kj[t], 0)),
        ],
        out_specs=pl.BlockSpec((nb, tq, D_INNER), lambda b, t, qi, kj: (b, qi[t], 0)),
        scratch_shapes=[
            pltpu.VMEM((nb, A_HEADS, tq), F32),
            pltpu.VMEM((nb, A_HEADS, tq), F32),
            pltpu.VMEM((nb, A_HEADS * A_HEAD_DIM, tq), F32),
        ],
    )
    return pl.pallas_call(
        functools.partial(_dsa_attn_body, tq=tq, kb=kb),
        grid_spec=grid_spec,
        out_shape=jax.ShapeDtypeStruct((batch, seq, D_INNER), BF16),
        compiler_params=_cparams(("parallel", "arbitrary")),
        name="dsa_masked_attention",
    )(qi, kj, per_batch(q), per_batch(z), per_batch(aux), per_batch(aux), bias).reshape(n, D_INNER)


def _mixer_a(h, g, w_in, w_out, batch, seq, tabs, zero_aux):
    hd = A_HEADS * A_HEAD_DIM
    hi = A_IDX_HEADS * A_IDX_DIM
    o0 = np.cumsum([0, hd, A_HEAD_DIM, A_HEAD_DIM, hi, A_IDX_DIM, A_IDX_HEADS, D_INNER])
    wq, wk, wv, wiq, wik, wiw, wz = (w_in[:, o0[t]:o0[t + 1]] for t in range(7))
    pad = jnp.zeros((D_MODEL, LANES - A_IDX_HEADS), F32)
    w_aux = jnp.concatenate([wiq, wk, wv, wik, wik, wiw, pad], axis=1).astype(BF16)
    n_iq = hi // LANES
    q = _proj(h, g, wq.astype(BF16), "rot128", seq, tabs, scales=[A_HEAD_DIM ** -0.5 * LOG2E] * A_HEADS)
    z = _proj(h, g, wz.astype(BF16), "silu", seq, tabs)
    aux = _proj(h, g, w_aux, ["rot64"] * n_iq + ["rot128", "none", "rot64", "none"], seq, tabs,
                scales=[1.0] * (n_iq + 3) + [A_IDX_HEADS ** -0.5 * A_IDX_DIM ** -0.5])
    bias = _dsa_index(aux, batch, seq, min(A_TOPK_MAX, seq // 4))
    return _dsa_attn(q, z, aux, bias, batch, seq)


def _band_body(*refs, blk, band_prev, cfg_prev, final):
    refs = list(refs)
    q_ref, kc_ref, vc_ref = refs[:3]
    del refs[:3]
    if band_prev:
        kp_ref, vp_ref = refs[-2:]
        del refs[-2:]

        @pl.when(pl.program_id(2) == 0)
        def _no_prev():
            kp_ref[...] = jnp.zeros(kp_ref.shape, kp_ref.dtype)
            vp_ref[...] = jnp.zeros(vp_ref.shape, vp_ref.dtype)
    nb = q_ref.shape[0]
    if cfg_prev:
        op4_ref, lp4_ref = refs[:2]
        del refs[:2]
        op_ref, lp_ref = refs[-2:]
        del refs[-2:]
        for g in range(nb):
            for a in range(B_DIL_RATIO):
                rows = pl.ds(a, blk // B_DIL_RATIO, stride=B_DIL_RATIO)
                lp_ref[g, rows, :] = lp4_ref[a, g]
                for h in range(B_HEADS):
                    op_ref[g, h, rows, :] = op4_ref[a, g, :, h * LANES:(h + 1) * LANES].astype(F32)
    if final:
        z_ref, o_ref = refs
    else:
        o_ref, l_ref = refs
    ki = lax.broadcasted_iota(jnp.int32, (blk, blk), 0)
    qi = lax.broadcasted_iota(jnp.int32, (blk, blk), 1)
    mask_cur = ki <= qi
    eye = ki == qi
    if band_prev:
        mask_prev = (ki >= qi) & (pl.program_id(2) > 0)
    if cfg_prev:
        lse_prev_t = [_pad_rows(lp_ref[g].T, LANES) for g in range(nb)]
    items = [(g, slice(h * LANES, (h + 1) * LANES)) for g in range(nb) for h in range(B_HEADS)]
    s_c = [jnp.where(mask_cur, _dot_nt(kc_ref[g, :, sl], q_ref[g, :, sl]), NEG_BIG) for g, sl in items]
    if band_prev:
        s_p = [jnp.where(mask_prev, _dot_nt(kp_ref[g, :, sl], q_ref[g, :, sl]), NEG_BIG) for g, sl in items]
    lse_rows, p_c, p_p, d_old = [], [], [], []
    for n, (g, sl) in enumerate(items):
        m = jnp.max(s_c[n], axis=0, keepdims=True)
        if band_prev:
            m = jnp.maximum(m, jnp.max(s_p[n], axis=0, keepdims=True))
        e_c = jnp.exp2(s_c[n] - m)
        den = jnp.sum(e_c, axis=0, keepdims=True)
        if band_prev:
            e_p = jnp.exp2(s_p[n] - m)
            den += jnp.sum(e_p, axis=0, keepdims=True)
        lse = m + jnp.log2(den)
        scale = 1.0 / den
        if cfg_prev:
            h = n % B_HEADS
            lse_prev = lse_prev_t[g][h:h + 1, :]
            m2 = jnp.maximum(lse, lse_prev)
            w_new, w_old = jnp.exp2(lse - m2), jnp.exp2(lse_prev - m2)
            scale = scale * w_new / (w_new + w_old)
            d_old.append(jnp.where(eye, w_old / (w_new + w_old), 0.0).astype(BF16))
            lse = m2 + jnp.log2(w_new + w_old)
        p_c.append((e_c * scale).astype(BF16))
        if band_prev:
            p_p.append((e_p * scale).astype(BF16))
        lse_rows.append(lse)
    for n, (g, sl) in enumerate(items):
        o = _dot_tn(p_c[n], vc_ref[g, :, sl])
        if band_prev:
            o += _dot_tn(p_p[n], vp_ref[g, :, sl])
        if cfg_prev:
            o += _dot(d_old[n], op_ref[g, n % B_HEADS].astype(BF16))
        if final:
            o_ref[g, :, sl] = (o * z_ref[g, :, sl].astype(F32)).astype(o_ref.dtype)
        else:
            o_ref[g, :, sl] = o.astype(o_ref.dtype)
    if not final:
        for g in range(nb):
            lse_t = _pad_rows(jnp.concatenate(lse_rows[g * B_HEADS:(g + 1) * B_HEADS], axis=0), LANES)
            l_ref[g] = lse_t.T
    if band_prev:
        kp_ref[...] = kc_ref[...]
        vp_ref[...] = vc_ref[...]


def _pad_rows(x, rows):
    if x.shape[0] == rows:
        return x
    return jnp.concatenate([x, jnp.zeros((rows - x.shape[0], x.shape[1]), x.dtype)], axis=0)


def _band_attn(qk, v, gate, batch, seq, dil, prev):
    final = gate is not None
    length = seq // dil
    blk = min(128, length)
    nbl = length // blk
    band_prev = nbl > 1
    nb = B_BATCH_GROUP if batch % B_BATCH_GROUP == 0 else 1

    def per_batch(x, classes=dil):
        return x.reshape(classes, batch, x.shape[1] // batch, x.shape[2])

    def spec(col, width=D_INNER):
        return pl.BlockSpec((None, nb, blk, width), lambda b, r, i: (r, b, i, col))

    o_spec, l_spec = spec(0), spec(0, LANES)
    in_specs, args, scratch = [spec(0), spec(1), spec(0)], [per_batch(qk), per_batch(qk), per_batch(v)], []
    if prev is not None:
        coarse = B_DIL_RATIO * dil
        assert prev[0].shape[0] == coarse and blk % (16 * B_DIL_RATIO) == 0
        pblk = blk // B_DIL_RATIO
        in_specs += [pl.BlockSpec((B_DIL_RATIO, None, nb, pblk, D_INNER), lambda b, r, i: (0, r, b, i, 0)),
                     pl.BlockSpec((B_DIL_RATIO, None, nb, pblk, LANES), lambda b, r, i: (0, r, b, i, 0))]
        args += [x.reshape(B_DIL_RATIO, dil, batch, length // B_DIL_RATIO, x.shape[-1]) for x in prev]
        scratch = [pltpu.VMEM((nb, B_HEADS, blk, LANES), F32), pltpu.VMEM((nb, blk, LANES), F32)]
    if band_prev:
        scratch += [pltpu.VMEM((nb, blk, D_INNER), BF16), pltpu.VMEM((nb, blk, D_INNER), BF16)]
    o_shape = jax.ShapeDtypeStruct((dil, batch, length, D_INNER), BF16)
    if final:
        in_specs.append(spec(0))
        args.append(per_batch(gate))
        out_specs, out_shape = o_spec, o_shape
    else:
        out_specs = [o_spec, l_spec]
        out_shape = [o_shape, jax.ShapeDtypeStruct((dil, batch, length, LANES), F32)]
    res = pl.pallas_call(
        functools.partial(_band_body, blk=blk, band_prev=band_prev, cfg_prev=prev is not None, final=final),
        grid=(batch // nb, dil, nbl),
        in_specs=in_specs,
        out_specs=out_specs,
        out_shape=out_shape,
        scratch_shapes=scratch,
        compiler_params=_cparams(("parallel", "parallel", "arbitrary")),
        name="dilated_band_attention",
    )(*args)
    if final:
        return res.reshape(dil, batch * length, D_INNER)
    return [x.reshape(dil, batch * length, x.shape[-1]) for x in res]


def _mixer_b(h, g, w_in, w_out, batch, seq, tabs, zero_aux):
    configs = sorted(B_CONFIGS, key=lambda wd: -wd[1])
    dils = [dil for _, dil in configs]
    assert dils[-1] == 1 and all(a == B_DIL_RATIO * b for a, b in zip(dils, dils[1:]))
    assert all(window // dil == 128 and seq % dil == 0 for window, dil in configs)
    w = w_in.astype(BF16)
    qk = _proj(h, g, w[:, :2 * D_INNER], "rot128", seq, tabs, res_dils=dils[:-1],
               scales=[B_HEAD_DIM ** -0.5 * LOG2E] * B_HEADS + [1.0] * B_HEADS)
    v = _proj(h, g, w[:, 2 * D_INNER:3 * D_INNER], "none", seq, tabs, res_dils=dils[:-1])
    z = _proj(h, g, w[:, 3 * D_INNER:], "silu", seq, tabs)
    qks = list(qk[1:]) + [qk[0][None]]
    vs = list(v[1:]) + [v[0][None]]
    prev = None
    for qk_d, v_d, dil in zip(qks, vs, dils):
        prev = _band_attn(qk_d, v_d, z[None] if dil == 1 else None, batch, seq, dil, prev)
    return prev.reshape(-1, D_INNER)


def _retention_body(lg_ref, q_ref, k_ref, v_ref, z_ref, o_ref, st_ref, *, seq):
    c = C_CHUNK
    lg = lg_ref[pl.program_id(1)]
    ri = lax.broadcasted_iota(jnp.int32, (c, c), 0)
    ci = lax.broadcasted_iota(jnp.int32, (c, c), 1)
    rel = (ri - ci).astype(F32)
    decay = jnp.where(rel >= 0, jnp.exp(jnp.maximum(rel, 0.0) * lg), 0.0)
    idx = lax.broadcasted_iota(jnp.int32, (c, 1), 0).astype(F32)
    k_decay = jnp.exp((c - 1 - idx) * lg)
    q_decay = jnp.exp((idx + 1.0) * lg)
    chunk_decay = jnp.exp(jnp.full((1, C_QK_DIM), c, F32) * lg)
    st_ref[...] = jnp.zeros(st_ref.shape, F32)

    unroll = min(C_UNROLL, seq // c)

    def group(gidx, carry):
        rows0 = [pl.multiple_of((gidx * unroll + u) * c, c) for u in range(unroll)]
        qs = [q_ref[pl.ds(r0, c), :] for r0 in rows0]
        ks = [k_ref[pl.ds(r0, c), :] for r0 in rows0]
        vs = [v_ref[pl.ds(r0, c), :] for r0 in rows0]
        scores = [(_dot_nt(q, k) * decay).astype(BF16) for q, k in zip(qs, ks)]
        o_intra = [_dot(s, v) for s, v in zip(scores, vs)]
        q_dec = [(q.astype(F32) * q_decay).astype(BF16) for q in qs]
        k_dec = [(k.astype(F32) * k_decay).astype(BF16) for k in ks]
        outs = []
        for u in range(unroll):
            st = st_ref[...]
            outs.append(o_intra[u] + _dot_nt(q_dec[u], st.astype(BF16)))
            st_ref[...] = st * chunk_decay + _dot_tn(vs[u], k_dec[u])
        for u, r0 in enumerate(rows0):
            o = outs[u]
            o = o * lax.rsqrt(jnp.mean(o * o, axis=-1, keepdims=True) + NORM_EPS)
            o_ref[pl.ds(r0, c), :] = (o * z_ref[pl.ds(r0, c), :].astype(F32)).astype(o_ref.dtype)
        return carry

    lax.fori_loop(0, seq // (c * unroll), group, 0)


def _mixer_c(h, g, w_in, w_out, batch, seq, tabs, zero_aux):
    w = w_in.astype(BF16)
    nh = C_HEADS
    dqk = 2 * nh * C_QK_DIM
    qk = _proj(h, g, w[:, :dqk], "rot128", seq, tabs, scales=[1.0] * nh + [C_QK_DIM ** -0.5] * nh)
    v = _proj(h, g, w[:, dqk:dqk + D_INNER], "none", seq, tabs)
    z = _proj(h, g, w[:, dqk + D_INNER:], "silu", seq, tabs)
    n = qk.shape[0]
    log_gamma = jnp.log1p(-jnp.exp2(-5.0 - jnp.arange(nh, dtype=F32)))
    return pl.pallas_call(
        functools.partial(_retention_body, seq=seq),
        grid=(batch, nh),
        in_specs=[
            pl.BlockSpec(memory_space=pltpu.SMEM),
            pl.BlockSpec((seq, C_QK_DIM), lambda b, h: (b, h)),
            pl.BlockSpec((seq, C_QK_DIM), lambda b, h: (b, nh + h)),
            pl.BlockSpec((seq, C_V_DIM), lambda b, h: (b, h)),
            pl.BlockSpec((seq, C_V_DIM), lambda b, h: (b, h)),
        ],
        out_specs=pl.BlockSpec((seq, C_V_DIM), lambda b, h: (b, h)),
        out_shape=jax.ShapeDtypeStruct((n, D_INNER), BF16),
        scratch_shapes=[pltpu.VMEM((C_V_DIM, C_QK_DIM), F32)],
        compiler_params=_cparams(("parallel", "parallel")),
        name="retention",
    )(log_gamma, qk, qk, v, z)


def _hgrn_body(q_ref, g_ref, v_ref, z_ref, o_ref, st_ref, *, seq):
    c = D_CHUNK
    w = D_HEAD_GROUP * LANES
    halves = [D_SUB * 2 ** lv for lv in range(16) if D_SUB * 2 ** (lv + 1) <= c]
    ri = lax.broadcasted_iota(jnp.int32, (c, c), 0)
    ci = lax.broadcasted_iota(jnp.int32, (c, c), 1)
    mats = [ri >= ci]
    pair_masks = []
    for half in halves:
        block = 2 * half
        mid = (ri // block) * block + half - 1
        right = (ri % block) >= half
        mats.append((right & (ci > mid) & (ci <= ri)) | (jnp.logical_not(right) & (ci > ri) & (ci <= mid)))
        pair_masks.append(((ri // block) == (ci // block)) & right & ((ci % block) < half))
    diag_masks = []
    for j in range(D_SUB):
        key = (ri // D_SUB) * D_SUB + j
        mats.append((ci > key) & (ci <= ri))
        diag_masks.append((ci == key) & (ri >= key))
    m1 = jnp.concatenate([m.astype(BF16) for m in mats], axis=0)
    m3 = jnp.concatenate([m1, m1, m1], axis=1)
    rows = lax.broadcasted_iota(jnp.int32, (c, w), 0)
    right_rows = [(rows % (2 * half)) >= half for half in halves]
    n_lv = len(halves)
    st_ref[...] = jnp.zeros(st_ref.shape, F32)

    head_sl = [slice(hd * LANES, (hd + 1) * LANES) for hd in range(D_HEAD_GROUP)]
    unroll = min(D_UNROLL, seq // c)

    def group_rows(gidx):
        return [pl.multiple_of((gidx * unroll + u) * c, c) for u in range(unroll)]

    def prepare(gidx):
        rows0 = group_rows(gidx)
        gates = [g_ref[pl.ds(r0, c), :] for r0 in rows0]
        bds = []
        for gate in gates:
            lf = jnp.log2(gate)
            hi = lf.astype(BF16)
            r1 = lf - hi.astype(F32)
            mid_part = r1.astype(BF16)
            lo = (r1 - mid_part.astype(F32)).astype(BF16)
            bds.append(_dot(m3, jnp.concatenate([hi, mid_part, lo], axis=0)))
        prep = []
        for u, r0 in enumerate(rows0):
            bd, kk = bds[u], 1.0 - gates[u]
            b = bd[0:c]
            q = q_ref[pl.ds(r0, c), :].astype(F32)
            v = v_ref[pl.ds(r0, c), :]
            qe = (q * jnp.exp2(b)).astype(BF16)
            qts, kts = [], []
            for lv in range(len(halves)):
                e = jnp.exp2(bd[(lv + 1) * c:(lv + 2) * c])
                qts.append(jnp.where(right_rows[lv], q * e, 0.0).astype(BF16))
                kts.append(jnp.where(right_rows[lv], 0.0, kk * e).astype(BF16))
            q_diag = jnp.concatenate(
                [(q * jnp.exp2(bd[(1 + n_lv + j) * c:(2 + n_lv + j) * c])).astype(BF16) for j in range(D_SUB)], axis=0)
            b_last = b[c - 1:c, :]
            k_dec = (kk * jnp.exp2(b_last - b)).astype(BF16)
            prep.append((qe, qts, kts, v, (q_diag, kk.astype(BF16)), k_dec, jnp.exp2(b_last)))
        o_local = []
        for qe, qts, kts, v, (q_diag, kk16), k_dec, eb_last in prep:
            per_head = []
            for hd, sl in enumerate(head_sl):
                a = jnp.zeros((c, c), F32)
                for lv in range(n_lv):
                    a = a + jnp.where(pair_masks[lv], _dot_nt(qts[lv][:, sl], kts[lv][:, sl]), 0.0)
                g_diag = _dot_nt(q_diag[:, sl], kk16[:, sl])
                for j in range(D_SUB):
                    a = a + jnp.where(diag_masks[j], g_diag[j * c:(j + 1) * c], 0.0)
                per_head.append(_dot(a.astype(BF16), v[:, sl]))
            o_local.append(per_head)
        return [(o_local[u], p[0], p[3], p[5], p[6]) for u, p in enumerate(prep)]

    def finish(gidx, prepared):
        outs = []
        for o_loc, qe, v, k_dec, eb_last in prepared:
            per_head = []
            for hd, sl in enumerate(head_sl):
                st = st_ref[hd]
                per_head.append(o_loc[hd] + _dot_nt(qe[:, sl], st.astype(BF16)))
                st_ref[hd] = st * eb_last[:, sl] + _dot_tn(v[:, sl], k_dec[:, sl])
            outs.append(per_head)
        for u, r0 in enumerate(group_rows(gidx)):
            for hd, sl in enumerate(head_sl):
                o = outs[u][hd]
                o = o * lax.rsqrt(jnp.mean(o * o, axis=-1, keepdims=True) + NORM_EPS)
                o_ref[pl.ds(r0, c), sl] = (o * z_ref[pl.ds(r0, c), sl].astype(F32)).astype(o_ref.dtype)

    def trip(gidx, carry):
        finish(gidx, prepare(gidx))
        return carry

    lax.fori_loop(0, seq // (c * unroll), trip, 0)


def _mixer_d(h, g, w_in, w_out, lower_bound, batch, seq, tabs, zero_aux):
    wq, wf, wi, wz = (w_in[:, t * D_INNER:(t + 1) * D_INNER] for t in range(4))
    qz = _proj(h, g, jnp.concatenate([wq, wz], axis=1).astype(BF16), "silu", seq, tabs,
               scales=[D_K_DIM ** -0.5] * D_HEADS + [1.0] * D_HEADS)
    v = _proj(h, g, wi.astype(BF16), "none", seq, tabs)
    gates = _proj(h, g, wf.astype(BF16), "hgrn_g", seq, tabs, out_dtype=F32,
                  aux=lower_bound.reshape(1, D_INNER).astype(F32))
    n = qz.shape[0]
    ng = D_HEADS // D_HEAD_GROUP
    w = D_HEAD_GROUP * LANES
    return pl.pallas_call(
        functools.partial(_hgrn_body, seq=seq),
        grid=(batch, ng),
        in_specs=[
            pl.BlockSpec((seq, w), lambda b, h: (b, h)),
            pl.BlockSpec((seq, w), lambda b, h: (b, h)),
            pl.BlockSpec((seq, w), lambda b, h: (b, h)),
            pl.BlockSpec((seq, w), lambda b, h: (b, ng + h)),
        ],
        out_specs=pl.BlockSpec((seq, w), lambda b, h: (b, h)),
        out_shape=jax.ShapeDtypeStruct((n, D_INNER), BF16),
        scratch_shapes=[pltpu.VMEM((D_HEAD_GROUP, D_V_DIM, D_K_DIM), F32)],
        compiler_params=_cparams(("parallel", "parallel")),
        name="hgrn2_recurrence",
    )(qz, gates, v, qz)


def kernel(x, norm_g, final_g, a_w_in, a_w_out, b_w_in, b_w_out, c_w_in, c_w_out, d_w_in, d_w_out, hgrn_lb_logits):
    batch, seq, d = x.shape
    depth = norm_g.shape[0]
    lb_cum = jnp.cumsum(jax.nn.softmax(hgrn_lb_logits.astype(F32), axis=0), axis=0)
    lower_bounds = lb_cum - lb_cum[0]
    tabs = _rope_tables(seq, 128) + _rope_tables(seq, 64)
    h = x.reshape(batch * seq, d)
    for layer in range(depth):
        kind, slot = layer % N_MIXERS, layer // N_MIXERS
        g = norm_g[layer]
        if kind == 0:
            og, w_out = _mixer_a(h, g, a_w_in[slot], a_w_out[slot], batch, seq, tabs, None), a_w_out[slot]
        elif kind == 1:
            og, w_out = _mixer_b(h, g, b_w_in[slot], b_w_out[slot], batch, seq, tabs, None), b_w_out[slot]
        elif kind == 2:
            og, w_out = _mixer_c(h, g, c_w_in[slot], c_w_out[slot], batch, seq, tabs, None), c_w_out[slot]
        else:
            og, w_out = _mixer_d(h, g, d_w_in[slot], d_w_out[slot], lower_bounds[layer], batch, seq, tabs, None), d_w_out[slot]
        h = _outproj(og, w_out.astype(BF16), h, final_g, final=(layer == depth - 1))
    return h.reshape(batch, seq, d)
```

```python
import functools

import jax
import jax.numpy as jnp
import numpy as np
from jax import lax
from jax.experimental import pallas as pl
from jax.experimental.pallas import tpu as pltpu

F32 = jnp.float32
BF16 = jnp.bfloat16

D_MODEL = 1024
D_INNER = 2048
N_MIXERS = 4
NORM_EPS = 1e-6
ROPE_THETA = 10000.0
LANES = 128
NEG_BIG = -1e30
LOG2E = 1.4426950408889634

A_HEADS, A_HEAD_DIM = 16, 128
A_IDX_HEADS, A_IDX_DIM = 16, 64
A_TOPK_MAX = 256
A_Q_BLOCK = 256
A_K_BLOCK = 512
A_BATCH_GROUP = 4
A_CNT_ROWS = 64
A_HEAD_STAGE = 16
B_HEADS, B_HEAD_DIM = 16, 128
B_CONFIGS = ((128, 1), (512, 4), (2048, 16))
B_DIL_RATIO = 4
B_BATCH_GROUP = 4
C_HEADS, C_QK_DIM, C_V_DIM, C_CHUNK = 8, 128, 256, 128
C_UNROLL = 8
D_HEADS, D_K_DIM, D_V_DIM = 16, 128, 128
D_CHUNK = 64
D_SUB = 2
D_HEAD_GROUP = 4
D_UNROLL = 4

PROJ_TILE_N = 2048
PROJ_TILE_N_RES = 1024
VMEM_LIMIT = 56 * 1024 * 1024


def _cparams(sem):
    return pltpu.CompilerParams(dimension_semantics=sem, vmem_limit_bytes=VMEM_LIMIT)


def _dot(a, b):
    return jnp.dot(a, b, preferred_element_type=F32)


def _dot_nt(a, b):
    return lax.dot_general(a, b, (((1,), (1,)), ((), ())), preferred_element_type=F32)


def _dot_tn(a, b):
    return lax.dot_general(a, b, (((0,), (0,)), ((), ())), preferred_element_type=F32)


def _rope_tables(seq, dh):
    freqs = ROPE_THETA ** (-jnp.arange(0, dh, 2, dtype=F32) / dh)
    ang = jnp.arange(seq, dtype=F32)[:, None] * freqs[None, :]
    cos, sin = jnp.cos(ang), jnp.sin(ang)
    cosf = jnp.concatenate([cos, cos], axis=-1)
    sinf = jnp.concatenate([-sin, sin], axis=-1)
    rep = LANES // dh
    return jnp.tile(cosf, (1, rep)), jnp.tile(sinf, (1, rep))


def _epilogue(kind, scale, x, tabs, aux):
    c128, s128, c64, s64 = tabs
    if kind == "rot128":
        y = x * c128 + pltpu.roll(x, 64, 1) * s128
    elif kind == "rot64":
        lane = lax.broadcasted_iota(jnp.int32, x.shape, 1)
        partner = jnp.where((lane % 64) < 32, pltpu.roll(x, 96, 1), pltpu.roll(x, 32, 1))
        y = x * c64 + partner * s64
    elif kind == "silu":
        y = x * jax.nn.sigmoid(x)
    elif kind == "hgrn_g":
        y = aux + (1.0 - aux) * jax.nn.sigmoid(x)
    else:
        y = x
    if scale is not None:
        y = y * scale
    return y


def _proj_body(h_ref, g_ref, w_ref, c128_ref, s128_ref, c64_ref, s64_ref, aux_ref, scale_ref, *rest,
               kinds, scaled, res_dils):
    o_ref = rest[0]
    res_refs = rest[1:1 + len(res_dils)]
    u_ref = rest[1 + len(res_dils)]
    tm = o_ref.shape[0]

    @pl.when(pl.program_id(1) == 0)
    def _norm():
        x = h_ref[...]
        ms = jnp.mean(x * x, axis=-1, keepdims=True)
        u_ref[...] = (x * lax.rsqrt(ms + NORM_EPS) * g_ref[...]).astype(BF16)

    tabs = (c128_ref[...] if "rot128" in kinds else None, s128_ref[...] if "rot128" in kinds else None,
            c64_ref[...] if "rot64" in kinds else None, s64_ref[...] if "rot64" in kinds else None)
    def pair_dot(p):
        return _dot(u_ref[...], w_ref[:, 2 * p * LANES:min(2 * p + 2, len(kinds)) * LANES])

    n_pairs = (len(kinds) + 1) // 2
    accs = {0: pair_dot(0)}
    for c, kind in enumerate(kinds):
        sl = slice(c * LANES, (c + 1) * LANES)
        aux = aux_ref[:, sl] if kind == "hgrn_g" else None
        if c % 2 == 0 and c // 2 + 1 < n_pairs:
            accs[c // 2 + 1] = pair_dot(c // 2 + 1)
        acc = accs[c // 2][:, (c % 2) * LANES:(c % 2 + 1) * LANES]
        y = _epilogue(kind, scale_ref[:, sl] if scaled else None, acc, tabs, aux)
        o_ref[:, sl] = y.astype(o_ref.dtype)
        if res_dils:
            y_ref = rest[-1]
            y_ref[c] = y
            for dil, r_ref in zip(res_dils, res_refs):
                for r in range(dil):
                    r_ref[r, :, sl] = y_ref[c, pl.ds(r, tm // dil, stride=dil), :].astype(r_ref.dtype)


def _proj(h, g, w, kinds, seq, tabs, *, scales=None, aux=None, out_dtype=None, res_dils=()):
    n, d = h.shape
    c = w.shape[1]
    tm = min(1024, seq)
    tn = min(PROJ_TILE_N_RES if res_dils else PROJ_TILE_N, c)
    if isinstance(kinds, str):
        kinds = (kinds,) * (tn // LANES)
    assert len(kinds) == tn // LANES and (c == tn or len(set(kinds)) == 1)
    if scales is None:
        scale_row = jnp.zeros((1, c), F32)
    else:
        scale_row = jnp.repeat(jnp.asarray(scales, F32), LANES).reshape(1, c)
    if aux is None:
        aux = jnp.zeros((1, c), F32)
    out_dtype = out_dtype or BF16
    pos_blocks = seq // tm
    tab_spec = pl.BlockSpec((tm, LANES), lambda i, j: (i % pos_blocks, 0))
    row_spec = pl.BlockSpec((1, tn), lambda i, j: (0, j))
    out_specs = [pl.BlockSpec((tm, tn), lambda i, j: (i, j))]
    out_shape = [jax.ShapeDtypeStruct((n, c), out_dtype)]
    scratch = [pltpu.VMEM((tm, d), BF16)]
    for dil in res_dils:
        out_specs.append(pl.BlockSpec((dil, tm // dil, tn), lambda i, j: (0, i, j)))
        out_shape.append(jax.ShapeDtypeStruct((dil, n // dil, c), out_dtype))
    if res_dils:
        scratch.append(pltpu.VMEM((tn // LANES, tm, LANES), F32))
    res = pl.pallas_call(
        functools.partial(_proj_body, kinds=tuple(kinds), scaled=scales is not None, res_dils=tuple(res_dils)),
        grid=(n // tm, c // tn),
        in_specs=[
            pl.BlockSpec((tm, d), lambda i, j: (i, 0)),
            pl.BlockSpec((1, d), lambda i, j: (0, 0)),
            pl.BlockSpec((d, tn), lambda i, j: (0, j)),
            tab_spec, tab_spec, tab_spec, tab_spec,
            row_spec, row_spec,
        ],
        out_specs=out_specs,
        out_shape=out_shape,
        scratch_shapes=scratch,
        compiler_params=_cparams(("parallel", "arbitrary")),
        name="norm_in_proj",
    )(h, g.reshape(1, d), w, *tabs, aux, scale_row)
    return res if res_dils else res[0]


def _outproj_body(og_ref, w_ref, h_ref, gf_ref, o_ref, *, final):
    y = h_ref[...] + _dot(og_ref[...], w_ref[...])
    if final:
        ms = jnp.mean(y * y, axis=-1, keepdims=True)
        y = y * lax.rsqrt(ms + NORM_EPS) * gf_ref[...]
    o_ref[...] = y


def _outproj(og, w, h, gf, final):
    n, di = og.shape
    d = w.shape[1]
    tm = min(1024, n)
    return pl.pallas_call(
        functools.partial(_outproj_body, final=final),
        grid=(n // tm,),
        in_specs=[
            pl.BlockSpec((tm, di), lambda i: (i, 0)),
            pl.BlockSpec((di, d), lambda i: (0, 0)),
            pl.BlockSpec((tm, d), lambda i: (i, 0)),
            pl.BlockSpec((1, d), lambda i: (0, 0)),
        ],
        out_specs=pl.BlockSpec((tm, d), lambda i: (i, 0)),
        out_shape=jax.ShapeDtypeStruct((n, d), F32),
        compiler_params=_cparams(("parallel",)),
        name="out_proj_residual",
    )(og, w, h, gf.reshape(1, d))


def _dsa_index_body(iq_ref, ik_ref, iw_ref, bias_ref, sc_ref, *, topk, tq, kc, seq):
    i = pl.program_id(1)
    n_chunks = ((i + 1) * tq + kc - 1) // kc
    int_min = jnp.int32(-2 ** 31)
    iw_t = iw_ref[...].astype(F32).T
    lane = lax.broadcasted_iota(jnp.int32, (kc, LANES), 1)
    s_loc = lax.broadcasted_iota(jnp.int32, (kc, tq), 0)
    t_pos = i * tq + lax.broadcasted_iota(jnp.int32, (kc, tq), 1)

    def score_chunk(c, carry):
        k0 = pl.multiple_of(c * kc, kc)
        ik2 = ik_ref[pl.ds(k0, kc), :]
        ik_lo = jnp.where(lane < A_IDX_DIM, ik2, jnp.zeros_like(ik2))
        ik_hi = jnp.where(lane >= A_IDX_DIM, ik2, jnp.zeros_like(ik2))
        score = jnp.zeros((kc, tq), F32)
        for p in range(A_IDX_HEADS // 2):
            iq2 = iq_ref[:, p * LANES:(p + 1) * LANES]
            score += jnp.maximum(_dot_nt(ik_lo, iq2), 0.0) * iw_t[2 * p:2 * p + 1, :]
            score += jnp.maximum(_dot_nt(ik_hi, iq2), 0.0) * iw_t[2 * p + 1:2 * p + 2, :]
        sc_ref[pl.ds(k0, kc), :] = jnp.where(k0 + s_loc <= t_pos, score, -jnp.inf)
        return carry

    lax.fori_loop(0, n_chunks, score_chunk, 0)

    def threshold(prefix):
        key = prefix ^ int_min
        return pltpu.bitcast(jnp.where(key < 0, key ^ jnp.int32(0x7FFFFFFF), key), F32)

    def bisect(it, prefix):
        cand = prefix | (jnp.int32(1) << (31 - it))
        thr = threshold(cand)

        def count(c, part):
            blk = sc_ref[pl.ds(pl.multiple_of(c * kc, kc), kc), :]
            hits = jnp.where(blk >= thr, 1.0, 0.0).reshape(kc // A_CNT_ROWS, A_CNT_ROWS, tq)
            return part + jnp.sum(hits, axis=0)

        part = lax.fori_loop(0, n_chunks, count, jnp.zeros((A_CNT_ROWS, tq), F32))
        cnt = jnp.sum(part, axis=0, keepdims=True)
        return jnp.where(cnt >= topk, cand, prefix)

    n_iter = jnp.where((i + 1) * tq <= topk, 0, 32)
    prefix = lax.fori_loop(0, n_iter, bisect, jnp.zeros((1, tq), jnp.int32))
    take_all = prefix == 0
    thr = threshold(jnp.where(take_all, jnp.int32(0x00800000), prefix))

    def emit(c, carry):
        k0 = pl.multiple_of(c * kc, kc)
        sel = ((sc_ref[pl.ds(k0, kc), :] >= thr) | take_all) & (k0 + s_loc <= t_pos)
        bias_ref[pl.ds(k0, kc), :] = jnp.where(sel, 0.0, NEG_BIG).astype(bias_ref.dtype)
        return carry

    lax.fori_loop(0, n_chunks, emit, 0)

    def emit_masked(c, carry):
        bias_ref[pl.ds(pl.multiple_of(c * kc, kc), kc), :] = jnp.full((kc, tq), NEG_BIG, bias_ref.dtype)
        return carry

    lax.fori_loop(n_chunks, seq // kc, emit_masked, 0)


def _dsa_index(aux, batch, seq, topk):
    c_ik = A_IDX_HEADS * A_IDX_DIM // LANES + 2
    tq = min(A_Q_BLOCK, seq)
    kc = min(A_K_BLOCK, seq)
    nq = seq // tq
    return pl.pallas_call(
        functools.partial(_dsa_index_body, topk=topk, tq=tq, kc=kc, seq=seq),
        grid=(batch, nq),
        in_specs=[
            pl.BlockSpec((tq, A_IDX_HEADS * A_IDX_DIM), lambda b, i: (b * nq + i, 0)),
            pl.BlockSpec((seq, LANES), lambda b, i: (b, c_ik)),
            pl.BlockSpec((tq, LANES), lambda b, i: (b * nq + i, c_ik + 1)),
        ],
        out_specs=pl.BlockSpec((None, None, seq, tq), lambda b, i: (b, i, 0, 0)),
        out_shape=jax.ShapeDtypeStruct((batch, nq, seq, tq), BF16),
        scratch_shapes=[pltpu.VMEM((seq, tq), F32)],
        compiler_params=_cparams(("parallel", "parallel")),
        name="dsa_indexer_topk_mask",
    )(aux, aux, aux)


def _dsa_attn_body(qi_ref, kj_ref, q_ref, z_ref, k_ref, v_ref, bias_ref, o_ref, m_ref, l_ref, acc_ref, *, tq, kb):
    t = pl.program_id(1)
    i = qi_ref[t]
    j = kj_ref[t]
    nh = A_HEADS
    dv = A_HEAD_DIM

    @pl.when(j == 0)
    def _init():
        m_ref[...] = jnp.full(m_ref.shape, NEG_BIG, F32)
        l_ref[...] = jnp.zeros(l_ref.shape, F32)
        acc_ref[...] = jnp.zeros(acc_ref.shape, F32)

    for g in range(q_ref.shape[0]):
        k = k_ref[g]
        v_t = v_ref[g].astype(F32).T.astype(BF16)
        bias = bias_ref[g].astype(F32)
        for h0 in range(0, nh, A_HEAD_STAGE):
            hs = range(h0, h0 + A_HEAD_STAGE)
            s = [_dot_nt(k, q_ref[g, :, h * LANES:(h + 1) * LANES]) + bias for h in hs]
            ps, alphas = [], []
            for n, h in enumerate(hs):
                m_old = m_ref[g, h:h + 1, :]
                m_new = jnp.maximum(m_old, jnp.max(s[n], axis=0, keepdims=True))
                alpha = jnp.exp2(m_old - m_new)
                p = jnp.exp2(s[n] - m_new)
                l_ref[g, h:h + 1, :] = alpha * l_ref[g, h:h + 1, :] + jnp.sum(p, axis=0, keepdims=True)
                m_ref[g, h:h + 1, :] = m_new
                ps.append(p.astype(BF16))
                alphas.append(alpha)
            for n, h in enumerate(hs):
                rows = slice(h * dv, (h + 1) * dv)
                acc_ref[g, rows, :] = alphas[n] * acc_ref[g, rows, :] + _dot(v_t, ps[n])

    @pl.when(j == (i * tq + tq - 1) // kb)
    def _fin():
        for g in range(q_ref.shape[0]):
            for h in range(nh):
                sl = slice(h * LANES, (h + 1) * LANES)
                o_t = acc_ref[g, h * dv:(h + 1) * dv, :] * (1.0 / l_ref[g, h:h + 1, :])
                o_ref[g, :, sl] = (o_t.T * z_ref[g, :, sl].astype(F32)).astype(o_ref.dtype)


def _dsa_attn(q, z, aux, bias, batch, seq):
    n = q.shape[0]
    c_k = A_IDX_HEADS * A_IDX_DIM // LANES
    tq = bias.shape[-1]
    kb = min(A_K_BLOCK, seq)
    nq, nk = seq // tq, seq // kb
    pairs = [(i, j) for i in range(nq) for j in range((i * tq + tq - 1) // kb + 1)]
    qi = jnp.asarray([p[0] for p in pairs], jnp.int32)
    kj = jnp.asarray([p[1] for p in pairs], jnp.int32)
    nb = A_BATCH_GROUP if batch % A_BATCH_GROUP == 0 else 1

    def per_batch(x):
        return x.reshape(batch, seq, x.shape[-1])

    grid_spec = pltpu.PrefetchScalarGridSpec(
        num_scalar_prefetch=2,
        grid=(batch // nb, len(pairs)),
        in_specs=[
            pl.BlockSpec((nb, tq, D_INNER), lambda b, t, qi, kj: (b, qi[t], 0)),
            pl.BlockSpec((nb, tq, D_INNER), lambda b, t, qi, kj: (b, qi[t], 0)),
            pl.BlockSpec((nb, kb, LANES), lambda b, t, qi, kj: (b, kj[t], c_k)),
            pl.BlockSpec((nb, kb, LANES), lambda b, t, qi, kj: (b, kj[t], c_k + 1)),
            pl.BlockSpec((nb, None, kb, tq), lambda b, t, qi, kj: (b, qi[t], kj[t], 0)),
        ],
        out_specs=pl.BlockSpec((nb, tq, D_INNER), lambda b, t, qi, kj: (b, qi[t], 0)),
        scratch_shapes=[
            pltpu.VMEM((nb, A_HEADS, tq), F32),
            pltpu.VMEM((nb, A_HEADS, tq), F32),
            pltpu.VMEM((nb, A_HEADS * A_HEAD_DIM, tq), F32),
        ],
    )
    return pl.pallas_call(
        functools.partial(_dsa_attn_body, tq=tq, kb=kb),
        grid_spec=grid_spec,
        out_shape=jax.ShapeDtypeStruct((batch, seq, D_INNER), BF16),
        compiler_params=_cparams(("parallel", "arbitrary")),
        name="dsa_masked_attention",
    )(qi, kj, per_batch(q), per_batch(z), per_batch(aux), per_batch(aux), bias).reshape(n, D_INNER)


def _mixer_a(h, g, w_in, w_out, batch, seq, tabs, zero_aux):
    hd = A_HEADS * A_HEAD_DIM
    hi = A_IDX_HEADS * A_IDX_DIM
    o0 = np.cumsum([0, hd, A_HEAD_DIM, A_HEAD_DIM, hi, A_IDX_DIM, A_IDX_HEADS, D_INNER])
    wq, wk, wv, wiq, wik, wiw, wz = (w_in[:, o0[t]:o0[t + 1]] for t in range(7))
    pad = jnp.zeros((D_MODEL, LANES - A_IDX_HEADS), F32)
    w_aux = jnp.concatenate([wiq, wk, wv, wik, wik, wiw, pad], axis=1).astype(BF16)
    n_iq = hi // LANES
    q = _proj(h, g, wq.astype(BF16), "rot128", seq, tabs, scales=[A_HEAD_DIM ** -0.5 * LOG2E] * A_HEADS)
    z = _proj(h, g, wz.astype(BF16), "silu", seq, tabs)
    aux = _proj(h, g, w_aux, ["rot64"] * n_iq + ["rot128", "none", "rot64", "none"], seq, tabs,
                scales=[1.0] * (n_iq + 3) + [A_IDX_HEADS ** -0.5 * A_IDX_DIM ** -0.5])
    bias = _dsa_index(aux, batch, seq, min(A_TOPK_MAX, seq // 4))
    return _dsa_attn(q, z, aux, bias, batch, seq)


def _band_body(*refs, blk, band_prev, cfg_prev, final):
    refs = list(refs)
    q_ref, kc_ref, vc_ref = refs[:3]
    del refs[:3]
    if band_prev:
        kp_ref, vp_ref = refs[-2:]
        del refs[-2:]

        @pl.when(pl.program_id(2) == 0)
        def _no_prev():
            kp_ref[...] = jnp.zeros(kp_ref.shape, kp_ref.dtype)
            vp_ref[...] = jnp.zeros(vp_ref.shape, vp_ref.dtype)
    nb = q_ref.shape[0]
    if cfg_prev:
        op4_ref, lp4_ref = refs[:2]
        del refs[:2]
        op_ref, lp_ref = refs[-2:]
        del refs[-2:]
        for g in range(nb):
            for a in range(B_DIL_RATIO):
                rows = pl.ds(a, blk // B_DIL_RATIO, stride=B_DIL_RATIO)
                lp_ref[g, rows, :] = lp4_ref[a, g]
                for h in range(B_HEADS):
                    op_ref[g, h, rows, :] = op4_ref[a, g, :, h * LANES:(h + 1) * LANES].astype(F32)
    if final:
        z_ref, o_ref = refs
    else:
        o_ref, l_ref = refs
    ki = lax.broadcasted_iota(jnp.int32, (blk, blk), 0)
    qi = lax.broadcasted_iota(jnp.int32, (blk, blk), 1)
    mask_cur = ki <= qi
    eye = ki == qi
    if band_prev:
        mask_prev = (ki >= qi) & (pl.program_id(2) > 0)
    if cfg_prev:
        lse_prev_t = [_pad_rows(lp_ref[g].T, LANES) for g in range(nb)]
    items = [(g, slice(h * LANES, (h + 1) * LANES)) for g in range(nb) for h in range(B_HEADS)]
    s_c = [jnp.where(mask_cur, _dot_nt(kc_ref[g, :, sl], q_ref[g, :, sl]), NEG_BIG) for g, sl in items]
    if band_prev:
        s_p = [jnp.where(mask_prev, _dot_nt(kp_ref[g, :, sl], q_ref[g, :, sl]), NEG_BIG) for g, sl in items]
    lse_rows, p_c, p_p, d_old = [], [], [], []
    for n, (g, sl) in enumerate(items):
        m = jnp.max(s_c[n], axis=0, keepdims=True)
        if band_prev:
            m = jnp.maximum(m, jnp.max(s_p[n], axis=0, keepdims=True))
        e_c = jnp.exp2(s_c[n] - m)
        den = jnp.sum(e_c, axis=0, keepdims=True)
        if band_prev:
            e_p = jnp.exp2(s_p[n] - m)
            den += jnp.sum(e_p, axis=0, keepdims=True)
        lse = m + jnp.log2(den)
        scale = 1.0 / den
        if cfg_prev:
            h = n % B_HEADS
            lse_prev = lse_prev_t[g][h:h + 1, :]
            m2 = jnp.maximum(lse, lse_prev)
            w_new, w_old = jnp.exp2(lse - m2), jnp.exp2(lse_prev - m2)
            scale = scale * w_new / (w_new + w_old)
            d_old.append(jnp.where(eye, w_old / (w_new + w_old), 0.0).astype(BF16))
            lse = m2 + jnp.log2(w_new + w_old)
        p_c.append((e_c * scale).astype(BF16))
        if band_prev:
            p_p.append((e_p * scale).astype(BF16))
        lse_rows.append(lse)
    for n, (g, sl) in enumerate(items):
        o = _dot_tn(p_c[n], vc_ref[g, :, sl])
        if band_prev:
            o += _dot_tn(p_p[n], vp_ref[g, :, sl])
        if cfg_prev:
            o += _dot(d_old[n], op_ref[g, n % B_HEADS].astype(BF16))
        if final:
            o_ref[g, :, sl] = (o * z_ref[g, :, sl].astype(F32)).astype(o_ref.dtype)
        else:
            o_ref[g, :, sl] = o.astype(o_ref.dtype)
    if not final:
        for g in range(nb):
            lse_t = _pad_rows(jnp.concatenate(lse_rows[g * B_HEADS:(g + 1) * B_HEADS], axis=0), LANES)
            l_ref[g] = lse_t.T
    if band_prev:
        kp_ref[...] = kc_ref[...]
        vp_ref[...] = vc_ref[...]


def _pad_rows(x, rows):
    if x.shape[0] == rows:
        return x
    return jnp.concatenate([x, jnp.zeros((rows - x.shape[0], x.shape[1]), x.dtype)], axis=0)


def _band_attn(qk, v, gate, batch, seq, dil, prev):
    final = gate is not None
    length = seq // dil
    blk = min(128, length)
    nbl = length // blk
    band_prev = nbl > 1
    nb = B_BATCH_GROUP if batch % B_BATCH_GROUP == 0 else 1

    def per_batch(x, classes=dil):
        return x.reshape(classes, batch, x.shape[1] // batch, x.shape[2])

    def spec(col, width=D_INNER):
        return pl.BlockSpec((None, nb, blk, width), lambda b, r, i: (r, b, i, col))

    o_spec, l_spec = spec(0), spec(0, LANES)
    in_specs, args, scratch = [spec(0), spec(1), spec(0)], [per_batch(qk), per_batch(qk), per_batch(v)], []
    if prev is not None:
        coarse = B_DIL_RATIO * dil
        assert prev[0].shape[0] == coarse and blk % (16 * B_DIL_RATIO) == 0
        pblk = blk // B_DIL_RATIO
        in_specs += [pl.BlockSpec((B_DIL_RATIO, None, nb, pblk, D_INNER), lambda b, r, i: (0, r, b, i, 0)),
                     pl.BlockSpec((B_DIL_RATIO, None, nb, pblk, LANES), lambda b, r, i: (0, r, b, i, 0))]
        args += [x.reshape(B_DIL_RATIO, dil, batch, length // B_DIL_RATIO, x.shape[-1]) for x in prev]
        scratch = [pltpu.VMEM((nb, B_HEADS, blk, LANES), F32), pltpu.VMEM((nb, blk, LANES), F32)]
    if band_prev:
        scratch += [pltpu.VMEM((nb, blk, D_INNER), BF16), pltpu.VMEM((nb, blk, D_INNER), BF16)]
    o_shape = jax.ShapeDtypeStruct((dil, batch, length, D_INNER), BF16)
    if final:
        in_specs.append(spec(0))
        args.append(per_batch(gate))
        out_specs, out_shape = o_spec, o_shape
    else:
        out_specs = [o_spec, l_spec]
        out_shape = [o_shape, jax.ShapeDtypeStruct((dil, batch, length, LANES), F32)]
    res = pl.pallas_call(
        functools.partial(_band_body, blk=blk, band_prev=band_prev, cfg_prev=prev is not None, final=final),
        grid=(batch // nb, dil, nbl),
        in_specs=in_specs,
        out_specs=out_specs,
        out_shape=out_shape,
        scratch_shapes=scratch,
        compiler_params=_cparams(("parallel", "parallel", "arbitrary")),
        name="dilated_band_attention",
    )(*args)
    if final:
        return res.reshape(dil, batch * length, D_INNER)
    return [x.reshape(dil, batch * length, x.shape[-1]) for x in res]


def _mixer_b(h, g, w_in, w_out, batch, seq, tabs, zero_aux):
    configs = sorted(B_CONFIGS, key=lambda wd: -wd[1])
    dils = [dil for _, dil in configs]
    assert dils[-1] == 1 and all(a == B_DIL_RATIO * b for a, b in zip(dils, dils[1:]))
    assert all(window // dil == 128 and seq % dil == 0 for window, dil in configs)
    w = w_in.astype(BF16)
    qk = _proj(h, g, w[:, :2 * D_INNER], "rot128", seq, tabs, res_dils=dils[:-1],
               scales=[B_HEAD_DIM ** -0.5 * LOG2E] * B_HEADS + [1.0] * B_HEADS)
    v = _proj(h, g, w[:, 2 * D_INNER:3 * D_INNER], "none", seq, tabs, res_dils=dils[:-1])
    z = _proj(h, g, w[:, 3 * D_INNER:], "silu", seq, tabs)
    qks = list(qk[1:]) + [qk[0][None]]
    vs = list(v[1:]) + [v[0][None]]
    prev = None
    for qk_d, v_d, dil in zip(qks, vs, dils):
        prev = _band_attn(qk_d, v_d, z[None] if dil == 1 else None, batch, seq, dil, prev)
    return prev.reshape(-1, D_INNER)


def _retention_body(lg_ref, q_ref, k_ref, v_ref, z_ref, o_ref, st_ref, *, seq):
    c = C_CHUNK
    lg = lg_ref[pl.program_id(1)]
    ri = lax.broadcasted_iota(jnp.int32, (c, c), 0)
    ci = lax.broadcasted_iota(jnp.int32, (c, c), 1)
    rel = (ri - ci).astype(F32)
    decay = jnp.where(rel >= 0, jnp.exp(jnp.maximum(rel, 0.0) * lg), 0.0)
    idx = lax.broadcasted_iota(jnp.int32, (c, 1), 0).astype(F32)
    k_decay = jnp.exp((c - 1 - idx) * lg)
    q_decay = jnp.exp((idx + 1.0) * lg)
    chunk_decay = jnp.exp(jnp.full((1, C_QK_DIM), c, F32) * lg)
    st_ref[...] = jnp.zeros(st_ref.shape, F32)

    unroll = min(C_UNROLL, seq // c)

    def group(gidx, carry):
        rows0 = [pl.multiple_of((gidx * unroll + u) * c, c) for u in range(unroll)]
        qs = [q_ref[pl.ds(r0, c), :] for r0 in rows0]
        ks = [k_ref[pl.ds(r0, c), :] for r0 in rows0]
        vs = [v_ref[pl.ds(r0, c), :] for r0 in rows0]
        scores = [(_dot_nt(q, k) * decay).astype(BF16) for q, k in zip(qs, ks)]
        o_intra = [_dot(s, v) for s, v in zip(scores, vs)]
        q_dec = [(q.astype(F32) * q_decay).astype(BF16) for q in qs]
        k_dec = [(k.astype(F32) * k_decay).astype(BF16) for k in ks]
        outs = []
        for u in range(unroll):
            st = st_ref[...]
            outs.append(o_intra[u] + _dot_nt(q_dec[u], st.astype(BF16)))
            st_ref[...] = st * chunk_decay + _dot_tn(vs[u], k_dec[u])
        for u, r0 in enumerate(rows0):
            o = outs[u]
            o = o * lax.rsqrt(jnp.mean(o * o, axis=-1, keepdims=True) + NORM_EPS)
            o_ref[pl.ds(r0, c), :] = (o * z_ref[pl.ds(r0, c), :].astype(F32)).astype(o_ref.dtype)
        return carry

    lax.fori_loop(0, seq // (c * unroll), group, 0)


def _mixer_c(h, g, w_in, w_out, batch, seq, tabs, zero_aux):
    w = w_in.astype(BF16)
    nh = C_HEADS
    dqk = 2 * nh * C_QK_DIM
    qk = _proj(h, g, w[:, :dqk], "rot128", seq, tabs, scales=[1.0] * nh + [C_QK_DIM ** -0.5] * nh)
    v = _proj(h, g, w[:, dqk:dqk + D_INNER], "none", seq, tabs)
    z = _proj(h, g, w[:, dqk + D_INNER:], "silu", seq, tabs)
    n = qk.shape[0]
    log_gamma = jnp.log1p(-jnp.exp2(-5.0 - jnp.arange(nh, dtype=F32)))
    return pl.pallas_call(
        functools.partial(_retention_body, seq=seq),
        grid=(batch, nh),
        in_specs=[
            pl.BlockSpec(memory_space=pltpu.SMEM),
            pl.BlockSpec((seq, C_QK_DIM), lambda b, h: (b, h)),
            pl.BlockSpec((seq, C_QK_DIM), lambda b, h: (b, nh + h)),
            pl.BlockSpec((seq, C_V_DIM), lambda b, h: (b, h)),
            pl.BlockSpec((seq, C_V_DIM), lambda b, h: (b, h)),
        ],
        out_specs=pl.BlockSpec((seq, C_V_DIM), lambda b, h: (b, h)),
        out_shape=jax.ShapeDtypeStruct((n, D_INNER), BF16),
        scratch_shapes=[pltpu.VMEM((C_V_DIM, C_QK_DIM), F32)],
        compiler_params=_cparams(("parallel", "parallel")),
        name="retention",
    )(log_gamma, qk, qk, v, z)


def _hgrn_body(q_ref, g_ref, v_ref, z_ref, o_ref, st_ref, *, seq):
    c = D_CHUNK
    w = D_HEAD_GROUP * LANES
    halves = [D_SUB * 2 ** lv for lv in range(16) if D_SUB * 2 ** (lv + 1) <= c]
    ri = lax.broadcasted_iota(jnp.int32, (c, c), 0)
    ci = lax.broadcasted_iota(jnp.int32, (c, c), 1)
    mats = [ri >= ci]
    pair_masks = []
    for half in halves:
        block = 2 * half
        mid = (ri // block) * block + half - 1
        right = (ri % block) >= half
        mats.append((right & (ci > mid) & (ci <= ri)) | (jnp.logical_not(right) & (ci > ri) & (ci <= mid)))
        pair_masks.append(((ri // block) == (ci // block)) & right & ((ci % block) < half))
    diag_masks = []
    for j in range(D_SUB):
        key = (ri // D_SUB) * D_SUB + j
        mats.append((ci > key) & (ci <= ri))
        diag_masks.append((ci == key) & (ri >= key))
    m1 = jnp.concatenate([m.astype(BF16) for m in mats], axis=0)
    m3 = jnp.concatenate([m1, m1, m1], axis=1)
    rows = lax.broadcasted_iota(jnp.int32, (c, w), 0)
    right_rows = [(rows % (2 * half)) >= half for half in halves]
    n_lv = len(halves)
    st_ref[...] = jnp.zeros(st_ref.shape, F32)

    head_sl = [slice(hd * LANES, (hd + 1) * LANES) for hd in range(D_HEAD_GROUP)]
    unroll = min(D_UNROLL, seq // c)

    def group_rows(gidx):
        return [pl.multiple_of((gidx * unroll + u) * c, c) for u in range(unroll)]

    def prepare(gidx):
        rows0 = group_rows(gidx)
        gates = [g_ref[pl.ds(r0, c), :] for r0 in rows0]
        bds = []
        for gate in gates:
            lf = jnp.log2(gate)
            hi = lf.astype(BF16)
            r1 = lf - hi.astype(F32)
            mid_part = r1.astype(BF16)
            lo = (r1 - mid_part.astype(F32)).astype(BF16)
            bds.append(_dot(m3, jnp.concatenate([hi, mid_part, lo], axis=0)))
        prep = []
        for u, r0 in enumerate(rows0):
            bd, kk = bds[u], 1.0 - gates[u]
            b = bd[0:c]
            q = q_ref[pl.ds(r0, c), :].astype(F32)
            v = v_ref[pl.ds(r0, c), :]
            qe = (q * jnp.exp2(b)).astype(BF16)
            qts, kts = [], []
            for lv in range(len(halves)):
                e = jnp.exp2(bd[(lv + 1) * c:(lv + 2) * c])
                qts.append(jnp.where(right_rows[lv], q * e, 0.0).astype(BF16))
                kts.append(jnp.where(right_rows[lv], 0.0, kk * e).astype(BF16))
            q_diag = jnp.concatenate(
                [(q * jnp.exp2(bd[(1 + n_lv + j) * c:(2 + n_lv + j) * c])).astype(BF16) for j in range(D_SUB)], axis=0)
            b_last = b[c - 1:c, :]
            k_dec = (kk * jnp.exp2(b_last - b)).astype(BF16)
            prep.append((qe, qts, kts, v, (q_diag, kk.astype(BF16)), k_dec, jnp.exp2(b_last)))
        o_local = []
        for qe, qts, kts, v, (q_diag, kk16), k_dec, eb_last in prep:
            per_head = []
            for hd, sl in enumerate(head_sl):
                a = jnp.zeros((c, c), F32)
                for lv in range(n_lv):
                    a = a + jnp.where(pair_masks[lv], _dot_nt(qts[lv][:, sl], kts[lv][:, sl]), 0.0)
                g_diag = _dot_nt(q_diag[:, sl], kk16[:, sl])
                for j in range(D_SUB):
                    a = a + jnp.where(diag_masks[j], g_diag[j * c:(j + 1) * c], 0.0)
                per_head.append(_dot(a.astype(BF16), v[:, sl]))
            o_local.append(per_head)
        return [(o_local[u], p[0], p[3], p[5], p[6]) for u, p in enumerate(prep)]

    def finish(gidx, prepared):
        outs = []
        for o_loc, qe, v, k_dec, eb_last in prepared:
            per_head = []
            for hd, sl in enumerate(head_sl):
                st = st_ref[hd]
                per_head.append(o_loc[hd] + _dot_nt(qe[:, sl], st.astype(BF16)))
                st_ref[hd] = st * eb_last[:, sl] + _dot_tn(v[:, sl], k_dec[:, sl])
            outs.append(per_head)
        for u, r0 in enumerate(group_rows(gidx)):
            for hd, sl in enumerate(head_sl):
                o = outs[u][hd]
                o = o * lax.rsqrt(jnp.mean(o * o, axis=-1, keepdims=True) + NORM_EPS)
                o_ref[pl.ds(r0, c), sl] = (o * z_ref[pl.ds(r0, c), sl].astype(F32)).astype(o_ref.dtype)

    def trip(gidx, carry):
        finish(gidx, prepare(gidx))
        return carry

    lax.fori_loop(0, seq // (c * unroll), trip, 0)


def _mixer_d(h, g, w_in, w_out, lower_bound, batch, seq, tabs, zero_aux):
    wq, wf, wi, wz = (w_in[:, t * D_INNER:(t + 1) * D_INNER] for t in range(4))
    qz = _proj(h, g, jnp.concatenate([wq, wz], axis=1).astype(BF16), "silu", seq, tabs,
               scales=[D_K_DIM ** -0.5] * D_HEADS + [1.0] * D_HEADS)
    v = _proj(h, g, wi.astype(BF16), "none", seq, tabs)
    gates = _proj(h, g, wf.astype(BF16), "hgrn_g", seq, tabs, out_dtype=F32,
                  aux=lower_bound.reshape(1, D_INNER).astype(F32))
    n = qz.shape[0]
    ng = D_HEADS // D_HEAD_GROUP
    w = D_HEAD_GROUP * LANES
    return pl.pallas_call(
        functools.partial(_hgrn_body, seq=seq),
        grid=(batch, ng),
        in_specs=[
            pl.BlockSpec((seq, w), lambda b, h: (b, h)),
            pl.BlockSpec((seq, w), lambda b, h: (b, h)),
            pl.BlockSpec((seq, w), lambda b, h: (b, h)),
            pl.BlockSpec((seq, w), lambda b, h: (b, ng + h)),
        ],
        out_specs=pl.BlockSpec((seq, w), lambda b, h: (b, h)),
        out_shape=jax.ShapeDtypeStruct((n, D_INNER), BF16),
        scratch_shapes=[pltpu.VMEM((D_HEAD_GROUP, D_V_DIM, D_K_DIM), F32)],
        compiler_params=_cparams(("parallel", "parallel")),
        name="hgrn2_recurrence",
    )(qz, gates, v, qz)


def kernel(x, norm_g, final_g, a_w_in, a_w_out, b_w_in, b_w_out, c_w_in, c_w_out, d_w_in, d_w_out, hgrn_lb_logits):
    batch, seq, d = x.shape
    depth = norm_g.shape[0]
    lb_cum = jnp.cumsum(jax.nn.softmax(hgrn_lb_logits.astype(F32), axis=0), axis=0)
    lower_bounds = lb_cum - lb_cum[0]
    tabs = _rope_tables(seq, 128) + _rope_tables(seq, 64)
    h = x.reshape(batch * seq, d)
    for layer in range(depth):
        kind, slot = layer % N_MIXERS, layer // N_MIXERS
        g = norm_g[layer]
        if kind == 0:
            og, w_out = _mixer_a(h, g, a_w_in[slot], a_w_out[slot], batch, seq, tabs, None), a_w_out[slot]
        elif kind == 1:
            og, w_out = _mixer_b(h, g, b_w_in[slot], b_w_out[slot], batch, seq, tabs, None), b_w_out[slot]
        elif kind == 2:
            og, w_out = _mixer_c(h, g, c_w_in[slot], c_w_out[slot], batch, seq, tabs, None), c_w_out[slot]
        else:
            og, w_out = _mixer_d(h, g, d_w_in[slot], d_w_out[slot], lower_bounds[layer], batch, seq, tabs, None), d_w_out[slot]
        h = _outproj(og, w_out.astype(BF16), h, final_g, final=(layer == depth - 1))
    return h.reshape(batch, seq, d)
```
